```python
import math
import jax, jax.numpy as jnp
from jax import lax
import numpy as np

D_MODEL = 1024
BATCH = 2
SEQ = 8192
DEPTH = 2
DEC_BATCH = 128
DEC_SEQ = 8
PAST_LEN = 8192
PAGE_SIZE = 128

BRANCH_WIDTH = 256
N_BRANCH = 4
A_HEADS = 4
A_NOPE = 64
A_ROPE = 32
A_V = 64
A_Q_LORA = 256
A_KV_LORA = 128
ROPE_THETA = 10000.0
Q_BLOCK = 128
B_WIDTH = BRANCH_WIDTH
B_KERNEL = 31
C_WIDTH = BRANCH_WIDTH
C_KERNEL = 3
D_HEADS = 4
D_KEY = 64
D_VAL = 64
D_CHUNK = 64
P_HEADS = 8
P_NKEYS = 128
P_DKEY = 256
P_TOPK = 16
P_EXPERTS = P_NKEYS * P_NKEYS
P_TOKEN_BLOCK = 256
EPS = 1e-6
NEG_BIG = -1e30
IN_SIZES = (A_Q_LORA, A_KV_LORA, A_ROPE, 2 * B_WIDTH, 3 * C_WIDTH, D_HEADS * D_KEY, D_HEADS * D_KEY, D_HEADS * D_VAL, D_HEADS * D_VAL, N_BRANCH * D_MODEL)
IN_COLS = sum(IN_SIZES)

kernel_name = 'hybrid_mla_conv_hgrn2_peer_step'

F32 = jnp.float32


def _rmsnorm(x, g):
    xf = x.astype(F32)
    y = xf * lax.rsqrt(jnp.mean(xf * xf, axis=-1, keepdims=True) + EPS)
    return (y * g.astype(F32)).astype(x.dtype)


def _layernorm(x, g, b):
    xf = x.astype(F32)
    mu = jnp.mean(xf, axis=-1, keepdims=True)
    var = jnp.mean(jnp.square(xf - mu), axis=-1, keepdims=True)
    y = (xf - mu) * lax.rsqrt(var + EPS)
    return (y * g.astype(F32) + b.astype(F32)).astype(x.dtype)


def _rope(x, pos):
    half = A_ROPE // 2
    inv = ROPE_THETA ** (-jnp.arange(half, dtype=F32) / half)
    ang = pos.astype(F32)[:, None] * inv[None, :]
    shape = (ang.shape[0],) + (1,) * (x.ndim - 3) + (half,)
    cos = jnp.cos(ang).reshape(shape)
    sin = jnp.sin(ang).reshape(shape)
    x1 = x[..., :half].astype(F32)
    x2 = x[..., half:].astype(F32)
    return jnp.concatenate([x1 * cos - x2 * sin, x2 * cos + x1 * sin], axis=-1).astype(x.dtype)


def _causal_dwconv(u, hist, w):
    k = w.shape[0]
    xp = jnp.concatenate([hist.astype(u.dtype), u], axis=1)
    y = lax.conv_general_dilated(xp, w[:, None, :].astype(u.dtype), window_strides=(1,), padding='VALID',
                                 dimension_numbers=('NWC', 'WIO', 'NWC'), feature_group_count=u.shape[-1])
    return y, xp[:, xp.shape[1] - (k - 1):]


def _mla_prompt(q_nope, q_rope, ckv, krope, w_uk, w_uv):
    bsz, t = q_nope.shape[:2]
    scale = 1.0 / math.sqrt(A_NOPE + A_ROPE)
    k_nope = jnp.einsum('btc,chn->bthn', ckv, w_uk)
    v = jnp.einsum('btc,chv->bthv', ckv, w_uv)
    blk = math.gcd(t, Q_BLOCK)
    nb = t // blk
    kpos = jnp.arange(t)

    def block(args):
        qn, qr, qpos = args
        s = (jnp.einsum('bqhn,bkhn->bhqk', qn, k_nope) + jnp.einsum('bqhr,bkr->bhqk', qr, krope)).astype(F32) * scale
        s = jnp.where(kpos[None, :] <= qpos[:, None], s, NEG_BIG)
        pr = jax.nn.softmax(s, axis=-1).astype(v.dtype)
        return jnp.einsum('bhqk,bkhv->bqhv', pr, v)

    qn_b = q_nope.reshape(bsz, nb, blk, A_HEADS, A_NOPE).transpose(1, 0, 2, 3, 4)
    qr_b = q_rope.reshape(bsz, nb, blk, A_HEADS, A_ROPE).transpose(1, 0, 2, 3, 4)
    out = lax.map(block, (qn_b, qr_b, jnp.arange(t).reshape(nb, blk)))
    return out.transpose(1, 0, 2, 3, 4).reshape(bsz, t, A_HEADS, A_V)


def _mla_sample(q_nope, q_rope, ckv, krope, ckv_past, krope_past, w_uk, w_uv):
    t = q_nope.shape[1]
    p_len = ckv_past.shape[1]
    scale = 1.0 / math.sqrt(A_NOPE + A_ROPE)
    q_lat = jnp.einsum('bthn,chn->bthc', q_nope, w_uk)
    s_past = jnp.einsum('bthc,bpc->bhtp', q_lat, ckv_past) + jnp.einsum('bthr,bpr->bhtp', q_rope, krope_past)
    s_new = jnp.einsum('bthc,bsc->bhts', q_lat, ckv) + jnp.einsum('bthr,bsr->bhts', q_rope, krope)
    causal = jnp.tril(jnp.ones((t, t), dtype=bool))
    s = jnp.concatenate([s_past.astype(F32), s_new.astype(F32)], axis=-1) * scale
    mask = jnp.concatenate([jnp.ones((t, p_len), dtype=bool), causal], axis=-1)
    s = jnp.where(mask, s, NEG_BIG)
    pr = jax.nn.softmax(s, axis=-1).astype(ckv.dtype)
    o_lat = jnp.einsum('bhtp,bpc->bthc', pr[..., :p_len], ckv_past) + jnp.einsum('bhts,bsc->bthc', pr[..., p_len:], ckv)
    return jnp.einsum('bthc,chv->bthv', o_lat, w_uv)


def _hgrn2_chunked(q, k, v, logf, s0):
    bsz, t = q.shape[:2]
    c = t if t <= D_CHUNK else math.gcd(t, D_CHUNK)
    n = t // c

    def to_chunks(a):
        return a.astype(F32).reshape(bsz, n, c, D_HEADS, a.shape[-1]).transpose(1, 0, 3, 2, 4)

    mask = jnp.tril(jnp.ones((c, c), dtype=bool))[:, :, None]

    def step(s, inp):
        qc, kc, vc, gc = inp
        g = jnp.cumsum(gc, axis=2)
        o_inter = jnp.einsum('bhck,bhkv->bhcv', qc * jnp.exp(g), s)
        diff = g[:, :, :, None, :] - g[:, :, None, :, :]
        decay = jnp.where(mask, jnp.exp(jnp.where(mask, diff, 0.0)), 0.0)
        att = jnp.einsum('bhtk,bhsk,bhtsk->bhts', qc, kc, decay)
        o = o_inter + jnp.einsum('bhts,bhsv->bhtv', att, vc)
        g_last = g[:, :, -1:, :]
        s_new = jnp.exp(g_last[:, :, 0, :])[..., None] * s + jnp.einsum('bhsk,bhsv->bhkv', kc * jnp.exp(g_last - g), vc)
        return s_new, o

    s_fin, o = lax.scan(step, s0.astype(F32), (to_chunks(q), to_chunks(k), to_chunks(v), to_chunks(logf)))
    o = o.transpose(1, 0, 3, 2, 4).reshape(bsz, t, D_HEADS, D_VAL)
    return o, s_fin


def _token_mixers(h, pos, p, past):
    bsz, t, _ = h.shape
    z = h @ p['w_in']
    offs = np.cumsum(IN_SIZES)[:-1].tolist()
    cq, ckv_raw, kr_raw, b_in, c_in, d_q, d_f, d_i, d_g, gate_in = jnp.split(z, offs, axis=-1)

    q = (_rmsnorm(cq, p['a_qn']) @ p['a_wuq']).reshape(bsz, t, A_HEADS, A_NOPE + A_ROPE)
    q_nope = q[..., :A_NOPE]
    q_rope = _rope(q[..., A_NOPE:], pos)
    ckv = _rmsnorm(ckv_raw, p['a_kvn'])
    krope = _rope(kr_raw, pos)
    if past is None:
        o_a = _mla_prompt(q_nope, q_rope, ckv, krope, p['a_wuk'], p['a_wuv'])
        hist_b = jnp.zeros((bsz, B_KERNEL - 1, B_WIDTH), h.dtype)
        hist_c = jnp.zeros((bsz, C_KERNEL - 1, C_WIDTH), h.dtype)
        s0 = jnp.zeros((bsz, D_HEADS, D_KEY, D_VAL), F32)
    else:
        ckv_past, krope_past, hist_b, hist_c, s0 = past
        o_a = _mla_sample(q_nope, q_rope, ckv, krope, ckv_past.astype(ckv.dtype), krope_past.astype(krope.dtype), p['a_wuk'], p['a_wuv'])
    o_a = o_a.reshape(bsz, t, A_HEADS * A_V)

    glu = b_in[..., :B_WIDTH] * jax.nn.sigmoid(b_in[..., B_WIDTH:])
    conv_b, new_hist_b = _causal_dwconv(glu, hist_b, p['b_cw'])
    o_b = jax.nn.silu(_layernorm(conv_b + p['b_cb'], p['b_lng'], p['b_lnb']))

    gb, gc, cx = jnp.split(c_in, 3, axis=-1)
    conv_c, new_hist_c = _causal_dwconv(gc * cx, hist_c, p['c_cw'])
    o_c = gb * conv_c

    lb = p['d_lb']
    qd = jax.nn.silu(d_q).reshape(bsz, t, D_HEADS, D_KEY)
    f = lb + (1.0 - lb) * jax.nn.sigmoid(d_f.astype(F32))
    logf = jnp.log(f).reshape(bsz, t, D_HEADS, D_KEY)
    kd = (1.0 - f).reshape(bsz, t, D_HEADS, D_KEY)
    vd = d_i.reshape(bsz, t, D_HEADS, D_VAL)
    od, s_fin = _hgrn2_chunked(qd, kd, vd, logf, s0)
    od = _rmsnorm(od.astype(h.dtype), p['d_gn']) * jax.nn.silu(d_g.reshape(bsz, t, D_HEADS, D_VAL))
    o_d = od.reshape(bsz, t, D_HEADS * D_VAL)

    branches = jnp.stack([o_a, o_b, o_c, o_d], axis=2)
    yb = jnp.einsum('btnc,ncd->btnd', branches, p['w_branch'])
    gates = jax.nn.sigmoid(gate_in.reshape(bsz, t, N_BRANCH, D_MODEL))
    merged = jnp.sum(gates * yb, axis=2)
    new_state = (ckv, krope, new_hist_b, new_hist_c, s_fin.astype(h.dtype))
    return merged @ p['w_out'], new_state


def _peer(h, w_q, sub_keys, u_tab, v_tab):
    bsz, t, d = h.shape
    n = bsz * t
    blk = math.gcd(n, P_TOKEN_BLOCK)

    def block(xb):
        q = (xb @ w_q).reshape(blk, P_HEADS, 2, P_DKEY // 2)
        s = jnp.einsum('thpd,hpkd->thpk', q, sub_keys).astype(F32)
        sv, si = lax.top_k(s, P_TOPK)
        cand = (sv[:, :, 0, :, None] + sv[:, :, 1, None, :]).reshape(blk, P_HEADS, P_TOPK * P_TOPK)
        cidx = (si[:, :, 0, :, None] * P_NKEYS + si[:, :, 1, None, :]).reshape(blk, P_HEADS, P_TOPK * P_TOPK)
        fv, fi = lax.top_k(cand, P_TOPK)
        eidx = jnp.take_along_axis(cidx, fi, axis=-1)
        g = jax.nn.softmax(fv, axis=-1)
        a = jax.nn.gelu(jnp.einsum('thkd,td->thk', u_tab[eidx], xb).astype(F32), approximate=False)
        return jnp.einsum('thk,thkd->td', (g * a).astype(xb.dtype), v_tab[eidx])

    out = lax.map(block, h.reshape(n // blk, blk, d))
    return out.reshape(bsz, t, d)


def setup_inputs(seed: int = 0) -> dict:
    key = jax.random.key(seed)
    ks = iter(jax.random.split(key, 40))

    def nrm(shape, scale):
        return jax.random.normal(next(ks), shape, F32) * scale

    def gain(shape):
        return 1.0 + 0.01 * jax.random.normal(next(ks), shape, F32)

    n_pages = PAST_LEN // PAGE_SIZE
    n_used = DEC_BATCH * n_pages
    n_pool = n_used + n_used // 4
    perm = jax.random.permutation(next(ks), n_pool)
    page_table = perm[:n_used].reshape(DEC_BATCH, n_pages).astype(jnp.int32)
    return {
        'x_prompt': nrm((BATCH, SEQ, D_MODEL), 1.0),
        'x_sample': nrm((DEC_BATCH, DEC_SEQ, D_MODEL), 1.0),
        'cache_ckv': nrm((DEPTH, n_pool, PAGE_SIZE, A_KV_LORA), 1.0),
        'cache_krope': nrm((DEPTH, n_pool, PAGE_SIZE, A_ROPE), 1.0),
        'page_table': page_table,
        'state_conv_b': nrm((DEPTH, DEC_BATCH, B_KERNEL - 1, B_WIDTH), 0.5),
        'state_conv_c': nrm((DEPTH, DEC_BATCH, C_KERNEL - 1, C_WIDTH), 0.5),
        'state_hgrn': nrm((DEPTH, DEC_BATCH, D_HEADS, D_KEY, D_VAL), 1.0),
        'norm1_g': gain((DEPTH, D_MODEL)),
        'w_in': nrm((DEPTH, D_MODEL, IN_COLS), D_MODEL ** -0.5),
        'a_q_norm_g': gain((DEPTH, A_Q_LORA)),
        'a_w_uq': nrm((DEPTH, A_Q_LORA, A_HEADS * (A_NOPE + A_ROPE)), A_Q_LORA ** -0.5),
        'a_kv_norm_g': gain((DEPTH, A_KV_LORA)),
        'a_w_uk': nrm((DEPTH, A_KV_LORA, A_HEADS, A_NOPE), A_KV_LORA ** -0.5),
        'a_w_uv': nrm((DEPTH, A_KV_LORA, A_HEADS, A_V), A_KV_LORA ** -0.5),
        'b_conv_w': nrm((DEPTH, B_KERNEL, B_WIDTH), B_KERNEL ** -0.5),
        'b_conv_b': nrm((DEPTH, B_WIDTH), 0.01),
        'b_ln_g': gain((DEPTH, B_WIDTH)),
        'b_ln_b': nrm((DEPTH, B_WIDTH), 0.01),
        'c_conv_w': nrm((DEPTH, C_KERNEL, C_WIDTH), C_KERNEL ** -0.5),
        'd_lower_bound': nrm((DEPTH, D_HEADS * D_KEY), 0.5),
        'd_gnorm_g': gain((DEPTH, D_VAL)),
        'w_branch': nrm((DEPTH, N_BRANCH, BRANCH_WIDTH, D_MODEL), BRANCH_WIDTH ** -0.5),
        'w_out': nrm((DEPTH, D_MODEL, D_MODEL), D_MODEL ** -0.5),
        'norm2_g': gain((DEPTH, D_MODEL)),
        'p_w_q': nrm((DEPTH, D_MODEL, P_HEADS * P_DKEY), D_MODEL ** -0.5),
        'p_sub_keys': nrm((DEPTH, P_HEADS, 2, P_NKEYS, P_DKEY // 2), (P_DKEY // 2) ** -0.5),
        'p_u': nrm((DEPTH, P_EXPERTS, D_MODEL), D_MODEL ** -0.5),
        'p_v': nrm((DEPTH, P_EXPERTS, D_MODEL), 0.25),
        'final_norm_g': gain((D_MODEL,)),
    }


def reference(x_prompt, x_sample, cache_ckv, cache_krope, page_table, state_conv_b, state_conv_c, state_hgrn,
              norm1_g, w_in, a_q_norm_g, a_w_uq, a_kv_norm_g, a_w_uk, a_w_uv, b_conv_w, b_conv_b, b_ln_g, b_ln_b,
              c_conv_w, d_lower_bound, d_gnorm_g, w_branch, w_out, norm2_g, p_w_q, p_sub_keys, p_u, p_v, final_norm_g):
    lb_soft = jax.nn.softmax(d_lower_bound.astype(F32), axis=0)
    lower_bounds = jnp.cumsum(lb_soft, axis=0) - lb_soft[0:1]

    def layer_params(l):
        return dict(w_in=w_in[l], a_qn=a_q_norm_g[l], a_wuq=a_w_uq[l], a_kvn=a_kv_norm_g[l], a_wuk=a_w_uk[l],
                    a_wuv=a_w_uv[l], b_cw=b_conv_w[l], b_cb=b_conv_b[l], b_lng=b_ln_g[l], b_lnb=b_ln_b[l],
                    c_cw=c_conv_w[l], d_lb=lower_bounds[l], d_gn=d_gnorm_g[l], w_branch=w_branch[l], w_out=w_out[l])

    def trunk(x, pos, past_of):
        states = []
        for l in range(DEPTH):
            p = layer_params(l)
            mix, st = _token_mixers(_rmsnorm(x, norm1_g[l]), pos, p, past_of(l))
            x = x + mix
            x = x + _peer(_rmsnorm(x, norm2_g[l]), p_w_q[l], p_sub_keys[l], p_u[l], p_v[l])
            states.append(st)
        y = _rmsnorm(x, final_norm_g)
        stacked = [jnp.stack([s[i] for s in states], axis=0) for i in range(5)]
        return y, stacked

    def prompt_past(l):
        return None

    def sample_past(l):
        db = page_table.shape[0]
        ckv_past = cache_ckv[l, page_table].reshape(db, -1, A_KV_LORA)
        krope_past = cache_krope[l, page_table].reshape(db, -1, A_ROPE)
        return (ckv_past, krope_past, state_conv_b[l], state_conv_c[l], state_hgrn[l])

    pos_p = jnp.arange(x_prompt.shape[1])
    pos_s = PAST_LEN + jnp.arange(x_sample.shape[1])
    y_prompt, (ckv_p, krope_p, convb_p, convc_p, hgrn_p) = trunk(x_prompt, pos_p, prompt_past)
    y_sample, (ckv_s, krope_s, convb_s, convc_s, hgrn_s) = trunk(x_sample, pos_s, sample_past)
    return (y_prompt, y_sample, ckv_p, krope_p, convb_p, convc_p, hgrn_p, ckv_s, krope_s, convb_s, convc_s, hgrn_s)
```

```python
import functools
import math

import jax
import jax.numpy as jnp
from jax import lax
from jax.experimental import pallas as pl
from jax.experimental.pallas import tpu as pltpu

F32 = jnp.float32
BF16 = jnp.bfloat16
I32 = jnp.int32

D_MODEL = 1024
BATCH = 2
SEQ = 8192
DEPTH = 2
DEC_BATCH = 128
DEC_SEQ = 8
PAST_LEN = 8192
PAGE_SIZE = 128
N_PAGES = PAST_LEN // PAGE_SIZE
BRANCH_WIDTH = 256
N_BRANCH = 4
A_HEADS = 4
A_NOPE = 64
A_ROPE = 32
A_V = 64
A_Q_LORA = 256
A_KV_LORA = 128
ROPE_THETA = 10000.0
B_KERNEL = 31
C_KERNEL = 3
D_HEADS = 4
D_KEY = 64
D_VAL = 64
P_HEADS = 8
P_NKEYS = 128
P_DKEY = 256
P_TOPK = 16
P_EXPERTS = P_NKEYS * P_NKEYS
EPS = 1e-6
NEG_BIG = -1e30

NP = BATCH * SEQ
NS = DEC_BATCH * DEC_SEQ
NT = NP + NS

LANES = 128
HEAD_PAD = 128
ZA, ZB, ZD, ZG, ZC = 0, 512, 1024, 2048, 6144
Z_COLS = 6912
VMEM_LIMIT = 56 * 1024 * 1024

HI = lax.Precision.HIGHEST


def _cparams(sem):
    return pltpu.CompilerParams(dimension_semantics=sem, vmem_limit_bytes=VMEM_LIMIT)


def _dot(a, b, dims=(((1,), (0,)), ((), ())), precision=None):
    return lax.dot_general(a, b, dims, precision=precision, preferred_element_type=F32)


NT_DIMS = (((1,), (1,)), ((), ()))
TN_DIMS = (((0,), (0,)), ((), ()))


def _norm_matmul_kernel(x_ref, g_ref, w_ref, o_ref, h_ref):
    @pl.when(pl.program_id(1) == 0)
    def _():
        x = x_ref[...]
        y = x * lax.rsqrt(jnp.mean(x * x, axis=-1, keepdims=True) + EPS)
        h_ref[...] = (y * g_ref[...]).astype(BF16)

    o_ref[...] = _dot(h_ref[...], w_ref[...])


def norm_matmul(x, g, w, tm, tn):
    n, d = x.shape
    cols = w.shape[1]
    return pl.pallas_call(
        _norm_matmul_kernel,
        grid=(n // tm, cols // tn),
        in_specs=[
            pl.BlockSpec((tm, d), lambda i, j: (i, 0)),
            pl.BlockSpec((1, d), lambda i, j: (0, 0)),
            pl.BlockSpec((d, tn), lambda i, j: (0, j)),
        ],
        out_specs=pl.BlockSpec((tm, tn), lambda i, j: (i, j)),
        out_shape=jax.ShapeDtypeStruct((n, cols), F32),
        scratch_shapes=[pltpu.VMEM((tm, d), BF16)],
        compiler_params=_cparams(("parallel", "arbitrary")),
        name="norm_matmul",
    )(x, g.reshape(1, d), w)


def _matmul_kernel(x_ref, w_ref, o_ref):
    o_ref[...] = _dot(x_ref[...], w_ref[...]).astype(o_ref.dtype)


def matmul(x, w, tm, out_dtype):
    n, d = x.shape
    cols = w.shape[1]
    return pl.pallas_call(
        _matmul_kernel,
        grid=(n // tm,),
        in_specs=[pl.BlockSpec((tm, d), lambda i: (i, 0)), pl.BlockSpec((d, cols), lambda i: (0, 0))],
        out_specs=pl.BlockSpec((tm, cols), lambda i: (i, 0)),
        out_shape=jax.ShapeDtypeStruct((n, cols), out_dtype),
        compiler_params=_cparams(("parallel",)),
        name="matmul",
    )(x, w)


def _tile4(t):
    return jnp.concatenate([t, t, t, t], axis=1)


def _mla_prep_kernel(z_ref, qn_ref, kvn_ref, wqm_ref, wqs_ref, wuk_ref, wuv_ref, place_ref,
                     ccq_ref, ssq_ref, cck_ref, ssk_ref,
                     ckv_ref, kr_ref, q_ref, k_ref, v_ref):
    z = z_ref[...]
    cq = z[:, 0:A_Q_LORA]
    cqn = cq * lax.rsqrt(jnp.mean(cq * cq, axis=-1, keepdims=True) + EPS) * qn_ref[...]
    cqn = cqn.astype(BF16)
    scale = 1.0 / math.sqrt(A_NOPE + A_ROPE)
    q = _dot(cqn, wqm_ref[...]) * _tile4(ccq_ref[...]) + _dot(cqn, wqs_ref[...]) * _tile4(ssq_ref[...])
    q_ref[...] = (q * scale).astype(BF16)

    c = z[:, A_Q_LORA:A_Q_LORA + A_KV_LORA]
    ckv = c * lax.rsqrt(jnp.mean(c * c, axis=-1, keepdims=True) + EPS) * kvn_ref[...]
    ckv_ref[...] = ckv
    ckv_b = ckv.astype(BF16)

    kc = z[:, A_Q_LORA + A_KV_LORA:]
    kr = kc * cck_ref[...] + pltpu.roll(kc, LANES - A_ROPE, axis=1) * ssk_ref[...]
    kr_ref[...] = kr

    k = _dot(ckv_b, wuk_ref[...]) + _dot(kr.astype(BF16), place_ref[...])
    k_ref[...] = k.astype(BF16)
    v_ref[...] = _dot(ckv_b, wuv_ref[...]).astype(BF16)


def mla_prep(z_all, qn_g, kvn_g, wq_main, wq_swap, wuk_p, wuv_p, place, ccq, ssq, cck, ssk, tm=512):
    n = z_all.shape[0]
    hp = A_HEADS * HEAD_PAD
    row = lambda w: pl.BlockSpec((tm, w), lambda i: (i, 0))
    full = lambda a: pl.BlockSpec(a.shape, lambda i: (0,) * a.ndim)
    qn_g = qn_g.reshape(1, -1)
    kvn_g = kvn_g.reshape(1, -1)
    return pl.pallas_call(
        _mla_prep_kernel,
        grid=(n // tm,),
        in_specs=[row(512), full(qn_g), full(kvn_g), full(wq_main), full(wq_swap), full(wuk_p), full(wuv_p), full(place),
                  row(LANES), row(LANES), row(LANES), row(LANES)],
        out_specs=[row(LANES), row(LANES), row(hp), row(hp), row(hp)],
        out_shape=[jax.ShapeDtypeStruct((n, LANES), F32), jax.ShapeDtypeStruct((n, LANES), F32),
                   jax.ShapeDtypeStruct((n, hp), BF16), jax.ShapeDtypeStruct((n, hp), BF16),
                   jax.ShapeDtypeStruct((n, hp), BF16)],
        compiler_params=_cparams(("parallel",)),
        name="mla_prep",
    )(z_all, qn_g, kvn_g, wq_main, wq_swap, wuk_p, wuv_p, place, ccq, ssq, cck, ssk)


def _flash_kernel(q_ref, k_ref, v_ref, o_ref, *, tq, tk):
    i = pl.program_id(2)
    q = q_ref[...]

    def step(j, carry, masked):
        m, l, acc = carry
        kj = k_ref[pl.ds(pl.multiple_of(j * tk, tk), tk), :]
        vj = v_ref[pl.ds(pl.multiple_of(j * tk, tk), tk), :]
        s = _dot(q, kj, NT_DIMS)
        if masked:
            qpos = i * tq + lax.broadcasted_iota(I32, (tq, tk), 0)
            kpos = j * tk + lax.broadcasted_iota(I32, (tq, tk), 1)
            s = jnp.where(kpos <= qpos, s, NEG_BIG)
        m_new = jnp.maximum(m, jnp.max(s, axis=-1, keepdims=True))
        alpha = jnp.exp(m - m_new)
        p = jnp.exp(s - m_new)
        l = alpha * l + jnp.sum(p, axis=-1, keepdims=True)
        acc = alpha * acc + _dot(p.astype(BF16), vj)
        return m_new, l, acc

    init = (jnp.full((tq, 1), NEG_BIG, F32), jnp.zeros((tq, 1), F32), jnp.zeros((tq, HEAD_PAD), F32))
    n_full = (i * tq) // tk
    carry = lax.fori_loop(0, n_full, lambda j, c: step(j, c, False), init)
    for d in range(tq // tk):
        carry = step(n_full + d, carry, True)
    m, l, acc = carry
    o_ref[...] = (acc / l).astype(o_ref.dtype)


def flash_prompt(qp, kp, vp, tq=512, tk=512):
    nq = SEQ // tq
    return pl.pallas_call(
        functools.partial(_flash_kernel, tq=tq, tk=tk),
        grid=(BATCH, A_HEADS, nq),
        in_specs=[
            pl.BlockSpec((tq, HEAD_PAD), lambda b, h, i: (b * nq + i, h)),
            pl.BlockSpec((SEQ, HEAD_PAD), lambda b, h, i: (b, h)),
            pl.BlockSpec((SEQ, HEAD_PAD), lambda b, h, i: (b, h)),
        ],
        out_specs=pl.BlockSpec((tq, HEAD_PAD), lambda b, h, i: (b * nq + i, h)),
        out_shape=jax.ShapeDtypeStruct((NP, A_HEADS * HEAD_PAD), BF16),
        compiler_params=_cparams(("parallel", "parallel", "arbitrary")),
        name="flash_prompt",
    )(qp, kp, vp)


DEC_ROWS = A_HEADS * DEC_SEQ
PAGES_PER_STEP = 8


def _decode_kernel(pt_ref, q_ref, *refs):
    del pt_ref
    pg = PAGES_PER_STEP
    ckv_refs = refs[:pg]
    kr_refs = refs[pg:2 * pg]
    cnew_ref, knew_ref, wuv_ref, o_ref, m_ref, l_ref, acc_ref = refs[2 * pg:]
    g = pl.program_id(1)

    @pl.when(g == 0)
    def _():
        m_ref[...] = jnp.full(m_ref.shape, NEG_BIG, F32)
        l_ref[...] = jnp.zeros(l_ref.shape, F32)
        acc_ref[...] = jnp.zeros(acc_ref.shape, F32)

    q = q_ref[...]
    q_lat = q[:, :A_KV_LORA]
    q_rope = q[:, A_KV_LORA:A_KV_LORA + A_ROPE]

    def update(s_list, v_list):
        m_old = m_ref[...]
        m_new = m_old
        for s in s_list:
            m_new = jnp.maximum(m_new, jnp.max(s, axis=-1, keepdims=True))
        alpha = jnp.exp(m_old - m_new)
        l = alpha * l_ref[...]
        acc = alpha * acc_ref[...]
        for s, v in zip(s_list, v_list):
            p = jnp.exp(s - m_new)
            l = l + jnp.sum(p, axis=-1, keepdims=True)
            acc = acc + _dot(p.astype(BF16), v)
        m_ref[...] = m_new
        l_ref[...] = l
        acc_ref[...] = acc

    s_list, v_list = [], []
    for r in range(pg):
        kc = ckv_refs[r][...].astype(BF16)
        kr = kr_refs[r][...].astype(BF16)
        s_list.append(_dot(q_lat, kc, NT_DIMS) + _dot(q_rope, kr, NT_DIMS))
        v_list.append(kc)
    update(s_list, v_list)

    @pl.when(g == pl.num_programs(1) - 1)
    def _():
        cn = cnew_ref[...].astype(BF16)
        kn = knew_ref[...][:, :A_ROPE].astype(BF16)
        s = _dot(q_lat, cn, NT_DIMS) + _dot(q_rope, kn, NT_DIMS)
        t_q = lax.broadcasted_iota(I32, s.shape, 0) % DEC_SEQ
        t_k = lax.broadcasted_iota(I32, s.shape, 1)
        s = jnp.where(t_k <= t_q, s, NEG_BIG)
        update([s], [cn])
        o_lat = (acc_ref[...] / l_ref[...]).astype(BF16)
        for h in range(A_HEADS):
            o_ref[:, h * HEAD_PAD:(h + 1) * HEAD_PAD] = _dot(
                o_lat[h * DEC_SEQ:(h + 1) * DEC_SEQ, :], wuv_ref[:, h * HEAD_PAD:(h + 1) * HEAD_PAD]).astype(o_ref.dtype)


def mla_decode(layer, page_table, qlat, cache_ckv, cache_krope, ckv_new, kr_new, wuv_p):
    pg = PAGES_PER_STEP

    def page_spec(width, r):
        return pl.BlockSpec((None, None, PAGE_SIZE, width), lambda b, g, pt: (layer, pt[b, g * pg + r], 0, 0))

    in_specs = [pl.BlockSpec((None, DEC_ROWS, 2 * LANES), lambda b, g, pt: (b, 0, 0))]
    in_specs += [page_spec(A_KV_LORA, r) for r in range(pg)]
    in_specs += [page_spec(A_ROPE, r) for r in range(pg)]
    in_specs += [
        pl.BlockSpec((None, DEC_SEQ, LANES), lambda b, g, pt: (b, 0, 0)),
        pl.BlockSpec((None, DEC_SEQ, LANES), lambda b, g, pt: (b, 0, 0)),
        pl.BlockSpec(wuv_p.shape, lambda b, g, pt: (0, 0)),
    ]
    grid_spec = pltpu.PrefetchScalarGridSpec(
        num_scalar_prefetch=1,
        grid=(DEC_BATCH, N_PAGES // pg),
        in_specs=in_specs,
        out_specs=pl.BlockSpec((None, DEC_SEQ, A_HEADS * HEAD_PAD), lambda b, g, pt: (b, 0, 0)),
        scratch_shapes=[pltpu.VMEM((DEC_ROWS, 1), F32), pltpu.VMEM((DEC_ROWS, 1), F32), pltpu.VMEM((DEC_ROWS, A_KV_LORA), F32)],
    )
    return pl.pallas_call(
        _decode_kernel,
        grid_spec=grid_spec,
        out_shape=jax.ShapeDtypeStruct((DEC_BATCH, DEC_SEQ, A_HEADS * HEAD_PAD), BF16),
        compiler_params=_cparams(("parallel", "arbitrary")),
        name="mla_decode",
    )(page_table, qlat, *([cache_ckv] * pg), *([cache_krope] * pg), ckv_new, kr_new, wuv_p)


CTX_B = 32
CTX_C = 8


def _conv_kernel(*refs, tt, has_hist):
    if has_hist:
        zb_ref, zc_ref, hb_ref, hc_ref = refs[:4]
        refs = refs[4:]
    else:
        zb_ref, zc_ref = refs[:2]
        refs = refs[2:]
    bw_ref, bb_ref, lg_ref, lb_ref, cw_ref, o_ref, nhb_ref, nhc_ref, xb_ref, xc_ref = refs
    i = pl.program_id(1)
    hb_rows, hc_rows = B_KERNEL - 1, C_KERNEL - 1

    @pl.when(i == 0)
    def _():
        xb_ref[0:CTX_B, :] = jnp.zeros((CTX_B, BRANCH_WIDTH), F32)
        xc_ref[0:CTX_C, :] = jnp.zeros((CTX_C, BRANCH_WIDTH), F32)
        if has_hist:
            xb_ref[CTX_B - hb_rows:CTX_B, :] = hb_ref[...]
            xc_ref[CTX_C - hc_rows:CTX_C, :] = hc_ref[...]

    zb = zb_ref[...]
    xb_ref[CTX_B:CTX_B + tt, :] = zb[:, :BRANCH_WIDTH] * jax.nn.sigmoid(zb[:, BRANCH_WIDTH:])
    acc = jnp.zeros((tt, BRANCH_WIDTH), F32)
    for k in range(B_KERNEL):
        acc = acc + xb_ref[pl.ds(CTX_B - hb_rows + k, tt), :] * bw_ref[k:k + 1, :]
    y = acc + bb_ref[...]
    mu = jnp.mean(y, axis=-1, keepdims=True)
    yc = y - mu
    var = jnp.mean(yc * yc, axis=-1, keepdims=True)
    o_b = jax.nn.silu(yc * lax.rsqrt(var + EPS) * lg_ref[...] + lb_ref[...])

    zc = zc_ref[...]
    gb = zc[:, :BRANCH_WIDTH]
    xc_ref[CTX_C:CTX_C + tt, :] = zc[:, BRANCH_WIDTH:2 * BRANCH_WIDTH] * zc[:, 2 * BRANCH_WIDTH:]
    acc_c = jnp.zeros((tt, BRANCH_WIDTH), F32)
    for k in range(C_KERNEL):
        acc_c = acc_c + xc_ref[pl.ds(CTX_C - hc_rows + k, tt), :] * cw_ref[k:k + 1, :]
    o_ref[...] = jnp.concatenate([o_b, gb * acc_c], axis=1).astype(o_ref.dtype)

    new_b = xb_ref[tt:tt + CTX_B, :]
    new_c = xc_ref[tt:tt + CTX_C, :]
    xb_ref[0:CTX_B, :] = new_b
    xc_ref[0:CTX_C, :] = new_c

    @pl.when(i == pl.num_programs(1) - 1)
    def _():
        nhb_ref[...] = new_b[CTX_B - hb_rows:, :]
        nhc_ref[...] = new_c[CTX_C - hc_rows:, :]


def conv_branches(z_all, row0, nb, t, tt, hist_b, hist_c, b_cw, b_cb, ln_g, ln_b, c_cw):
    nt = t // tt
    r0 = row0 // tt
    has_hist = hist_b is not None
    w = BRANCH_WIDTH
    in_specs = [
        pl.BlockSpec((tt, 2 * w), lambda b, i: (r0 + b * nt + i, ZB // (2 * w))),
        pl.BlockSpec((tt, 3 * w), lambda b, i: (r0 + b * nt + i, ZC // (3 * w))),
    ]
    args = [z_all, z_all]
    if has_hist:
        in_specs += [pl.BlockSpec((None, B_KERNEL - 1, w), lambda b, i: (b, 0, 0)),
                     pl.BlockSpec((None, C_KERNEL - 1, w), lambda b, i: (b, 0, 0))]
        args += [hist_b, hist_c]
    params = [b_cw, b_cb.reshape(1, w), ln_g.reshape(1, w), ln_b.reshape(1, w), c_cw]
    in_specs += [pl.BlockSpec(p.shape, lambda b, i: (0, 0)) for p in params]
    return pl.pallas_call(
        functools.partial(_conv_kernel, tt=tt, has_hist=has_hist),
        grid=(nb, nt),
        in_specs=in_specs,
        out_specs=[pl.BlockSpec((tt, 2 * w), lambda b, i: (b * nt + i, 0)),
                   pl.BlockSpec((None, B_KERNEL - 1, w), lambda b, i: (b, 0, 0)),
                   pl.BlockSpec((None, C_KERNEL - 1, w), lambda b, i: (b, 0, 0))],
        out_shape=[jax.ShapeDtypeStruct((nb * t, 2 * w), BF16),
                   jax.ShapeDtypeStruct((nb, B_KERNEL - 1, w), F32),
                   jax.ShapeDtypeStruct((nb, C_KERNEL - 1, w), F32)],
        scratch_shapes=[pltpu.VMEM((CTX_B + tt, w), F32), pltpu.VMEM((CTX_C + tt, w), F32)],
        compiler_params=_cparams(("parallel", "arbitrary")),
        name="conv_branches",
    )(*args, *params)


HW = D_HEADS * D_KEY
SUB = 16


def _split_dot(a, b_bf16):
    hi = a.astype(BF16)
    lo = (a - hi.astype(F32)).astype(BF16)
    return _dot(hi, b_bf16) + _dot(lo, b_bf16)


def _hgrn_kernel(*refs, tb, has_state):
    if has_state:
        zd_ref, s0_ref = refs[:2]
        refs = refs[2:]
    else:
        zd_ref = refs[0]
        refs = refs[1:]
    lb_ref, gn_ref, o_ref, sfin_ref, st_ref, vpad_ref, kpad_ref = refs
    i = pl.program_id(1)
    c = min(SUB, tb)

    @pl.when(i == 0)
    def _():
        st_ref[...] = s0_ref[...] if has_state else jnp.zeros(st_ref.shape, F32)
        if tb < LANES:
            vpad_ref[...] = jnp.zeros(vpad_ref.shape, F32)
            kpad_ref[...] = jnp.zeros(kpad_ref.shape, F32)

    zd = zd_ref[...]
    lb = lb_ref[...]
    q = jax.nn.silu(zd[:, 0:HW])
    f = lb + (1.0 - lb) * jax.nn.sigmoid(zd[:, HW:2 * HW])
    logf = jnp.log(f)
    kk = 1.0 - f
    v = zd[:, 2 * HW:3 * HW]
    gate = jax.nn.silu(zd[:, 3 * HW:4 * HW])

    row = lax.broadcasted_iota(I32, (tb, tb), 0)
    col = lax.broadcasted_iota(I32, (tb, tb), 1)
    tril_sub = ((row // c == col // c) & (col <= row)).astype(F32)
    g = _dot(tril_sub, logf, precision=HI)

    hrow = lax.broadcasted_iota(I32, (HW, HW), 0) // D_KEY
    hcol = lax.broadcasted_iota(I32, (HW, HW), 1) // D_KEY
    same_head = hrow == hcol
    head_ones = same_head.astype(BF16)

    if tb < LANES:
        vpad_ref[0:tb, :] = v
        v_t = vpad_ref[...].T
    else:
        v_t = v.T
    v_t = v_t.astype(BF16)
    tok = lax.broadcasted_iota(I32, (tb, HW), 0)
    s_idx = lax.broadcasted_iota(I32, (c, HW), 0)

    outs = []
    for j in range(tb // c):
        r0 = j * c
        gj = g[r0:r0 + c]
        qj = q[r0:r0 + c]
        kj = kk[r0:r0 + c]
        vj = v[r0:r0 + c]
        g_last = gj[c - 1:c]
        st = st_ref[...]
        o_inter = _dot((qj * jnp.exp(gj)).astype(BF16), st.astype(BF16), NT_DIMS)

        pieces = []
        for t in range(c):
            mask = s_idx <= t
            e = jnp.exp(jnp.where(mask, gj[t:t + 1] - gj, 0.0))
            pieces.append(jnp.where(mask, qj[t:t + 1] * kj * e, 0.0))
        att = _split_dot(jnp.concatenate(pieces, axis=0), head_ones)
        o_intra = jnp.sum(att.reshape(c, c, HW) * vj[None, :, :], axis=1)
        o = o_inter + o_intra
        ms = _split_dot(o * o, head_ones) * (1.0 / D_VAL)
        outs.append(o * lax.rsqrt(ms + EPS) * gn_ref[...] * gate[r0:r0 + c])

        in_sub = (tok >= r0) & (tok < r0 + c)
        kd = jnp.where(in_sub, kk * jnp.exp(jnp.where(in_sub, g_last - g, 0.0)), 0.0)
        if tb < LANES:
            kpad_ref[0:tb, :] = kd
            kd = kpad_ref[...]
        upd = _dot(v_t, kd.astype(BF16))
        st_ref[...] = st * jnp.exp(g_last) + jnp.where(same_head, upd, 0.0)

    o_ref[...] = jnp.concatenate(outs, axis=0).astype(o_ref.dtype) if len(outs) > 1 else outs[0].astype(o_ref.dtype)

    @pl.when(i == pl.num_programs(1) - 1)
    def _():
        sfin_ref[...] = st_ref[...]


def hgrn(z_all, row0, nb, t, tb, s0_t, lb, gn):
    nt = t // tb
    r0 = row0 // tb
    has_state = s0_t is not None
    in_specs = [pl.BlockSpec((tb, 4 * HW), lambda b, i: (r0 + b * nt + i, ZD // (4 * HW)))]
    args = [z_all]
    if has_state:
        in_specs.append(pl.BlockSpec((None, HW, HW), lambda b, i: (b, 0, 0)))
        args.append(s0_t)
    in_specs += [pl.BlockSpec((1, HW), lambda b, i: (0, 0)), pl.BlockSpec((1, HW), lambda b, i: (0, 0))]
    args += [lb.reshape(1, HW), jnp.tile(gn, D_HEADS).reshape(1, HW)]
    pad_rows = LANES if tb < LANES else 8
    return pl.pallas_call(
        functools.partial(_hgrn_kernel, tb=tb, has_state=has_state),
        grid=(nb, nt),
        in_specs=in_specs,
        out_specs=[pl.BlockSpec((tb, HW), lambda b, i: (b * nt + i, 0)),
                   pl.BlockSpec((None, HW, HW), lambda b, i: (b, 0, 0))],
        out_shape=[jax.ShapeDtypeStruct((nb * t, HW), BF16), jax.ShapeDtypeStruct((nb, HW, HW), F32)],
        scratch_shapes=[pltpu.VMEM((HW, HW), F32), pltpu.VMEM((pad_rows, HW), F32), pltpu.VMEM((pad_rows, HW), F32)],
        compiler_params=_cparams(("parallel", "arbitrary")),
        name="hgrn",
    )(*args)


def _merge_kernel(oa_ref, obc_ref, od_ref, ga_ref, gb_ref, gc_ref, gd_ref, wa_ref, wb_ref, wc_ref, wd_ref, wo_ref, x_ref, o_ref):
    obc = obc_ref[...]
    merged = jax.nn.sigmoid(ga_ref[...]) * _dot(oa_ref[...], wa_ref[...])
    merged += jax.nn.sigmoid(gb_ref[...]) * _dot(obc[:, :BRANCH_WIDTH], wb_ref[...])
    merged += jax.nn.sigmoid(gc_ref[...]) * _dot(obc[:, BRANCH_WIDTH:], wc_ref[...])
    merged += jax.nn.sigmoid(gd_ref[...]) * _dot(od_ref[...], wd_ref[...])
    o_ref[...] = x_ref[...] + _dot(merged.astype(BF16), wo_ref[...])


def merge(o_a, o_bc, o_d, z_all, wa_p, wb, wc, wd, w_out, x, tm=512):
    n = x.shape[0]
    row = lambda w: pl.BlockSpec((tm, w), lambda i: (i, 0))
    full = lambda a: pl.BlockSpec(a.shape, lambda i: (0, 0))
    gate = lambda k: pl.BlockSpec((tm, D_MODEL), lambda i: (i, ZG // D_MODEL + k))
    return pl.pallas_call(
        _merge_kernel,
        grid=(n // tm,),
        in_specs=[row(o_a.shape[1]), row(o_bc.shape[1]), row(o_d.shape[1]), gate(0), gate(1), gate(2), gate(3),
                  full(wa_p), full(wb), full(wc), full(wd), full(w_out), row(D_MODEL)],
        out_specs=row(D_MODEL),
        out_shape=jax.ShapeDtypeStruct((n, D_MODEL), F32),
        compiler_params=_cparams(("parallel",)),
        name="merge",
    )(o_a, o_bc, o_d, z_all, z_all, z_all, z_all, wa_p, wb, wc, wd, w_out, x)


HP = P_HEADS * 2
SLOTS = P_HEADS * P_TOPK
SUBK = P_DKEY // 2


def _take_max(s, iota, n):
    m = jnp.max(s, axis=0, keepdims=True)
    idx = jnp.min(jnp.where(s == m, iota, n), axis=0, keepdims=True)
    return m, idx, jnp.where(iota == idx, -jnp.inf, s)


def _route_kernel(x_ref, g_ref, wq_ref, keys_ref, h_ref, ii_ref, jj_ref, gw_ref, q_s, sv_s, si_s, oi_s, oj_s, og_s, *, tm):
    x = x_ref[...]
    h = (x * lax.rsqrt(jnp.mean(x * x, axis=-1, keepdims=True) + EPS) * g_ref[...]).astype(BF16)
    h_ref[...] = h
    q_s[...] = _dot(h, wq_ref[...]).astype(BF16)

    iota_k = lax.broadcasted_iota(I32, (P_NKEYS, tm), 0)

    def stage1(hp, carry):
        q = q_s[:, pl.ds(pl.multiple_of(hp * SUBK, SUBK), SUBK)]
        s = _dot(keys_ref[hp], q, NT_DIMS)
        vals, idxs = [], []
        for _ in range(P_TOPK):
            m, idx, s = _take_max(s, iota_k, P_NKEYS)
            vals.append(m)
            idxs.append(idx)
        sv_s[hp] = jnp.concatenate(vals, axis=0)
        si_s[hp] = jnp.concatenate(idxs, axis=0)
        return carry

    lax.fori_loop(0, HP, stage1, 0)

    nc = P_TOPK * P_TOPK
    iota_c = lax.broadcasted_iota(I32, (nc, tm), 0)
    iota_t = lax.broadcasted_iota(I32, (P_TOPK, tm), 0)

    def stage2(hd, carry):
        sv1, sv2 = sv_s[2 * hd], sv_s[2 * hd + 1]
        si1, si2 = si_s[2 * hd], si_s[2 * hd + 1]
        cand = jnp.concatenate([sv1[a:a + 1] + sv2 for a in range(P_TOPK)], axis=0)
        fv, ei, ej = [], [], []
        for _ in range(P_TOPK):
            m, idx, cand = _take_max(cand, iota_c, nc)
            fv.append(m)
            ei.append(jnp.sum(jnp.where(iota_t == idx // P_TOPK, si1, 0), axis=0, keepdims=True))
            ej.append(jnp.sum(jnp.where(iota_t == idx % P_TOPK, si2, 0), axis=0, keepdims=True))
        fv = jnp.concatenate(fv, axis=0)
        e = jnp.exp(fv - fv[0:1])
        rows = pl.ds(pl.multiple_of(hd * P_TOPK, P_TOPK), P_TOPK)
        og_s[rows, :] = e / jnp.sum(e, axis=0, keepdims=True)
        oi_s[rows, :] = jnp.concatenate(ei, axis=0)
        oj_s[rows, :] = jnp.concatenate(ej, axis=0)
        return carry

    lax.fori_loop(0, P_HEADS, stage2, 0)
    ii_ref[...] = oi_s[...].T
    jj_ref[...] = oj_s[...].T
    gw_ref[...] = og_s[...].T


def peer_route(x, g, wq, keys, tm=256):
    n = x.shape[0]
    row = lambda w: pl.BlockSpec((tm, w), lambda i: (i, 0))
    return pl.pallas_call(
        functools.partial(_route_kernel, tm=tm),
        grid=(n // tm,),
        in_specs=[row(D_MODEL), pl.BlockSpec((1, D_MODEL), lambda i: (0, 0)),
                  pl.BlockSpec(wq.shape, lambda i: (0, 0)), pl.BlockSpec(keys.shape, lambda i: (0, 0, 0))],
        out_specs=[row(D_MODEL), row(SLOTS), row(SLOTS), row(SLOTS)],
        out_shape=[jax.ShapeDtypeStruct((n, D_MODEL), BF16), jax.ShapeDtypeStruct((n, SLOTS), I32),
                   jax.ShapeDtypeStruct((n, SLOTS), I32), jax.ShapeDtypeStruct((n, SLOTS), F32)],
        scratch_shapes=[pltpu.VMEM((tm, HP * SUBK), BF16), pltpu.VMEM((HP, P_TOPK, tm), F32), pltpu.VMEM((HP, P_TOPK, tm), I32),
                        pltpu.VMEM((SLOTS, tm), I32), pltpu.VMEM((SLOTS, tm), I32), pltpu.VMEM((SLOTS, tm), F32)],
        compiler_params=_cparams(("parallel",)),
        name="peer_route",
    )(x, g.reshape(1, D_MODEL), wq, keys)


CHUNK_I = 16
CHUNK_E = CHUNK_I * P_NKEYS
N_CHUNKS = P_NKEYS // CHUNK_I


def _peer_kernel(h_ref, ii_ref, jj_ref, gw_ref, u_ref, v_ref, x_ref, o_ref, w_s, acc_s, *, tm):
    c = pl.program_id(1)

    @pl.when(c == 0)
    def _():
        acc_s[...] = jnp.zeros(acc_s.shape, F32)
        iota = lax.broadcasted_iota(I32, (P_NKEYS, SLOTS), 0)

        def build(t, carry):
            irow = ii_ref[pl.ds(t, 1), :]
            jrow = jj_ref[pl.ds(t, 1), :]
            grow = gw_ref[pl.ds(t, 1), :]
            p_t = jnp.where(iota == irow, grow, 0.0).astype(BF16)
            q_t = jnp.where(iota == jrow, 1.0, 0.0).astype(BF16)
            w = _dot(p_t, q_t, NT_DIMS)
            w_s[:, pl.ds(t, 1)] = w.reshape(N_CHUNKS, 1, CHUNK_I, P_NKEYS)
            return carry

        lax.fori_loop(0, tm, build, 0)

    a = _dot(h_ref[...], u_ref[...], NT_DIMS)
    act = 0.5 * a * (1.0 + lax.erf(a * (1.0 / math.sqrt(2.0))))
    wd = jnp.concatenate([w_s[c, :, r, :] for r in range(CHUNK_I)], axis=1)
    acc_s[...] += _dot((act * wd).astype(BF16), v_ref[...])

    @pl.when(c == N_CHUNKS - 1)
    def _():
        o_ref[...] = x_ref[...] + acc_s[...]


def peer_experts(h2, ii, jj, gw, u_tab, v_tab, x, tm=256):
    n = x.shape[0]
    row = lambda w: pl.BlockSpec((tm, w), lambda i, c: (i, 0))
    return pl.pallas_call(
        functools.partial(_peer_kernel, tm=tm),
        grid=(n // tm, N_CHUNKS),
        in_specs=[row(D_MODEL), row(SLOTS), row(SLOTS), row(SLOTS),
                  pl.BlockSpec((CHUNK_E, D_MODEL), lambda i, c: (c, 0)),
                  pl.BlockSpec((CHUNK_E, D_MODEL), lambda i, c: (c, 0)),
                  row(D_MODEL)],
        out_specs=row(D_MODEL),
        out_shape=jax.ShapeDtypeStruct((n, D_MODEL), F32),
        scratch_shapes=[pltpu.VMEM((N_CHUNKS, tm, CHUNK_I, P_NKEYS), F32), pltpu.VMEM((tm, D_MODEL), F32)],
        compiler_params=_cparams(("parallel", "arbitrary")),
        name="peer_experts",
    )(h2, ii, jj, gw, u_tab, v_tab, x)


def _rmsnorm_kernel(x_ref, g_ref, o_ref):
    x = x_ref[...]
    o_ref[...] = x * lax.rsqrt(jnp.mean(x * x, axis=-1, keepdims=True) + EPS) * g_ref[...]


def rmsnorm(x, g, tm=512):
    n, d = x.shape
    return pl.pallas_call(
        _rmsnorm_kernel,
        grid=(n // tm,),
        in_specs=[pl.BlockSpec((tm, d), lambda i: (i, 0)), pl.BlockSpec((1, d), lambda i: (0, 0))],
        out_specs=pl.BlockSpec((tm, d), lambda i: (i, 0)),
        out_shape=jax.ShapeDtypeStruct((n, d), F32),
        compiler_params=_cparams(("parallel",)),
        name="rmsnorm",
    )(x, g.reshape(1, d))


def _rope_tables():
    half = A_ROPE // 2
    inv = ROPE_THETA ** (-jnp.arange(half, dtype=F32) / half)
    pos = jnp.concatenate([jnp.tile(jnp.arange(SEQ), BATCH), jnp.tile(PAST_LEN + jnp.arange(DEC_SEQ), DEC_BATCH)])
    ang = pos.astype(F32)[:, None] * inv[None, :]
    cos, sin = jnp.cos(ang), jnp.sin(ang)
    zeros = lambda w: jnp.zeros((NT, w), F32)
    ccq = jnp.concatenate([jnp.ones((NT, A_NOPE), F32), cos, cos, zeros(HEAD_PAD - A_NOPE - A_ROPE)], axis=1)
    ssq = jnp.concatenate([zeros(A_NOPE), -sin, sin, zeros(HEAD_PAD - A_NOPE - A_ROPE)], axis=1)
    cck = jnp.concatenate([cos, cos, zeros(LANES - A_ROPE)], axis=1)
    ssk = jnp.concatenate([-sin, sin, zeros(LANES - A_ROPE)], axis=1)
    return ccq, ssq, cck, ssk


def _swap_halves(w):
    half = w.shape[-1] // 2
    return jnp.concatenate([w[..., half:], w[..., :half]], axis=-1)


def _layer_weights(l, w_in, a_w_uq, a_w_uk, a_w_uv, w_branch, w_out, p_w_q, p_sub_keys, p_u, p_v):
    w = w_in[l]
    o = 0
    parts = []
    for size in (A_Q_LORA, A_KV_LORA, A_ROPE, 2 * BRANCH_WIDTH, 3 * BRANCH_WIDTH, 4 * HW, N_BRANCH * D_MODEL):
        parts.append(w[:, o:o + size])
        o += size
    cq, ckv, kr, b_in, c_in, d_in, gate = parts
    pad = jnp.zeros((D_MODEL, ZB - A_Q_LORA - A_KV_LORA - 2 * A_ROPE), F32)
    w_in_p = jnp.concatenate([cq, ckv, kr, _swap_halves(kr), pad, b_in, d_in, gate, c_in], axis=1).astype(BF16)

    wq = a_w_uq[l].reshape(A_Q_LORA, A_HEADS, A_NOPE + A_ROPE)
    nope, rope = wq[..., :A_NOPE], wq[..., A_NOPE:]
    tail = jnp.zeros((A_Q_LORA, A_HEADS, HEAD_PAD - A_NOPE - A_ROPE), F32)
    wq_main = jnp.concatenate([nope, rope, tail], axis=-1).reshape(A_Q_LORA, -1).astype(BF16)
    wq_swap = jnp.concatenate([jnp.zeros_like(nope), _swap_halves(rope), tail], axis=-1).reshape(A_Q_LORA, -1).astype(BF16)

    w_uk, w_uv = a_w_uk[l], a_w_uv[l]
    head_tail = jnp.zeros((A_KV_LORA, A_HEADS, HEAD_PAD - A_NOPE), F32)
    wuk_p = jnp.concatenate([w_uk, head_tail], axis=-1).reshape(A_KV_LORA, -1).astype(BF16)
    wuv_p = jnp.concatenate([w_uv, head_tail], axis=-1).reshape(A_KV_LORA, -1).astype(BF16)

    r = jnp.arange(LANES)[:, None]
    col = jnp.arange(A_HEADS * HEAD_PAD)[None, :]
    place = ((col % HEAD_PAD == A_NOPE + r) & (r < A_ROPE)).astype(BF16)

    blk = jnp.zeros((A_HEADS, HEAD_PAD, 2 * LANES), F32)
    blk = blk.at[:, :A_NOPE, :A_KV_LORA].set(jnp.transpose(w_uk, (1, 2, 0)))
    blk = blk.at[:, A_NOPE:A_NOPE + A_ROPE, A_KV_LORA:A_KV_LORA + A_ROPE].set(jnp.eye(A_ROPE, dtype=F32))
    eye_h = jnp.eye(A_HEADS, dtype=F32)
    wabs = (blk[:, :, None, :] * eye_h[:, None, :, None]).reshape(A_HEADS * HEAD_PAD, A_HEADS * 2 * LANES).astype(BF16)

    wb = w_branch[l]
    wa_p = jnp.concatenate([wb[0].reshape(A_HEADS, A_V, D_MODEL), jnp.zeros((A_HEADS, HEAD_PAD - A_V, D_MODEL), F32)],
                           axis=1).reshape(A_HEADS * HEAD_PAD, D_MODEL).astype(BF16)
    return dict(w_in_p=w_in_p, wq_main=wq_main, wq_swap=wq_swap, wuk_p=wuk_p, wuv_p=wuv_p, place=place, wabs=wabs,
                wa_p=wa_p, wb=wb[1].astype(BF16), wc=wb[2].astype(BF16), wd=wb[3].astype(BF16), w_out=w_out[l].astype(BF16),
                wq=p_w_q[l].astype(BF16), keys=p_sub_keys[l].reshape(HP, P_NKEYS, SUBK).astype(BF16),
                u=p_u[l].astype(BF16), v=p_v[l].astype(BF16))


def _state_to_kernel(s):
    t = jnp.transpose(s, (0, 1, 3, 2))
    eye_h = jnp.eye(D_HEADS, dtype=F32)
    return (t[:, :, :, None, :] * eye_h[None, :, None, :, None]).reshape(s.shape[0], HW, HW)


def _state_from_kernel(s_t):
    s5 = s_t.reshape(s_t.shape[0], D_HEADS, D_VAL, D_HEADS, D_KEY)
    diag = jnp.stack([s5[:, h, :, h, :] for h in range(D_HEADS)], axis=1)
    return jnp.transpose(diag, (0, 1, 3, 2))


def kernel(x_prompt, x_sample, cache_ckv, cache_krope, page_table, state_conv_b, state_conv_c, state_hgrn, norm1_g, w_in, a_q_norm_g, a_w_uq, a_kv_norm_g, a_w_uk, a_w_uv, b_conv_w, b_conv_b, b_ln_g, b_ln_b, c_conv_w, d_lower_bound, d_gnorm_g, w_branch, w_out, norm2_g, p_w_q, p_sub_keys, p_u, p_v, final_norm_g):
    lb_soft = jax.nn.softmax(d_lower_bound.astype(F32), axis=0)
    lower_bounds = jnp.cumsum(lb_soft, axis=0) - lb_soft[0:1]
    ccq, ssq, cck, ssk = _rope_tables()
    x = jnp.concatenate([x_prompt.reshape(NP, D_MODEL), x_sample.reshape(NS, D_MODEL)], axis=0)

    states_p, states_s = [], []
    for l in range(DEPTH):
        w = _layer_weights(l, w_in, a_w_uq, a_w_uk, a_w_uv, w_branch, w_out, p_w_q, p_sub_keys, p_u, p_v)
        z = norm_matmul(x, norm1_g[l], w["w_in_p"], 512, 1152)

        ckv, kr, qp, kp, vp = mla_prep(z, a_q_norm_g[l], a_kv_norm_g[l], w["wq_main"], w["wq_swap"], w["wuk_p"], w["wuv_p"],
                                       w["place"], ccq, ssq, cck, ssk)
        oa_p = flash_prompt(qp, kp, vp)
        qlat = matmul(qp[NP:], w["wabs"], 512, BF16)
        qlat = jnp.transpose(qlat.reshape(DEC_BATCH, DEC_SEQ, A_HEADS, 2 * LANES), (0, 2, 1, 3)).reshape(DEC_BATCH, DEC_ROWS, 2 * LANES)
        oa_s = mla_decode(l, page_table, qlat, cache_ckv, cache_krope, ckv[NP:].reshape(DEC_BATCH, DEC_SEQ, LANES),
                          kr[NP:].reshape(DEC_BATCH, DEC_SEQ, LANES), w["wuv_p"])
        o_a = jnp.concatenate([oa_p, oa_s.reshape(NS, -1)], axis=0)

        conv_w = (b_conv_w[l], b_conv_b[l], b_ln_g[l], b_ln_b[l], c_conv_w[l])
        obc_p, hb_p, hc_p = conv_branches(z, 0, BATCH, SEQ, 512, None, None, *conv_w)
        obc_s, hb_s, hc_s = conv_branches(z, NP, DEC_BATCH, DEC_SEQ, DEC_SEQ, state_conv_b[l], state_conv_c[l], *conv_w)
        o_bc = jnp.concatenate([obc_p, obc_s], axis=0)

        od_p, st_p = hgrn(z, 0, BATCH, SEQ, LANES, None, lower_bounds[l], d_gnorm_g[l])
        od_s, st_s = hgrn(z, NP, DEC_BATCH, DEC_SEQ, DEC_SEQ, _state_to_kernel(state_hgrn[l]), lower_bounds[l], d_gnorm_g[l])
        o_d = jnp.concatenate([od_p, od_s], axis=0)

        x1 = merge(o_a, o_bc, o_d, z, w["wa_p"], w["wb"], w["wc"], w["wd"], w["w_out"], x)
        h2, ii, jj, gw = peer_route(x1, norm2_g[l], w["wq"], w["keys"])
        x = peer_experts(h2, ii, jj, gw, w["u"], w["v"], x1)

        states_p.append((ckv[:NP].reshape(BATCH, SEQ, A_KV_LORA), kr[:NP, :A_ROPE].reshape(BATCH, SEQ, A_ROPE),
                         hb_p, hc_p, _state_from_kernel(st_p)))
        states_s.append((ckv[NP:].reshape(DEC_BATCH, DEC_SEQ, A_KV_LORA), kr[NP:, :A_ROPE].reshape(DEC_BATCH, DEC_SEQ, A_ROPE),
                         hb_s, hc_s, _state_from_kernel(st_s)))

    y = rmsnorm(x, final_norm_g)
    stack = lambda states: [jnp.stack([s[i] for s in states], axis=0) for i in range(5)]
    return (y[:NP].reshape(BATCH, SEQ, D_MODEL), y[NP:].reshape(DEC_BATCH, DEC_SEQ, D_MODEL), *stack(states_p), *stack(states_s))
```

```python
import functools
import math

import jax
import jax.numpy as jnp
from jax import lax
from jax.experimental import pallas as pl
from jax.experimental.pallas import tpu as pltpu

F32 = jnp.float32
BF16 = jnp.bfloat16
I32 = jnp.int32

D_MODEL = 1024
BATCH = 2
SEQ = 8192
DEPTH = 2
DEC_BATCH = 128
DEC_SEQ = 8
PAST_LEN = 8192
PAGE_SIZE = 128
N_PAGES = PAST_LEN // PAGE_SIZE
BRANCH_WIDTH = 256
N_BRANCH = 4
A_HEADS = 4
A_NOPE = 64
A_ROPE = 32
A_V = 64
A_Q_LORA = 256
A_KV_LORA = 128
ROPE_THETA = 10000.0
B_KERNEL = 31
C_KERNEL = 3
D_HEADS = 4
D_KEY = 64
D_VAL = 64
P_HEADS = 8
P_NKEYS = 128
P_DKEY = 256
P_TOPK = 16
P_EXPERTS = P_NKEYS * P_NKEYS
EPS = 1e-6
NEG_BIG = -1e30

NP = BATCH * SEQ
NS = DEC_BATCH * DEC_SEQ
NT = NP + NS

LANES = 128
SUBLANES = 8
HEAD_PAD = 128
ZA, ZB, ZD, ZG, ZC = 0, 512, 1024, 2048, 6144
Z_COLS = 6912
VMEM_LIMIT = 56 * 1024 * 1024

HI = lax.Precision.HIGHEST


def _cparams(sem):
    return pltpu.CompilerParams(dimension_semantics=sem, vmem_limit_bytes=VMEM_LIMIT)


def _dot(a, b, dims=(((1,), (0,)), ((), ())), precision=None):
    return lax.dot_general(a, b, dims, precision=precision, preferred_element_type=F32)


NT_DIMS = (((1,), (1,)), ((), ()))
TN_DIMS = (((0,), (0,)), ((), ()))


def _norm_matmul_kernel(x_ref, g_ref, w_ref, o_ref, h_ref):
    @pl.when(pl.program_id(1) == 0)
    def _():
        x = x_ref[...]
        y = x * lax.rsqrt(jnp.mean(x * x, axis=-1, keepdims=True) + EPS)
        h_ref[...] = (y * g_ref[...]).astype(BF16)

    o_ref[...] = _dot(h_ref[...], w_ref[...])


def norm_matmul(x, g, w, tm, tn):
    n, d = x.shape
    cols = w.shape[1]
    return pl.pallas_call(
        _norm_matmul_kernel,
        grid=(n // tm, cols // tn),
        in_specs=[
            pl.BlockSpec((tm, d), lambda i, j: (i, 0)),
            pl.BlockSpec((1, d), lambda i, j: (0, 0)),
            pl.BlockSpec((d, tn), lambda i, j: (0, j)),
        ],
        out_specs=pl.BlockSpec((tm, tn), lambda i, j: (i, j)),
        out_shape=jax.ShapeDtypeStruct((n, cols), F32),
        scratch_shapes=[pltpu.VMEM((tm, d), BF16)],
        compiler_params=_cparams(("parallel", "arbitrary")),
        name="norm_matmul",
    )(x, g.reshape(1, d), w)


def _matmul_kernel(x_ref, w_ref, o_ref):
    o_ref[...] = _dot(x_ref[...], w_ref[...]).astype(o_ref.dtype)


def matmul(x, w, tm, out_dtype):
    n, d = x.shape
    cols = w.shape[1]
    return pl.pallas_call(
        _matmul_kernel,
        grid=(n // tm,),
        in_specs=[pl.BlockSpec((tm, d), lambda i: (i, 0)), pl.BlockSpec((d, cols), lambda i: (0, 0))],
        out_specs=pl.BlockSpec((tm, cols), lambda i: (i, 0)),
        out_shape=jax.ShapeDtypeStruct((n, cols), out_dtype),
        compiler_params=_cparams(("parallel",)),
        name="matmul",
    )(x, w)


def _tile4(t):
    return jnp.concatenate([t, t, t, t], axis=1)


def _mla_prep_kernel(z_ref, qn_ref, kvn_ref, wqm_ref, wqs_ref, wuk_ref, wuv_ref, place_ref,
                     ccq_ref, ssq_ref, cck_ref, ssk_ref,
                     ckv_ref, kr_ref, q_ref, k_ref, v_ref):
    z = z_ref[...]
    cq = z[:, 0:A_Q_LORA]
    cqn = cq * lax.rsqrt(jnp.mean(cq * cq, axis=-1, keepdims=True) + EPS) * qn_ref[...]
    cqn = cqn.astype(BF16)
    scale = 1.0 / math.sqrt(A_NOPE + A_ROPE)
    q = _dot(cqn, wqm_ref[...]) * _tile4(ccq_ref[...]) + _dot(cqn, wqs_ref[...]) * _tile4(ssq_ref[...])
    q_ref[...] = (q * scale).astype(BF16)

    c = z[:, A_Q_LORA:A_Q_LORA + A_KV_LORA]
    ckv = c * lax.rsqrt(jnp.mean(c * c, axis=-1, keepdims=True) + EPS) * kvn_ref[...]
    ckv_ref[...] = ckv
    ckv_b = ckv.astype(BF16)

    kc = z[:, A_Q_LORA + A_KV_LORA:]
    kr = kc * cck_ref[...] + pltpu.roll(kc, LANES - A_ROPE, axis=1) * ssk_ref[...]
    kr_ref[...] = kr

    k = _dot(ckv_b, wuk_ref[...]) + _dot(kr.astype(BF16), place_ref[...])
    k_ref[...] = k.astype(BF16)
    v_ref[...] = _dot(ckv_b, wuv_ref[...]).astype(BF16)


def mla_prep(z_all, qn_g, kvn_g, wq_main, wq_swap, wuk_p, wuv_p, place, ccq, ssq, cck, ssk, tm=512):
    n = z_all.shape[0]
    hp = A_HEADS * HEAD_PAD
    row = lambda w: pl.BlockSpec((tm, w), lambda i: (i, 0))
    full = lambda a: pl.BlockSpec(a.shape, lambda i: (0,) * a.ndim)
    qn_g = qn_g.reshape(1, -1)
    kvn_g = kvn_g.reshape(1, -1)
    return pl.pallas_call(
        _mla_prep_kernel,
        grid=(n // tm,),
        in_specs=[row(512), full(qn_g), full(kvn_g), full(wq_main), full(wq_swap), full(wuk_p), full(wuv_p), full(place),
                  row(LANES), row(LANES), row(LANES), row(LANES)],
        out_specs=[row(LANES), row(LANES), row(hp), row(hp), row(hp)],
        out_shape=[jax.ShapeDtypeStruct((n, LANES), F32), jax.ShapeDtypeStruct((n, LANES), F32),
                   jax.ShapeDtypeStruct((n, hp), BF16), jax.ShapeDtypeStruct((n, hp), BF16),
                   jax.ShapeDtypeStruct((n, hp), BF16)],
        compiler_params=_cparams(("parallel",)),
        name="mla_prep",
    )(z_all, qn_g, kvn_g, wq_main, wq_swap, wuk_p, wuv_p, place, ccq, ssq, cck, ssk)


def _flash_kernel(q_ref, k_ref, v_ref, o_ref, *, tq, tk):
    i = pl.program_id(2)
    q = q_ref[...]

    def step(j, carry, masked):
        m, l, acc = carry
        kj = k_ref[pl.ds(pl.multiple_of(j * tk, tk), tk), :]
        vj = v_ref[pl.ds(pl.multiple_of(j * tk, tk), tk), :]
        s = _dot(q, kj, NT_DIMS)
        if masked:
            qpos = i * tq + lax.broadcasted_iota(I32, (tq, tk), 0)
            kpos = j * tk + lax.broadcasted_iota(I32, (tq, tk), 1)
            s = jnp.where(kpos <= qpos, s, NEG_BIG)
        m_new = jnp.maximum(m, jnp.max(s, axis=-1, keepdims=True))
        alpha = jnp.exp(m - m_new)
        p = jnp.exp(s - m_new)
        l = alpha * l + jnp.sum(p, axis=-1, keepdims=True)
        acc = alpha * acc + _dot(p.astype(BF16), vj)
        return m_new, l, acc

    init = (jnp.full((tq, 1), NEG_BIG, F32), jnp.zeros((tq, 1), F32), jnp.zeros((tq, HEAD_PAD), F32))
    n_full = (i * tq) // tk
    carry = lax.fori_loop(0, n_full, lambda j, c: step(j, c, False), init)
    for d in range(tq // tk):
        carry = step(n_full + d, carry, True)
    m, l, acc = carry
    o_ref[...] = (acc / l).astype(o_ref.dtype)


def flash_prompt(qp, kp, vp, tq=512, tk=512):
    nq = SEQ // tq
    return pl.pallas_call(
        functools.partial(_flash_kernel, tq=tq, tk=tk),
        grid=(BATCH, A_HEADS, nq),
        in_specs=[
            pl.BlockSpec((tq, HEAD_PAD), lambda b, h, i: (b * nq + i, h)),
            pl.BlockSpec((SEQ, HEAD_PAD), lambda b, h, i: (b, h)),
            pl.BlockSpec((SEQ, HEAD_PAD), lambda b, h, i: (b, h)),
        ],
        out_specs=pl.BlockSpec((tq, HEAD_PAD), lambda b, h, i: (b * nq + i, h)),
        out_shape=jax.ShapeDtypeStruct((NP, A_HEADS * HEAD_PAD), BF16),
        compiler_params=_cparams(("parallel", "parallel", "arbitrary")),
        name="flash_prompt",
    )(qp, kp, vp)


DEC_ROWS = A_HEADS * DEC_SEQ
PAGES_PER_STEP = 32


def _decode_kernel(pt_ref, q_ref, *refs):
    del pt_ref
    pg = PAGES_PER_STEP
    ckv_refs = refs[:pg]
    kr_refs = refs[pg:2 * pg]
    cnew_ref, knew_ref, wuv_ref, o_ref, m_ref, l_ref, acc_ref = refs[2 * pg:]
    g = pl.program_id(1)

    @pl.when(g == 0)
    def _():
        m_ref[...] = jnp.full(m_ref.shape, NEG_BIG, F32)
        l_ref[...] = jnp.zeros(l_ref.shape, F32)
        acc_ref[...] = jnp.zeros(acc_ref.shape, F32)

    q = q_ref[...]
    q_lat = q[:, :A_KV_LORA]
    q_rope = q[:, A_KV_LORA:A_KV_LORA + A_ROPE]

    def update(s_list, v_list):
        m_old = m_ref[...]
        m_new = m_old
        for s in s_list:
            m_new = jnp.maximum(m_new, jnp.max(s, axis=-1, keepdims=True))
        alpha = jnp.exp(m_old - m_new)
        l = alpha * l_ref[...]
        acc = alpha * acc_ref[...]
        for s, v in zip(s_list, v_list):
            p = jnp.exp(s - m_new)
            l = l + jnp.sum(p, axis=-1, keepdims=True)
            acc = acc + _dot(p.astype(BF16), v)
        m_ref[...] = m_new
        l_ref[...] = l
        acc_ref[...] = acc

    s_list, v_list = [], []
    for r in range(pg):
        kc = ckv_refs[r][...].astype(BF16)
        kr_t = kr_refs[r][...].astype(BF16)
        s_list.append(_dot(q_lat, kc, NT_DIMS) + _dot(q_rope, kr_t))
        v_list.append(kc)
    update(s_list, v_list)

    @pl.when(g == pl.num_programs(1) - 1)
    def _():
        cn = cnew_ref[...].astype(BF16)
        kn = knew_ref[...][:, :A_ROPE].astype(BF16)
        s = _dot(q_lat, cn, NT_DIMS) + _dot(q_rope, kn, NT_DIMS)
        t_q = lax.broadcasted_iota(I32, s.shape, 0) % DEC_SEQ
        t_k = lax.broadcasted_iota(I32, s.shape, 1)
        s = jnp.where(t_k <= t_q, s, NEG_BIG)
        update([s], [cn])
        o_lat = (acc_ref[...] / l_ref[...]).astype(BF16)
        for h in range(A_HEADS):
            o_ref[:, h * HEAD_PAD:(h + 1) * HEAD_PAD] = _dot(
                o_lat[h * DEC_SEQ:(h + 1) * DEC_SEQ, :], wuv_ref[:, h * HEAD_PAD:(h + 1) * HEAD_PAD]).astype(o_ref.dtype)


def mla_decode(layer, page_table, qlat, cache_ckv, cache_krope_t, ckv_new, kr_new, wuv_p):
    pg = PAGES_PER_STEP

    def page_spec(rows, cols, r):
        return pl.BlockSpec((None, None, rows, cols), lambda b, g, pt: (layer, pt[b, g * pg + r], 0, 0))

    in_specs = [pl.BlockSpec((None, DEC_ROWS, 2 * LANES), lambda b, g, pt: (b, 0, 0))]
    in_specs += [page_spec(PAGE_SIZE, A_KV_LORA, r) for r in range(pg)]
    in_specs += [page_spec(A_ROPE, PAGE_SIZE, r) for r in range(pg)]
    in_specs += [
        pl.BlockSpec((None, DEC_SEQ, LANES), lambda b, g, pt: (b, 0, 0)),
        pl.BlockSpec((None, DEC_SEQ, LANES), lambda b, g, pt: (b, 0, 0)),
        pl.BlockSpec(wuv_p.shape, lambda b, g, pt: (0, 0)),
    ]
    grid_spec = pltpu.PrefetchScalarGridSpec(
        num_scalar_prefetch=1,
        grid=(DEC_BATCH, N_PAGES // pg),
        in_specs=in_specs,
        out_specs=pl.BlockSpec((None, DEC_SEQ, A_HEADS * HEAD_PAD), lambda b, g, pt: (b, 0, 0)),
        scratch_shapes=[pltpu.VMEM((DEC_ROWS, 1), F32), pltpu.VMEM((DEC_ROWS, 1), F32), pltpu.VMEM((DEC_ROWS, A_KV_LORA), F32)],
    )
    return pl.pallas_call(
        _decode_kernel,
        grid_spec=grid_spec,
        out_shape=jax.ShapeDtypeStruct((DEC_BATCH, DEC_SEQ, A_HEADS * HEAD_PAD), BF16),
        compiler_params=_cparams(("parallel", "arbitrary")),
        name="mla_decode",
    )(page_table, qlat, *([cache_ckv] * pg), *([cache_krope_t] * pg), ckv_new, kr_new, wuv_p)


CTX_B = 32
CTX_C = 8


def _conv_kernel(*refs, tt, has_hist):
    if has_hist:
        zb_ref, zc_ref, hb_ref, hc_ref = refs[:4]
        refs = refs[4:]
    else:
        zb_ref, zc_ref = refs[:2]
        refs = refs[2:]
    bw_ref, bb_ref, lg_ref, lb_ref, cw_ref, o_ref, nhb_ref, nhc_ref, xb_ref, xc_ref = refs
    i = pl.program_id(1)
    hb_rows, hc_rows = B_KERNEL - 1, C_KERNEL - 1

    @pl.when(i == 0)
    def _():
        xb_ref[0:CTX_B, :] = jnp.zeros((CTX_B, BRANCH_WIDTH), F32)
        xc_ref[0:CTX_C, :] = jnp.zeros((CTX_C, BRANCH_WIDTH), F32)
        if has_hist:
            xb_ref[CTX_B - hb_rows:CTX_B, :] = hb_ref[...]
            xc_ref[CTX_C - hc_rows:CTX_C, :] = hc_ref[...]

    zb = zb_ref[...]
    xb_ref[CTX_B:CTX_B + tt, :] = zb[:, :BRANCH_WIDTH] * jax.nn.sigmoid(zb[:, BRANCH_WIDTH:])
    acc = jnp.zeros((tt, BRANCH_WIDTH), F32)
    for k in range(B_KERNEL):
        acc = acc + xb_ref[pl.ds(CTX_B - hb_rows + k, tt), :] * bw_ref[k:k + 1, :]
    y = acc + bb_ref[...]
    mu = jnp.mean(y, axis=-1, keepdims=True)
    yc = y - mu
    var = jnp.mean(yc * yc, axis=-1, keepdims=True)
    o_b = jax.nn.silu(yc * lax.rsqrt(var + EPS) * lg_ref[...] + lb_ref[...])

    zc = zc_ref[...]
    gb = zc[:, :BRANCH_WIDTH]
    xc_ref[CTX_C:CTX_C + tt, :] = zc[:, BRANCH_WIDTH:2 * BRANCH_WIDTH] * zc[:, 2 * BRANCH_WIDTH:]
    acc_c = jnp.zeros((tt, BRANCH_WIDTH), F32)
    for k in range(C_KERNEL):
        acc_c = acc_c + xc_ref[pl.ds(CTX_C - hc_rows + k, tt), :] * cw_ref[k:k + 1, :]
    o_ref[...] = jnp.concatenate([o_b, gb * acc_c], axis=1).astype(o_ref.dtype)

    new_b = xb_ref[tt:tt + CTX_B, :]
    new_c = xc_ref[tt:tt + CTX_C, :]
    xb_ref[0:CTX_B, :] = new_b
    xc_ref[0:CTX_C, :] = new_c

    @pl.when(i == pl.num_programs(1) - 1)
    def _():
        nhb_ref[...] = new_b[CTX_B - hb_rows:, :]
        nhc_ref[...] = new_c[CTX_C - hc_rows:, :]


def conv_branches(z_all, row0, nb, t, tt, hist_b, hist_c, b_cw, b_cb, ln_g, ln_b, c_cw):
    nt = t // tt
    r0 = row0 // tt
    has_hist = hist_b is not None
    w = BRANCH_WIDTH
    in_specs = [
        pl.BlockSpec((tt, 2 * w), lambda b, i: (r0 + b * nt + i, ZB // (2 * w))),
        pl.BlockSpec((tt, 3 * w), lambda b, i: (r0 + b * nt + i, ZC // (3 * w))),
    ]
    args = [z_all, z_all]
    if has_hist:
        in_specs += [pl.BlockSpec((None, B_KERNEL - 1, w), lambda b, i: (b, 0, 0)),
                     pl.BlockSpec((None, C_KERNEL - 1, w), lambda b, i: (b, 0, 0))]
        args += [hist_b, hist_c]
    params = [b_cw, b_cb.reshape(1, w), ln_g.reshape(1, w), ln_b.reshape(1, w), c_cw]
    in_specs += [pl.BlockSpec(p.shape, lambda b, i: (0, 0)) for p in params]
    return pl.pallas_call(
        functools.partial(_conv_kernel, tt=tt, has_hist=has_hist),
        grid=(nb, nt),
        in_specs=in_specs,
        out_specs=[pl.BlockSpec((tt, 2 * w), lambda b, i: (b * nt + i, 0)),
                   pl.BlockSpec((None, B_KERNEL - 1, w), lambda b, i: (b, 0, 0)),
                   pl.BlockSpec((None, C_KERNEL - 1, w), lambda b, i: (b, 0, 0))],
        out_shape=[jax.ShapeDtypeStruct((nb * t, 2 * w), BF16),
                   jax.ShapeDtypeStruct((nb, B_KERNEL - 1, w), F32),
                   jax.ShapeDtypeStruct((nb, C_KERNEL - 1, w), F32)],
        scratch_shapes=[pltpu.VMEM((CTX_B + tt, w), F32), pltpu.VMEM((CTX_C + tt, w), F32)],
        compiler_params=_cparams(("parallel", "arbitrary")),
        name="conv_branches",
    )(*args, *params)


HW = D_HEADS * D_KEY
SUB = 16


def _split_dot(a, b_bf16):
    hi = a.astype(BF16)
    lo = (a - hi.astype(F32)).astype(BF16)
    return _dot(hi, b_bf16) + _dot(lo, b_bf16)


def _hgrn_kernel(*refs, tb, has_state):
    if has_state:
        zd_ref, s0_ref = refs[:2]
        refs = refs[2:]
    else:
        zd_ref = refs[0]
        refs = refs[1:]
    lb_ref, gn_ref, o_ref, sfin_ref, st_ref, vpad_ref, kpad_ref = refs
    i = pl.program_id(1)
    c = min(SUB, tb)

    @pl.when(i == 0)
    def _():
        st_ref[...] = s0_ref[...] if has_state else jnp.zeros(st_ref.shape, F32)
        if tb < LANES:
            vpad_ref[...] = jnp.zeros(vpad_ref.shape, F32)
            kpad_ref[...] = jnp.zeros(kpad_ref.shape, F32)

    zd = zd_ref[...]
    lb = lb_ref[...]
    q = jax.nn.silu(zd[:, 0:HW])
    f = lb + (1.0 - lb) * jax.nn.sigmoid(zd[:, HW:2 * HW])
    logf = jnp.log(f)
    kk = 1.0 - f
    v = zd[:, 2 * HW:3 * HW]
    gate = jax.nn.silu(zd[:, 3 * HW:4 * HW])

    row = lax.broadcasted_iota(I32, (tb, tb), 0)
    col = lax.broadcasted_iota(I32, (tb, tb), 1)
    tril_sub = ((row // c == col // c) & (col <= row)).astype(F32)
    g = _dot(tril_sub, logf, precision=HI)

    hrow = lax.broadcasted_iota(I32, (HW, HW), 0) // D_KEY
    hcol = lax.broadcasted_iota(I32, (HW, HW), 1) // D_KEY
    same_head = hrow == hcol
    head_ones = same_head.astype(BF16)

    if tb < LANES:
        vpad_ref[0:tb, :] = v
        v_t = vpad_ref[...].T
    else:
        v_t = v.T
    v_t = v_t.astype(BF16)
    tok = lax.broadcasted_iota(I32, (tb, HW), 0)
    s_idx = lax.broadcasted_iota(I32, (c, HW), 0)

    outs = []
    for j in range(tb // c):
        r0 = j * c
        gj = g[r0:r0 + c]
        qj = q[r0:r0 + c]
        kj = kk[r0:r0 + c]
        vj = v[r0:r0 + c]
        g_last = gj[c - 1:c]
        st = st_ref[...]
        o_inter = _dot((qj * jnp.exp(gj)).astype(BF16), st.astype(BF16), NT_DIMS)

        pieces = []
        for t in range(c):
            mask = s_idx <= t
            e = jnp.exp(jnp.where(mask, gj[t:t + 1] - gj, 0.0))
            pieces.append(jnp.where(mask, qj[t:t + 1] * kj * e, 0.0))
        att = _split_dot(jnp.concatenate(pieces, axis=0), head_ones)
        o_intra = jnp.sum(att.reshape(c, c, HW) * vj[None, :, :], axis=1)
        o = o_inter + o_intra
        ms = _split_dot(o * o, head_ones) * (1.0 / D_VAL)
        outs.append(o * lax.rsqrt(ms + EPS) * gn_ref[...] * gate[r0:r0 + c])

        in_sub = (tok >= r0) & (tok < r0 + c)
        kd = jnp.where(in_sub, kk * jnp.exp(jnp.where(in_sub, g_last - g, 0.0)), 0.0)
        if tb < LANES:
            kpad_ref[0:tb, :] = kd
            kd = kpad_ref[...]
        upd = _dot(v_t, kd.astype(BF16))
        st_ref[...] = st * jnp.exp(g_last) + jnp.where(same_head, upd, 0.0)

    o_ref[...] = jnp.concatenate(outs, axis=0).astype(o_ref.dtype) if len(outs) > 1 else outs[0].astype(o_ref.dtype)

    @pl.when(i == pl.num_programs(1) - 1)
    def _():
        sfin_ref[...] = st_ref[...]


def hgrn(z_all, row0, nb, t, tb, s0_t, lb, gn):
    nt = t // tb
    r0 = row0 // tb
    has_state = s0_t is not None
    in_specs = [pl.BlockSpec((tb, 4 * HW), lambda b, i: (r0 + b * nt + i, ZD // (4 * HW)))]
    args = [z_all]
    if has_state:
        in_specs.append(pl.BlockSpec((None, HW, HW), lambda b, i: (b, 0, 0)))
        args.append(s0_t)
    in_specs += [pl.BlockSpec((1, HW), lambda b, i: (0, 0)), pl.BlockSpec((1, HW), lambda b, i: (0, 0))]
    args += [lb.reshape(1, HW), jnp.tile(gn, D_HEADS).reshape(1, HW)]
    pad_rows = LANES if tb < LANES else 8
    return pl.pallas_call(
        functools.partial(_hgrn_kernel, tb=tb, has_state=has_state),
        grid=(nb, nt),
        in_specs=in_specs,
        out_specs=[pl.BlockSpec((tb, HW), lambda b, i: (b * nt + i, 0)),
                   pl.BlockSpec((None, HW, HW), lambda b, i: (b, 0, 0))],
        out_shape=[jax.ShapeDtypeStruct((nb * t, HW), BF16), jax.ShapeDtypeStruct((nb, HW, HW), F32)],
        scratch_shapes=[pltpu.VMEM((HW, HW), F32), pltpu.VMEM((pad_rows, HW), F32), pltpu.VMEM((pad_rows, HW), F32)],
        compiler_params=_cparams(("parallel", "arbitrary")),
        name="hgrn",
    )(*args)


def _merge_kernel(oa_ref, obc_ref, od_ref, ga_ref, gb_ref, gc_ref, gd_ref, wa_ref, wb_ref, wc_ref, wd_ref, wo_ref, x_ref, o_ref):
    obc = obc_ref[...]
    merged = jax.nn.sigmoid(ga_ref[...]) * _dot(oa_ref[...], wa_ref[...])
    merged += jax.nn.sigmoid(gb_ref[...]) * _dot(obc[:, :BRANCH_WIDTH], wb_ref[...])
    merged += jax.nn.sigmoid(gc_ref[...]) * _dot(obc[:, BRANCH_WIDTH:], wc_ref[...])
    merged += jax.nn.sigmoid(gd_ref[...]) * _dot(od_ref[...], wd_ref[...])
    o_ref[...] = x_ref[...] + _dot(merged.astype(BF16), wo_ref[...])


def merge(o_a, o_bc, o_d, z_all, wa_p, wb, wc, wd, w_out, x, tm=512):
    n = x.shape[0]
    row = lambda w: pl.BlockSpec((tm, w), lambda i: (i, 0))
    full = lambda a: pl.BlockSpec(a.shape, lambda i: (0, 0))
    gate = lambda k: pl.BlockSpec((tm, D_MODEL), lambda i: (i, ZG // D_MODEL + k))
    return pl.pallas_call(
        _merge_kernel,
        grid=(n // tm,),
        in_specs=[row(o_a.shape[1]), row(o_bc.shape[1]), row(o_d.shape[1]), gate(0), gate(1), gate(2), gate(3),
                  full(wa_p), full(wb), full(wc), full(wd), full(w_out), row(D_MODEL)],
        out_specs=row(D_MODEL),
        out_shape=jax.ShapeDtypeStruct((n, D_MODEL), F32),
        compiler_params=_cparams(("parallel",)),
        name="merge",
    )(o_a, o_bc, o_d, z_all, z_all, z_all, z_all, wa_p, wb, wc, wd, w_out, x)


HP = P_HEADS * 2
SLOTS = P_HEADS * P_TOPK
SUBK = P_DKEY // 2


def _take_max(s, index, n):
    m = jnp.max(s, axis=0, keepdims=True)
    idx = jnp.min(jnp.where(s == m, index, n), axis=0, keepdims=True)
    return m, idx, jnp.where(index == idx, -jnp.inf, s)


_CAND_BLOCKS = [(0, P_TOPK)] + [(a, 8) for a in range(1, 8)]
_CAND_ROWS = sum(nb for _, nb in _CAND_BLOCKS) + 8


def _route_kernel(x_ref, g_ref, wq_ref, keys_ref, h_ref, ii_ref, jj_ref, gw_ref, q_s, sv_s, si_s, oi_s, oj_s, og_s, *, tm):
    x = x_ref[...]
    h = (x * lax.rsqrt(jnp.mean(x * x, axis=-1, keepdims=True) + EPS) * g_ref[...]).astype(BF16)
    h_ref[...] = h
    q_s[...] = _dot(h, wq_ref[...]).astype(BF16)

    iota_k = lax.broadcasted_iota(I32, (P_NKEYS, tm), 0).astype(F32)

    def stage1(hp, carry):
        q = q_s[:, pl.ds(pl.multiple_of(hp * SUBK, SUBK), SUBK)]
        s = _dot(keys_ref[hp], q, NT_DIMS)
        vals, idxs = [], []
        for _ in range(P_TOPK):
            m, idx, s = _take_max(s, iota_k, float(P_NKEYS))
            vals.append(m)
            idxs.append(idx)
        sv_s[hp] = jnp.concatenate(vals, axis=0)
        si_s[hp] = jnp.concatenate(idxs, axis=0)
        return carry

    lax.fori_loop(0, HP, stage1, 0, unroll=2)

    r = lax.broadcasted_iota(I32, (_CAND_ROWS, tm), 0)
    mid = r - P_TOPK
    flat = jnp.where(r < P_TOPK, r, jnp.where(r < _CAND_ROWS - 8, (1 + mid // 8) * P_TOPK + mid % 8, (r - (_CAND_ROWS - 16)) * P_TOPK))
    flat = flat.astype(F32)
    iota_t = lax.broadcasted_iota(I32, (P_TOPK, tm), 0).astype(F32)

    def stage2(hd, carry):
        sv1, sv2 = sv_s[2 * hd], sv_s[2 * hd + 1]
        si1, si2 = si_s[2 * hd], si_s[2 * hd + 1]
        cand = jnp.concatenate([sv1[a:a + 1] + sv2[0:nb] for a, nb in _CAND_BLOCKS] + [sv1[8:P_TOPK] + sv2[0:1]], axis=0)
        fv, ei, ej = [], [], []
        for _ in range(P_TOPK):
            m, idx, cand = _take_max(cand, flat, float(P_TOPK * P_TOPK))
            a = jnp.floor(idx * (1.0 / P_TOPK))
            b = idx - a * P_TOPK
            fv.append(m)
            ei.append(jnp.sum(jnp.where(iota_t == a, si1, 0.0), axis=0, keepdims=True))
            ej.append(jnp.sum(jnp.where(iota_t == b, si2, 0.0), axis=0, keepdims=True))
        fv = jnp.concatenate(fv, axis=0)
        e = jnp.exp(fv - fv[0:1])
        rows = pl.ds(pl.multiple_of(hd * P_TOPK, P_TOPK), P_TOPK)
        og_s[rows, :] = e / jnp.sum(e, axis=0, keepdims=True)
        oi_s[rows, :] = jnp.concatenate(ei, axis=0)
        oj_s[rows, :] = jnp.concatenate(ej, axis=0)
        return carry

    lax.fori_loop(0, P_HEADS, stage2, 0, unroll=2)
    ii_ref[...] = oi_s[...].T.astype(I32)
    jj_ref[...] = oj_s[...].T.astype(I32)
    gw_ref[...] = og_s[...].T


def peer_route(x, g, wq, keys, tm=256):
    n = x.shape[0]
    row = lambda w: pl.BlockSpec((tm, w), lambda i: (i, 0))
    return pl.pallas_call(
        functools.partial(_route_kernel, tm=tm),
        grid=(n // tm,),
        in_specs=[row(D_MODEL), pl.BlockSpec((1, D_MODEL), lambda i: (0, 0)),
                  pl.BlockSpec(wq.shape, lambda i: (0, 0)), pl.BlockSpec(keys.shape, lambda i: (0, 0, 0))],
        out_specs=[row(D_MODEL), row(SLOTS), row(SLOTS), row(SLOTS)],
        out_shape=[jax.ShapeDtypeStruct((n, D_MODEL), BF16), jax.ShapeDtypeStruct((n, SLOTS), I32),
                   jax.ShapeDtypeStruct((n, SLOTS), I32), jax.ShapeDtypeStruct((n, SLOTS), F32)],
        scratch_shapes=[pltpu.VMEM((tm, HP * SUBK), BF16), pltpu.VMEM((HP, P_TOPK, tm), F32), pltpu.VMEM((HP, P_TOPK, tm), F32),
                        pltpu.VMEM((SLOTS, tm), F32), pltpu.VMEM((SLOTS, tm), F32), pltpu.VMEM((SLOTS, tm), F32)],
        compiler_params=_cparams(("parallel",)),
        name="peer_route",
    )(x, g.reshape(1, D_MODEL), wq, keys)


CHUNK_I = 16
CHUNK_E = CHUNK_I * P_NKEYS
N_CHUNKS = P_NKEYS // CHUNK_I


def _peer_kernel(h_ref, ii_ref, jj_ref, gw_ref, u_ref, v_ref, x_ref, o_ref, w_s, acc_s, *, tm):
    c = pl.program_id(1)

    @pl.when(c == 0)
    def _():
        acc_s[...] = jnp.zeros(acc_s.shape, F32)
        iota = lax.broadcasted_iota(I32, (P_NKEYS, SLOTS), 0)

        def build(g, carry):
            t0 = pl.multiple_of(g * SUBLANES, SUBLANES)
            ws = []
            for t in range(SUBLANES):
                irow = ii_ref[pl.ds(t0 + t, 1), :]
                jrow = jj_ref[pl.ds(t0 + t, 1), :]
                grow = gw_ref[pl.ds(t0 + t, 1), :]
                p_t = jnp.where(iota == irow, grow, 0.0).astype(BF16)
                q_t = jnp.where(iota == jrow, 1.0, 0.0).astype(BF16)
                ws.append(_dot(p_t, q_t, NT_DIMS))
            w_s[:, pl.ds(t0, SUBLANES), :] = pltpu.einshape("tij->itj", jnp.stack(ws, axis=0))
            return carry

        lax.fori_loop(0, tm // SUBLANES, build, 0)

    a = _dot(h_ref[...], u_ref[...], NT_DIMS)
    act = 0.5 * a * (1.0 + lax.erf(a * (1.0 / math.sqrt(2.0))))
    wd = jnp.concatenate([w_s[c * CHUNK_I + r] for r in range(CHUNK_I)], axis=1)
    acc_s[...] += _dot((act * wd).astype(BF16), v_ref[...])

    @pl.when(c == N_CHUNKS - 1)
    def _():
        o_ref[...] = x_ref[...] + acc_s[...]


def peer_experts(h2, ii, jj, gw, u_tab, v_tab, x, tm=256):
    n = x.shape[0]
    row = lambda w: pl.BlockSpec((tm, w), lambda i, c: (i, 0))
    return pl.pallas_call(
        functools.partial(_peer_kernel, tm=tm),
        grid=(n // tm, N_CHUNKS),
        in_specs=[row(D_MODEL), row(SLOTS), row(SLOTS), row(SLOTS),
                  pl.BlockSpec((CHUNK_E, D_MODEL), lambda i, c: (c, 0)),
                  pl.BlockSpec((CHUNK_E, D_MODEL), lambda i, c: (c, 0)),
                  row(D_MODEL)],
        out_specs=row(D_MODEL),
        out_shape=jax.ShapeDtypeStruct((n, D_MODEL), F32),
        scratch_shapes=[pltpu.VMEM((P_NKEYS, tm, P_NKEYS), F32), pltpu.VMEM((tm, D_MODEL), F32)],
        compiler_params=_cparams(("parallel", "arbitrary")),
        name="peer_experts",
    )(h2, ii, jj, gw, u_tab, v_tab, x)


def _rmsnorm_kernel(x_ref, g_ref, o_ref):
    x = x_ref[...]
    o_ref[...] = x * lax.rsqrt(jnp.mean(x * x, axis=-1, keepdims=True) + EPS) * g_ref[...]


def rmsnorm(x, g, tm=512):
    n, d = x.shape
    return pl.pallas_call(
        _rmsnorm_kernel,
        grid=(n // tm,),
        in_specs=[pl.BlockSpec((tm, d), lambda i: (i, 0)), pl.BlockSpec((1, d), lambda i: (0, 0))],
        out_specs=pl.BlockSpec((tm, d), lambda i: (i, 0)),
        out_shape=jax.ShapeDtypeStruct((n, d), F32),
        compiler_params=_cparams(("parallel",)),
        name="rmsnorm",
    )(x, g.reshape(1, d))


def _rope_tables():
    half = A_ROPE // 2
    inv = ROPE_THETA ** (-jnp.arange(half, dtype=F32) / half)
    pos = jnp.concatenate([jnp.tile(jnp.arange(SEQ), BATCH), jnp.tile(PAST_LEN + jnp.arange(DEC_SEQ), DEC_BATCH)])
    ang = pos.astype(F32)[:, None] * inv[None, :]
    cos, sin = jnp.cos(ang), jnp.sin(ang)
    zeros = lambda w: jnp.zeros((NT, w), F32)
    ccq = jnp.concatenate([jnp.ones((NT, A_NOPE), F32), cos, cos, zeros(HEAD_PAD - A_NOPE - A_ROPE)], axis=1)
    ssq = jnp.concatenate([zeros(A_NOPE), -sin, sin, zeros(HEAD_PAD - A_NOPE - A_ROPE)], axis=1)
    cck = jnp.concatenate([cos, cos, zeros(LANES - A_ROPE)], axis=1)
    ssk = jnp.concatenate([-sin, sin, zeros(LANES - A_ROPE)], axis=1)
    return ccq, ssq, cck, ssk


def _swap_halves(w):
    half = w.shape[-1] // 2
    return jnp.concatenate([w[..., half:], w[..., :half]], axis=-1)


def _layer_weights(l, w_in, a_w_uq, a_w_uk, a_w_uv, w_branch, w_out, p_w_q, p_sub_keys, p_u, p_v):
    w = w_in[l]
    o = 0
    parts = []
    for size in (A_Q_LORA, A_KV_LORA, A_ROPE, 2 * BRANCH_WIDTH, 3 * BRANCH_WIDTH, 4 * HW, N_BRANCH * D_MODEL):
        parts.append(w[:, o:o + size])
        o += size
    cq, ckv, kr, b_in, c_in, d_in, gate = parts
    pad = jnp.zeros((D_MODEL, ZB - A_Q_LORA - A_KV_LORA - 2 * A_ROPE), F32)
    w_in_p = jnp.concatenate([cq, ckv, kr, _swap_halves(kr), pad, b_in, d_in, gate, c_in], axis=1).astype(BF16)

    wq = a_w_uq[l].reshape(A_Q_LORA, A_HEADS, A_NOPE + A_ROPE)
    nope, rope = wq[..., :A_NOPE], wq[..., A_NOPE:]
    tail = jnp.zeros((A_Q_LORA, A_HEADS, HEAD_PAD - A_NOPE - A_ROPE), F32)
    wq_main = jnp.concatenate([nope, rope, tail], axis=-1).reshape(A_Q_LORA, -1).astype(BF16)
    wq_swap = jnp.concatenate([jnp.zeros_like(nope), _swap_halves(rope), tail], axis=-1).reshape(A_Q_LORA, -1).astype(BF16)

    w_uk, w_uv = a_w_uk[l], a_w_uv[l]
    head_tail = jnp.zeros((A_KV_LORA, A_HEADS, HEAD_PAD - A_NOPE), F32)
    wuk_p = jnp.concatenate([w_uk, head_tail], axis=-1).reshape(A_KV_LORA, -1).astype(BF16)
    wuv_p = jnp.concatenate([w_uv, head_tail], axis=-1).reshape(A_KV_LORA, -1).astype(BF16)

    r = jnp.arange(LANES)[:, None]
    col = jnp.arange(A_HEADS * HEAD_PAD)[None, :]
    place = ((col % HEAD_PAD == A_NOPE + r) & (r < A_ROPE)).astype(BF16)

    blk = jnp.zeros((A_HEADS, HEAD_PAD, 2 * LANES), F32)
    blk = blk.at[:, :A_NOPE, :A_KV_LORA].set(jnp.transpose(w_uk, (1, 2, 0)))
    blk = blk.at[:, A_NOPE:A_NOPE + A_ROPE, A_KV_LORA:A_KV_LORA + A_ROPE].set(jnp.eye(A_ROPE, dtype=F32))
    eye_h = jnp.eye(A_HEADS, dtype=F32)
    wabs = (blk[:, :, None, :] * eye_h[:, None, :, None]).reshape(A_HEADS * HEAD_PAD, A_HEADS * 2 * LANES).astype(BF16)

    wb = w_branch[l]
    wa_p = jnp.concatenate([wb[0].reshape(A_HEADS, A_V, D_MODEL), jnp.zeros((A_HEADS, HEAD_PAD - A_V, D_MODEL), F32)],
                           axis=1).reshape(A_HEADS * HEAD_PAD, D_MODEL).astype(BF16)
    return dict(w_in_p=w_in_p, wq_main=wq_main, wq_swap=wq_swap, wuk_p=wuk_p, wuv_p=wuv_p, place=place, wabs=wabs,
                wa_p=wa_p, wb=wb[1].astype(BF16), wc=wb[2].astype(BF16), wd=wb[3].astype(BF16), w_out=w_out[l].astype(BF16),
                wq=p_w_q[l].astype(BF16), keys=p_sub_keys[l].reshape(HP, P_NKEYS, SUBK).astype(BF16),
                u=p_u[l].astype(BF16), v=p_v[l].astype(BF16))


def _state_to_kernel(s):
    t = jnp.transpose(s, (0, 1, 3, 2))
    eye_h = jnp.eye(D_HEADS, dtype=F32)
    return (t[:, :, :, None, :] * eye_h[None, :, None, :, None]).reshape(s.shape[0], HW, HW)


def _state_from_kernel(s_t):
    s5 = s_t.reshape(s_t.shape[0], D_HEADS, D_VAL, D_HEADS, D_KEY)
    diag = jnp.stack([s5[:, h, :, h, :] for h in range(D_HEADS)], axis=1)
    return jnp.transpose(diag, (0, 1, 3, 2))


def kernel(x_prompt, x_sample, cache_ckv, cache_krope, page_table, state_conv_b, state_conv_c, state_hgrn, norm1_g, w_in, a_q_norm_g, a_w_uq, a_kv_norm_g, a_w_uk, a_w_uv, b_conv_w, b_conv_b, b_ln_g, b_ln_b, c_conv_w, d_lower_bound, d_gnorm_g, w_branch, w_out, norm2_g, p_w_q, p_sub_keys, p_u, p_v, final_norm_g):
    lb_soft = jax.nn.softmax(d_lower_bound.astype(F32), axis=0)
    lower_bounds = jnp.cumsum(lb_soft, axis=0) - lb_soft[0:1]
    ccq, ssq, cck, ssk = _rope_tables()
    x = jnp.concatenate([x_prompt.reshape(NP, D_MODEL), x_sample.reshape(NS, D_MODEL)], axis=0)
    cache_krope_t = jnp.swapaxes(cache_krope, 2, 3)

    states_p, states_s = [], []
    for l in range(DEPTH):
        w = _layer_weights(l, w_in, a_w_uq, a_w_uk, a_w_uv, w_branch, w_out, p_w_q, p_sub_keys, p_u, p_v)
        z = norm_matmul(x, norm1_g[l], w["w_in_p"], 512, 1152)

        ckv, kr, qp, kp, vp = mla_prep(z, a_q_norm_g[l], a_kv_norm_g[l], w["wq_main"], w["wq_swap"], w["wuk_p"], w["wuv_p"],
                                       w["place"], ccq, ssq, cck, ssk)
        oa_p = flash_prompt(qp, kp, vp)
        qlat = matmul(qp[NP:], w["wabs"], 512, BF16)
        qlat = jnp.transpose(qlat.reshape(DEC_BATCH, DEC_SEQ, A_HEADS, 2 * LANES), (0, 2, 1, 3)).reshape(DEC_BATCH, DEC_ROWS, 2 * LANES)
        oa_s = mla_decode(l, page_table, qlat, cache_ckv, cache_krope_t, ckv[NP:].reshape(DEC_BATCH, DEC_SEQ, LANES),
                          kr[NP:].reshape(DEC_BATCH, DEC_SEQ, LANES), w["wuv_p"])
        o_a = jnp.concatenate([oa_p, oa_s.reshape(NS, -1)], axis=0)

        conv_w = (b_conv_w[l], b_conv_b[l], b_ln_g[l], b_ln_b[l], c_conv_w[l])
        obc_p, hb_p, hc_p = conv_branches(z, 0, BATCH, SEQ, 512, None, None, *conv_w)
        obc_s, hb_s, hc_s = conv_branches(z, NP, DEC_BATCH, DEC_SEQ, DEC_SEQ, state_conv_b[l], state_conv_c[l], *conv_w)
        o_bc = jnp.concatenate([obc_p, obc_s], axis=0)

        od_p, st_p = hgrn(z, 0, BATCH, SEQ, LANES, None, lower_bounds[l], d_gnorm_g[l])
        od_s, st_s = hgrn(z, NP, DEC_BATCH, DEC_SEQ, DEC_SEQ, _state_to_kernel(state_hgrn[l]), lower_bounds[l], d_gnorm_g[l])
        o_d = jnp.concatenate([od_p, od_s], axis=0)

        x1 = merge(o_a, o_bc, o_d, z, w["wa_p"], w["wb"], w["wc"], w["wd"], w["w_out"], x)
        h2, ii, jj, gw = peer_route(x1, norm2_g[l], w["wq"], w["keys"])
        x = peer_experts(h2, ii, jj, gw, w["u"], w["v"], x1)

        states_p.append((ckv[:NP].reshape(BATCH, SEQ, A_KV_LORA), kr[:NP, :A_ROPE].reshape(BATCH, SEQ, A_ROPE),
                         hb_p, hc_p, _state_from_kernel(st_p)))
        states_s.append((ckv[NP:].reshape(DEC_BATCH, DEC_SEQ, A_KV_LORA), kr[NP:, :A_ROPE].reshape(DEC_BATCH, DEC_SEQ, A_ROPE),
                         hb_s, hc_s, _state_from_kernel(st_s)))

    y = rmsnorm(x, final_norm_g)
    stack = lambda states: [jnp.stack([s[i] for s in states], axis=0) for i in range(5)]
    return (y[:NP].reshape(BATCH, SEQ, D_MODEL), y[NP:].reshape(DEC_BATCH, DEC_SEQ, D_MODEL), *stack(states_p), *stack(states_s))
```

```python
import functools
import math

import jax
import jax.numpy as jnp
from jax import lax
from jax.experimental import pallas as pl
from jax.experimental.pallas import tpu as pltpu

F32 = jnp.float32
BF16 = jnp.bfloat16
I32 = jnp.int32

D_MODEL = 1024
BATCH = 2
SEQ = 8192
DEPTH = 2
DEC_BATCH = 128
DEC_SEQ = 8
PAST_LEN = 8192
PAGE_SIZE = 128
N_PAGES = PAST_LEN // PAGE_SIZE
BRANCH_WIDTH = 256
N_BRANCH = 4
A_HEADS = 4
A_NOPE = 64
A_ROPE = 32
A_V = 64
A_Q_LORA = 256
A_KV_LORA = 128
ROPE_THETA = 10000.0
B_KERNEL = 31
C_KERNEL = 3
D_HEADS = 4
D_KEY = 64
D_VAL = 64
P_HEADS = 8
P_NKEYS = 128
P_DKEY = 256
P_TOPK = 16
P_EXPERTS = P_NKEYS * P_NKEYS
EPS = 1e-6
NEG_BIG = -1e30

NP = BATCH * SEQ
NS = DEC_BATCH * DEC_SEQ
NT = NP + NS

LANES = 128
SUBLANES = 8
PACK = 16
HEAD_PAD = 128
ZA, ZB, ZD, ZG, ZC = 0, 512, 1024, 2048, 6144
Z_COLS = 6912
VMEM_LIMIT = 56 * 1024 * 1024

HI = lax.Precision.HIGHEST


def _cparams(sem):
    return pltpu.CompilerParams(dimension_semantics=sem, vmem_limit_bytes=VMEM_LIMIT)


def _dot(a, b, dims=(((1,), (0,)), ((), ())), precision=None):
    return lax.dot_general(a, b, dims, precision=precision, preferred_element_type=F32)


NT_DIMS = (((1,), (1,)), ((), ()))
TN_DIMS = (((0,), (0,)), ((), ()))


def _norm_matmul_kernel(x_ref, g_ref, w_ref, o_ref, h_ref):
    @pl.when(pl.program_id(1) == 0)
    def _():
        x = x_ref[...]
        y = x * lax.rsqrt(jnp.mean(x * x, axis=-1, keepdims=True) + EPS)
        h_ref[...] = (y * g_ref[...]).astype(BF16)

    o_ref[...] = _dot(h_ref[...], w_ref[...])


def norm_matmul(x, g, w, tm, tn):
    n, d = x.shape
    cols = w.shape[1]
    return pl.pallas_call(
        _norm_matmul_kernel,
        grid=(n // tm, cols // tn),
        in_specs=[
            pl.BlockSpec((tm, d), lambda i, j: (i, 0)),
            pl.BlockSpec((1, d), lambda i, j: (0, 0)),
            pl.BlockSpec((d, tn), lambda i, j: (0, j)),
        ],
        out_specs=pl.BlockSpec((tm, tn), lambda i, j: (i, j)),
        out_shape=jax.ShapeDtypeStruct((n, cols), F32),
        scratch_shapes=[pltpu.VMEM((tm, d), BF16)],
        compiler_params=_cparams(("parallel", "arbitrary")),
        name="norm_matmul",
    )(x, g.reshape(1, d), w)


def _matmul_kernel(x_ref, w_ref, o_ref):
    o_ref[...] = _dot(x_ref[...], w_ref[...]).astype(o_ref.dtype)


def matmul(x, w, tm, out_dtype):
    n, d = x.shape
    cols = w.shape[1]
    return pl.pallas_call(
        _matmul_kernel,
        grid=(n // tm,),
        in_specs=[pl.BlockSpec((tm, d), lambda i: (i, 0)), pl.BlockSpec((d, cols), lambda i: (0, 0))],
        out_specs=pl.BlockSpec((tm, cols), lambda i: (i, 0)),
        out_shape=jax.ShapeDtypeStruct((n, cols), out_dtype),
        compiler_params=_cparams(("parallel",)),
        name="matmul",
    )(x, w)


def _tile4(t):
    return jnp.concatenate([t, t, t, t], axis=1)


def _mla_prep_kernel(z_ref, qn_ref, kvn_ref, wqm_ref, wqs_ref, wuk_ref, wuv_ref, place_ref,
                     ccq_ref, ssq_ref, cck_ref, ssk_ref,
                     ckv_ref, kr_ref, q_ref, k_ref, v_ref):
    z = z_ref[...]
    cq = z[:, 0:A_Q_LORA]
    cqn = cq * lax.rsqrt(jnp.mean(cq * cq, axis=-1, keepdims=True) + EPS) * qn_ref[...]
    cqn = cqn.astype(BF16)
    scale = 1.0 / math.sqrt(A_NOPE + A_ROPE)
    q = _dot(cqn, wqm_ref[...]) * _tile4(ccq_ref[...]) + _dot(cqn, wqs_ref[...]) * _tile4(ssq_ref[...])
    q_ref[...] = (q * scale).astype(BF16)

    c = z[:, A_Q_LORA:A_Q_LORA + A_KV_LORA]
    ckv = c * lax.rsqrt(jnp.mean(c * c, axis=-1, keepdims=True) + EPS) * kvn_ref[...]
    ckv_ref[...] = ckv
    ckv_b = ckv.astype(BF16)

    kc = z[:, A_Q_LORA + A_KV_LORA:]
    kr = kc * cck_ref[...] + pltpu.roll(kc, LANES - A_ROPE, axis=1) * ssk_ref[...]
    kr_ref[...] = kr

    k = _dot(ckv_b, wuk_ref[...]) + _dot(kr.astype(BF16), place_ref[...])
    k_ref[...] = k.astype(BF16)
    v_ref[...] = _dot(ckv_b, wuv_ref[...]).astype(BF16)


def mla_prep(z_all, qn_g, kvn_g, wq_main, wq_swap, wuk_p, wuv_p, place, ccq, ssq, cck, ssk, tm=512):
    n = z_all.shape[0]
    hp = A_HEADS * HEAD_PAD
    row = lambda w: pl.BlockSpec((tm, w), lambda i: (i, 0))
    full = lambda a: pl.BlockSpec(a.shape, lambda i: (0,) * a.ndim)
    qn_g = qn_g.reshape(1, -1)
    kvn_g = kvn_g.reshape(1, -1)
    return pl.pallas_call(
        _mla_prep_kernel,
        grid=(n // tm,),
        in_specs=[row(512), full(qn_g), full(kvn_g), full(wq_main), full(wq_swap), full(wuk_p), full(wuv_p), full(place),
                  row(LANES), row(LANES), row(LANES), row(LANES)],
        out_specs=[row(LANES), row(LANES), row(hp), row(hp), row(hp)],
        out_shape=[jax.ShapeDtypeStruct((n, LANES), F32), jax.ShapeDtypeStruct((n, LANES), F32),
                   jax.ShapeDtypeStruct((n, hp), BF16), jax.ShapeDtypeStruct((n, hp), BF16),
                   jax.ShapeDtypeStruct((n, hp), BF16)],
        compiler_params=_cparams(("parallel",)),
        name="mla_prep",
    )(z_all, qn_g, kvn_g, wq_main, wq_swap, wuk_p, wuv_p, place, ccq, ssq, cck, ssk)


def _flash_kernel(q_ref, k_ref, v_ref, o_ref, *, tq, tk):
    i = pl.program_id(2)
    q = q_ref[...]

    def step(j, carry, masked):
        m, l, acc = carry
        kj = k_ref[pl.ds(pl.multiple_of(j * tk, tk), tk), :]
        vj = v_ref[pl.ds(pl.multiple_of(j * tk, tk), tk), :]
        s = _dot(q, kj, NT_DIMS)
        if masked:
            qpos = i * tq + lax.broadcasted_iota(I32, (tq, tk), 0)
            kpos = j * tk + lax.broadcasted_iota(I32, (tq, tk), 1)
            s = jnp.where(kpos <= qpos, s, NEG_BIG)
        m_new = jnp.maximum(m, jnp.max(s, axis=-1, keepdims=True))
        alpha = jnp.exp(m - m_new)
        p = jnp.exp(s - m_new)
        l = alpha * l + jnp.sum(p, axis=-1, keepdims=True)
        acc = alpha * acc + _dot(p.astype(BF16), vj)
        return m_new, l, acc

    init = (jnp.full((tq, 1), NEG_BIG, F32), jnp.zeros((tq, 1), F32), jnp.zeros((tq, HEAD_PAD), F32))
    n_full = (i * tq) // tk
    carry = lax.fori_loop(0, n_full, lambda j, c: step(j, c, False), init)
    for d in range(tq // tk):
        carry = step(n_full + d, carry, True)
    m, l, acc = carry
    o_ref[...] = (acc / l).astype(o_ref.dtype)


def flash_prompt(qp, kp, vp, tq=512, tk=512):
    nq = SEQ // tq
    return pl.pallas_call(
        functools.partial(_flash_kernel, tq=tq, tk=tk),
        grid=(BATCH, A_HEADS, nq),
        in_specs=[
            pl.BlockSpec((tq, HEAD_PAD), lambda b, h, i: (b * nq + i, h)),
            pl.BlockSpec((SEQ, HEAD_PAD), lambda b, h, i: (b, h)),
            pl.BlockSpec((SEQ, HEAD_PAD), lambda b, h, i: (b, h)),
        ],
        out_specs=pl.BlockSpec((tq, HEAD_PAD), lambda b, h, i: (b * nq + i, h)),
        out_shape=jax.ShapeDtypeStruct((NP, A_HEADS * HEAD_PAD), BF16),
        compiler_params=_cparams(("parallel", "parallel", "arbitrary")),
        name="flash_prompt",
    )(qp, kp, vp)


DEC_ROWS = A_HEADS * DEC_SEQ
PAGES_PER_STEP = 32


def _decode_kernel(pt_ref, q_ref, *refs):
    del pt_ref
    pg = PAGES_PER_STEP
    ckv_refs = refs[:pg]
    kr_refs = refs[pg:2 * pg]
    cnew_ref, knew_ref, wuv_ref, o_ref, m_ref, l_ref, acc_ref = refs[2 * pg:]
    g = pl.program_id(1)

    @pl.when(g == 0)
    def _():
        m_ref[...] = jnp.full(m_ref.shape, NEG_BIG, F32)
        l_ref[...] = jnp.zeros(l_ref.shape, F32)
        acc_ref[...] = jnp.zeros(acc_ref.shape, F32)

    q = q_ref[...]
    q_lat = q[:, :A_KV_LORA]
    q_rope = q[:, A_KV_LORA:A_KV_LORA + A_ROPE]

    def update(s, v):
        m_old = m_ref[...]
        m_new = jnp.maximum(m_old, jnp.max(s, axis=-1, keepdims=True))
        alpha = jnp.exp(m_old - m_new)
        p = jnp.exp(s - m_new)
        m_ref[...] = m_new
        l_ref[...] = alpha * l_ref[...] + jnp.sum(p, axis=-1, keepdims=True)
        acc_ref[...] = alpha * acc_ref[...] + _dot(p.astype(BF16), v)

    kc = jnp.concatenate([ckv_refs[r][...].astype(BF16) for r in range(pg)], axis=0)
    kr_t = jnp.concatenate([kr_refs[r][...].astype(BF16) for r in range(pg)], axis=1)
    update(_dot(q_lat, kc, NT_DIMS) + _dot(q_rope, kr_t), kc)

    @pl.when(g == pl.num_programs(1) - 1)
    def _():
        cn = cnew_ref[...].astype(BF16)
        kn = knew_ref[...][:, :A_ROPE].astype(BF16)
        s = _dot(q_lat, cn, NT_DIMS) + _dot(q_rope, kn, NT_DIMS)
        t_q = lax.broadcasted_iota(I32, s.shape, 0) % DEC_SEQ
        t_k = lax.broadcasted_iota(I32, s.shape, 1)
        s = jnp.where(t_k <= t_q, s, NEG_BIG)
        update(s, cn)
        o_lat = (acc_ref[...] / l_ref[...]).astype(BF16)
        for h in range(A_HEADS):
            o_ref[:, h * HEAD_PAD:(h + 1) * HEAD_PAD] = _dot(
                o_lat[h * DEC_SEQ:(h + 1) * DEC_SEQ, :], wuv_ref[:, h * HEAD_PAD:(h + 1) * HEAD_PAD]).astype(o_ref.dtype)


def mla_decode(layer, page_table, qlat, cache_ckv, cache_krope_t, ckv_new, kr_new, wuv_p):
    pg = PAGES_PER_STEP

    def page_spec(rows, cols, r):
        return pl.BlockSpec((None, None, rows, cols), lambda b, g, pt: (layer, pt[b, g * pg + r], 0, 0))

    in_specs = [pl.BlockSpec((None, DEC_ROWS, 2 * LANES), lambda b, g, pt: (b, 0, 0))]
    in_specs += [page_spec(PAGE_SIZE, A_KV_LORA, r) for r in range(pg)]
    in_specs += [page_spec(A_ROPE, PAGE_SIZE, r) for r in range(pg)]
    in_specs += [
        pl.BlockSpec((None, DEC_SEQ, LANES), lambda b, g, pt: (b, 0, 0)),
        pl.BlockSpec((None, DEC_SEQ, LANES), lambda b, g, pt: (b, 0, 0)),
        pl.BlockSpec(wuv_p.shape, lambda b, g, pt: (0, 0)),
    ]
    grid_spec = pltpu.PrefetchScalarGridSpec(
        num_scalar_prefetch=1,
        grid=(DEC_BATCH, N_PAGES // pg),
        in_specs=in_specs,
        out_specs=pl.BlockSpec((None, DEC_SEQ, A_HEADS * HEAD_PAD), lambda b, g, pt: (b, 0, 0)),
        scratch_shapes=[pltpu.VMEM((DEC_ROWS, 1), F32), pltpu.VMEM((DEC_ROWS, 1), F32), pltpu.VMEM((DEC_ROWS, A_KV_LORA), F32)],
    )
    return pl.pallas_call(
        _decode_kernel,
        grid_spec=grid_spec,
        out_shape=jax.ShapeDtypeStruct((DEC_BATCH, DEC_SEQ, A_HEADS * HEAD_PAD), BF16),
        compiler_params=_cparams(("parallel", "arbitrary")),
        name="mla_decode",
    )(page_table, qlat, *([cache_ckv] * pg), *([cache_krope_t] * pg), ckv_new, kr_new, wuv_p)


CTX_B = 32
CTX_C = 8


def _conv_kernel(*refs, tt, has_hist):
    if has_hist:
        zb_ref, zc_ref, hb_ref, hc_ref = refs[:4]
        refs = refs[4:]
    else:
        zb_ref, zc_ref = refs[:2]
        refs = refs[2:]
    bw_ref, bb_ref, lg_ref, lb_ref, cw_ref, o_ref, nhb_ref, nhc_ref, xb_ref, xc_ref = refs
    i = pl.program_id(1)
    hb_rows, hc_rows = B_KERNEL - 1, C_KERNEL - 1

    @pl.when(i == 0)
    def _():
        xb_ref[0:CTX_B, :] = jnp.zeros((CTX_B, BRANCH_WIDTH), F32)
        xc_ref[0:CTX_C, :] = jnp.zeros((CTX_C, BRANCH_WIDTH), F32)
        if has_hist:
            xb_ref[CTX_B - hb_rows:CTX_B, :] = hb_ref[...]
            xc_ref[CTX_C - hc_rows:CTX_C, :] = hc_ref[...]

    zb = zb_ref[...]
    xb_ref[CTX_B:CTX_B + tt, :] = zb[:, :BRANCH_WIDTH] * jax.nn.sigmoid(zb[:, BRANCH_WIDTH:])
    acc = jnp.zeros((tt, BRANCH_WIDTH), F32)
    for k in range(B_KERNEL):
        acc = acc + xb_ref[pl.ds(CTX_B - hb_rows + k, tt), :] * bw_ref[k:k + 1, :]
    y = acc + bb_ref[...]
    mu = jnp.mean(y, axis=-1, keepdims=True)
    yc = y - mu
    var = jnp.mean(yc * yc, axis=-1, keepdims=True)
    o_b = jax.nn.silu(yc * lax.rsqrt(var + EPS) * lg_ref[...] + lb_ref[...])

    zc = zc_ref[...]
    gb = zc[:, :BRANCH_WIDTH]
    xc_ref[CTX_C:CTX_C + tt, :] = zc[:, BRANCH_WIDTH:2 * BRANCH_WIDTH] * zc[:, 2 * BRANCH_WIDTH:]
    acc_c = jnp.zeros((tt, BRANCH_WIDTH), F32)
    for k in range(C_KERNEL):
        acc_c = acc_c + xc_ref[pl.ds(CTX_C - hc_rows + k, tt), :] * cw_ref[k:k + 1, :]
    o_ref[...] = jnp.concatenate([o_b, gb * acc_c], axis=1).astype(o_ref.dtype)

    new_b = xb_ref[tt:tt + CTX_B, :]
    new_c = xc_ref[tt:tt + CTX_C, :]
    xb_ref[0:CTX_B, :] = new_b
    xc_ref[0:CTX_C, :] = new_c

    @pl.when(i == pl.num_programs(1) - 1)
    def _():
        nhb_ref[...] = new_b[CTX_B - hb_rows:, :]
        nhc_ref[...] = new_c[CTX_C - hc_rows:, :]


def conv_branches(z_all, row0, nb, t, tt, hist_b, hist_c, b_cw, b_cb, ln_g, ln_b, c_cw):
    nt = t // tt
    r0 = row0 // tt
    has_hist = hist_b is not None
    w = BRANCH_WIDTH
    in_specs = [
        pl.BlockSpec((tt, 2 * w), lambda b, i: (r0 + b * nt + i, ZB // (2 * w))),
        pl.BlockSpec((tt, 3 * w), lambda b, i: (r0 + b * nt + i, ZC // (3 * w))),
    ]
    args = [z_all, z_all]
    if has_hist:
        in_specs += [pl.BlockSpec((None, B_KERNEL - 1, w), lambda b, i: (b, 0, 0)),
                     pl.BlockSpec((None, C_KERNEL - 1, w), lambda b, i: (b, 0, 0))]
        args += [hist_b, hist_c]
    params = [b_cw, b_cb.reshape(1, w), ln_g.reshape(1, w), ln_b.reshape(1, w), c_cw]
    in_specs += [pl.BlockSpec(p.shape, lambda b, i: (0, 0)) for p in params]
    return pl.pallas_call(
        functools.partial(_conv_kernel, tt=tt, has_hist=has_hist),
        grid=(nb, nt),
        in_specs=in_specs,
        out_specs=[pl.BlockSpec((tt, 2 * w), lambda b, i: (b * nt + i, 0)),
                   pl.BlockSpec((None, B_KERNEL - 1, w), lambda b, i: (b, 0, 0)),
                   pl.BlockSpec((None, C_KERNEL - 1, w), lambda b, i: (b, 0, 0))],
        out_shape=[jax.ShapeDtypeStruct((nb * t, 2 * w), BF16),
                   jax.ShapeDtypeStruct((nb, B_KERNEL - 1, w), F32),
                   jax.ShapeDtypeStruct((nb, C_KERNEL - 1, w), F32)],
        scratch_shapes=[pltpu.VMEM((CTX_B + tt, w), F32), pltpu.VMEM((CTX_C + tt, w), F32)],
        compiler_params=_cparams(("parallel", "arbitrary")),
        name="conv_branches",
    )(*args, *params)


HW = D_HEADS * D_KEY
SUB = 16


def _split_dot(a, b_bf16):
    hi = a.astype(BF16)
    lo = (a - hi.astype(F32)).astype(BF16)
    return _dot(hi, b_bf16) + _dot(lo, b_bf16)


def _hgrn_kernel(*refs, tb, has_state):
    if has_state:
        zd_ref, s0_ref = refs[:2]
        refs = refs[2:]
    else:
        zd_ref = refs[0]
        refs = refs[1:]
    lb_ref, gn_ref, o_ref, sfin_ref, st_ref, vpad_ref, kpad_ref = refs
    i = pl.program_id(1)
    c = min(SUB, tb)

    @pl.when(i == 0)
    def _():
        st_ref[...] = s0_ref[...] if has_state else jnp.zeros(st_ref.shape, F32)
        if tb < LANES:
            vpad_ref[...] = jnp.zeros(vpad_ref.shape, F32)
            kpad_ref[...] = jnp.zeros(kpad_ref.shape, F32)

    zd = zd_ref[...]
    lb = lb_ref[...]
    q = jax.nn.silu(zd[:, 0:HW])
    f = lb + (1.0 - lb) * jax.nn.sigmoid(zd[:, HW:2 * HW])
    logf = jnp.log(f)
    kk = 1.0 - f
    v = zd[:, 2 * HW:3 * HW]
    gate = jax.nn.silu(zd[:, 3 * HW:4 * HW])

    row = lax.broadcasted_iota(I32, (tb, tb), 0)
    col = lax.broadcasted_iota(I32, (tb, tb), 1)
    tril_sub = ((row // c == col // c) & (col <= row)).astype(F32)
    g = _dot(tril_sub, logf, precision=HI)

    hrow = lax.broadcasted_iota(I32, (HW, HW), 0) // D_KEY
    hcol = lax.broadcasted_iota(I32, (HW, HW), 1) // D_KEY
    same_head = hrow == hcol
    head_ones = same_head.astype(BF16)

    if tb < LANES:
        vpad_ref[0:tb, :] = v
        v_t = vpad_ref[...].T
    else:
        v_t = v.T
    v_t = v_t.astype(BF16)
    tok = lax.broadcasted_iota(I32, (tb, HW), 0)
    s_idx = lax.broadcasted_iota(I32, (c, HW), 0)

    outs = []
    st = st_ref[...]
    for j in range(tb // c):
        r0 = j * c
        gj = g[r0:r0 + c]
        qj = q[r0:r0 + c]
        kj = kk[r0:r0 + c]
        vj = v[r0:r0 + c]
        g_last = gj[c - 1:c]
        o_inter = _dot((qj * jnp.exp(gj)).astype(BF16), st.astype(BF16), NT_DIMS)

        pieces = []
        for t in range(c):
            mask = s_idx <= t
            e = jnp.exp(jnp.where(mask, gj[t:t + 1] - gj, 0.0))
            pieces.append(jnp.where(mask, qj[t:t + 1] * kj * e, 0.0))
        att = _split_dot(jnp.concatenate(pieces, axis=0), head_ones)
        o_intra = jnp.sum(att.reshape(c, c, HW) * vj[None, :, :], axis=1)
        o = o_inter + o_intra
        ms = _split_dot(o * o, head_ones) * (1.0 / D_VAL)
        outs.append(o * lax.rsqrt(ms + EPS) * gn_ref[...] * gate[r0:r0 + c])

        in_sub = (tok >= r0) & (tok < r0 + c)
        kd = jnp.where(in_sub, kk * jnp.exp(jnp.where(in_sub, g_last - g, 0.0)), 0.0)
        if tb < LANES:
            kpad_ref[0:tb, :] = kd
            kd = kpad_ref[...]
        upd = _dot(v_t, kd.astype(BF16))
        st = st * jnp.exp(g_last) + jnp.where(same_head, upd, 0.0)

    st_ref[...] = st
    o_ref[...] = jnp.concatenate(outs, axis=0).astype(o_ref.dtype) if len(outs) > 1 else outs[0].astype(o_ref.dtype)

    @pl.when(i == pl.num_programs(1) - 1)
    def _():
        sfin_ref[...] = st_ref[...]


def hgrn(z_all, row0, nb, t, tb, s0_t, lb, gn):
    nt = t // tb
    r0 = row0 // tb
    has_state = s0_t is not None
    in_specs = [pl.BlockSpec((tb, 4 * HW), lambda b, i: (r0 + b * nt + i, ZD // (4 * HW)))]
    args = [z_all]
    if has_state:
        in_specs.append(pl.BlockSpec((None, HW, HW), lambda b, i: (b, 0, 0)))
        args.append(s0_t)
    in_specs += [pl.BlockSpec((1, HW), lambda b, i: (0, 0)), pl.BlockSpec((1, HW), lambda b, i: (0, 0))]
    args += [lb.reshape(1, HW), jnp.tile(gn, D_HEADS).reshape(1, HW)]
    pad_rows = LANES if tb < LANES else 8
    return pl.pallas_call(
        functools.partial(_hgrn_kernel, tb=tb, has_state=has_state),
        grid=(nb, nt),
        in_specs=in_specs,
        out_specs=[pl.BlockSpec((tb, HW), lambda b, i: (b * nt + i, 0)),
                   pl.BlockSpec((None, HW, HW), lambda b, i: (b, 0, 0))],
        out_shape=[jax.ShapeDtypeStruct((nb * t, HW), BF16), jax.ShapeDtypeStruct((nb, HW, HW), F32)],
        scratch_shapes=[pltpu.VMEM((HW, HW), F32), pltpu.VMEM((pad_rows, HW), F32), pltpu.VMEM((pad_rows, HW), F32)],
        compiler_params=_cparams(("parallel", "arbitrary")),
        name="hgrn",
    )(*args)


def _merge_kernel(oa_ref, obc_ref, od_ref, ga_ref, gb_ref, gc_ref, gd_ref, wa_ref, wb_ref, wc_ref, wd_ref, wo_ref, x_ref, o_ref):
    obc = obc_ref[...]
    merged = jax.nn.sigmoid(ga_ref[...]) * _dot(oa_ref[...], wa_ref[...])
    merged += jax.nn.sigmoid(gb_ref[...]) * _dot(obc[:, :BRANCH_WIDTH], wb_ref[...])
    merged += jax.nn.sigmoid(gc_ref[...]) * _dot(obc[:, BRANCH_WIDTH:], wc_ref[...])
    merged += jax.nn.sigmoid(gd_ref[...]) * _dot(od_ref[...], wd_ref[...])
    o_ref[...] = x_ref[...] + _dot(merged.astype(BF16), wo_ref[...])


def merge(o_a, o_bc, o_d, z_all, wa_p, wb, wc, wd, w_out, x, tm=512):
    n = x.shape[0]
    row = lambda w: pl.BlockSpec((tm, w), lambda i: (i, 0))
    full = lambda a: pl.BlockSpec(a.shape, lambda i: (0, 0))
    gate = lambda k: pl.BlockSpec((tm, D_MODEL), lambda i: (i, ZG // D_MODEL + k))
    return pl.pallas_call(
        _merge_kernel,
        grid=(n // tm,),
        in_specs=[row(o_a.shape[1]), row(o_bc.shape[1]), row(o_d.shape[1]), gate(0), gate(1), gate(2), gate(3),
                  full(wa_p), full(wb), full(wc), full(wd), full(w_out), row(D_MODEL)],
        out_specs=row(D_MODEL),
        out_shape=jax.ShapeDtypeStruct((n, D_MODEL), F32),
        compiler_params=_cparams(("parallel",)),
        name="merge",
    )(o_a, o_bc, o_d, z_all, z_all, z_all, z_all, wa_p, wb, wc, wd, w_out, x)


HP = P_HEADS * 2
SLOTS = P_HEADS * P_TOPK
SUBK = P_DKEY // 2
STAGE1_UNROLL = 8


def _take_max(s, index, n):
    m = jnp.max(s, axis=0, keepdims=True)
    idx = jnp.min(jnp.where(s == m, index, n), axis=0, keepdims=True)
    return m, idx, jnp.where(index == idx, -jnp.inf, s)


_CAND_BLOCKS = [(0, P_TOPK)] + [(a, 8) for a in range(1, 8)]
_CAND_ROWS = sum(nb for _, nb in _CAND_BLOCKS) + 8


def _route_kernel(x_ref, g_ref, wq_ref, keys_ref, h_ref, ii_ref, jj_ref, gw_ref, q_s, sv_s, si_s, oi_s, oj_s, og_s, *, tm):
    x = x_ref[...]
    h = (x * lax.rsqrt(jnp.mean(x * x, axis=-1, keepdims=True) + EPS) * g_ref[...]).astype(BF16)
    h_ref[...] = h
    q_s[...] = _dot(h, wq_ref[...]).astype(BF16)

    half = P_NKEYS // 2
    n_lt = tm // LANES
    iota_lo = lax.broadcasted_iota(I32, (half, LANES), 0).astype(F32)
    iota_hi = iota_lo + float(half)

    def stage1(it, carry):
        hp = it // n_lt
        toks = pl.ds(pl.multiple_of((it % n_lt) * LANES, LANES), LANES)
        q = q_s[toks, pl.ds(pl.multiple_of(hp * SUBK, SUBK), SUBK)]
        s = _dot(keys_ref[hp], q, NT_DIMS)
        a, b = s[:half], s[half:]
        first = a >= b
        cur, cur_i = jnp.where(first, a, b), jnp.where(first, iota_lo, iota_hi)
        nxt, nxt_i = jnp.where(first, b, a), jnp.where(first, iota_hi, iota_lo)
        vals, idxs = [], []
        for _ in range(P_TOPK):
            m = jnp.max(cur, axis=0, keepdims=True)
            idx = jnp.min(jnp.where(cur == m, cur_i, float(P_NKEYS)), axis=0, keepdims=True)
            hit = cur_i == idx
            cur, cur_i, nxt = jnp.where(hit, nxt, cur), jnp.where(hit, nxt_i, cur_i), jnp.where(hit, -jnp.inf, nxt)
            vals.append(m)
            idxs.append(idx)
        sv_s[hp, :, toks] = jnp.concatenate(vals, axis=0)
        si_s[hp, :, toks] = jnp.concatenate(idxs, axis=0)
        return carry

    lax.fori_loop(0, HP * n_lt, stage1, 0, unroll=STAGE1_UNROLL)

    r = lax.broadcasted_iota(I32, (_CAND_ROWS, tm), 0)
    mid = r - P_TOPK
    flat = jnp.where(r < P_TOPK, r, jnp.where(r < _CAND_ROWS - 8, (1 + mid // 8) * P_TOPK + mid % 8, (r - (_CAND_ROWS - 16)) * P_TOPK))
    flat = flat.astype(F32)
    iota_t = lax.broadcasted_iota(I32, (P_TOPK, tm), 0).astype(F32)

    def stage2(hd, carry):
        sv1, sv2 = sv_s[2 * hd], sv_s[2 * hd + 1]
        si1, si2 = si_s[2 * hd], si_s[2 * hd + 1]
        cand = jnp.concatenate([sv1[a:a + 1] + sv2[0:nb] for a, nb in _CAND_BLOCKS] + [sv1[8:P_TOPK] + sv2[0:1]], axis=0)
        fv, ei, ej = [], [], []
        for _ in range(P_TOPK):
            m, idx, cand = _take_max(cand, flat, float(P_TOPK * P_TOPK))
            a = jnp.floor(idx * (1.0 / P_TOPK))
            b = idx - a * P_TOPK
            fv.append(m)
            ei.append(jnp.sum(jnp.where(iota_t == a, si1, 0.0), axis=0, keepdims=True))
            ej.append(jnp.sum(jnp.where(iota_t == b, si2, 0.0), axis=0, keepdims=True))
        fv = jnp.concatenate(fv, axis=0)
        e = jnp.exp(fv - fv[0:1])
        rows = pl.ds(pl.multiple_of(hd * P_TOPK, P_TOPK), P_TOPK)
        og_s[rows, :] = e / jnp.sum(e, axis=0, keepdims=True)
        oi_s[rows, :] = jnp.concatenate(ei, axis=0)
        oj_s[rows, :] = jnp.concatenate(ej, axis=0)
        return carry

    lax.fori_loop(0, P_HEADS, stage2, 0, unroll=2)
    ii_ref[...] = oi_s[...].T.astype(I32)
    jj_ref[...] = oj_s[...].T.astype(I32)
    gw_ref[...] = og_s[...].T


def peer_route(x, g, wq, keys, tm=256):
    n = x.shape[0]
    row = lambda w: pl.BlockSpec((tm, w), lambda i: (i, 0))
    return pl.pallas_call(
        functools.partial(_route_kernel, tm=tm),
        grid=(n // tm,),
        in_specs=[row(D_MODEL), pl.BlockSpec((1, D_MODEL), lambda i: (0, 0)),
                  pl.BlockSpec(wq.shape, lambda i: (0, 0)), pl.BlockSpec(keys.shape, lambda i: (0, 0, 0))],
        out_specs=[row(D_MODEL), row(SLOTS), row(SLOTS), row(SLOTS)],
        out_shape=[jax.ShapeDtypeStruct((n, D_MODEL), BF16), jax.ShapeDtypeStruct((n, SLOTS), I32),
                   jax.ShapeDtypeStruct((n, SLOTS), I32), jax.ShapeDtypeStruct((n, SLOTS), F32)],
        scratch_shapes=[pltpu.VMEM((tm, HP * SUBK), BF16), pltpu.VMEM((HP, P_TOPK, tm), F32), pltpu.VMEM((HP, P_TOPK, tm), F32),
                        pltpu.VMEM((SLOTS, tm), F32), pltpu.VMEM((SLOTS, tm), F32), pltpu.VMEM((SLOTS, tm), F32)],
        compiler_params=_cparams(("parallel",)),
        name="peer_route",
    )(x, g.reshape(1, D_MODEL), wq, keys)


CHUNK_I = 16
CHUNK_E = CHUNK_I * P_NKEYS
N_CHUNKS = P_NKEYS // CHUNK_I


def _peer_kernel(h_ref, ii_ref, jj_ref, gw_ref, u_ref, v_ref, x_ref, o_ref, w_s, acc_s, *, tm):
    c = pl.program_id(1)

    @pl.when(c == 0)
    def _():
        acc_s[...] = jnp.zeros(acc_s.shape, F32)
        iota = lax.broadcasted_iota(I32, (P_NKEYS, SLOTS), 0)

        def build(g, carry):
            t0 = pl.multiple_of(g * PACK, PACK)
            halves = []
            for half in range(PACK // SUBLANES):
                ws = []
                for t in range(half * SUBLANES, (half + 1) * SUBLANES):
                    irow = ii_ref[pl.ds(t0 + t, 1), :]
                    jrow = jj_ref[pl.ds(t0 + t, 1), :]
                    grow = gw_ref[pl.ds(t0 + t, 1), :]
                    p_t = jnp.where(iota == irow, grow, 0.0).astype(BF16)
                    q_t = jnp.where(iota == jrow, 1.0, 0.0).astype(BF16)
                    ws.append(_dot(p_t, q_t, NT_DIMS))
                halves.append(pltpu.einshape("tij->itj", jnp.stack(ws, axis=0)))
            w_s[:, pl.ds(t0, PACK), :] = jnp.concatenate(halves, axis=1).astype(BF16)
            return carry

        lax.fori_loop(0, tm // PACK, build, 0)

    a = _dot(h_ref[...], u_ref[...], NT_DIMS)
    act = 0.5 * a * (1.0 + lax.erf(a * (1.0 / math.sqrt(2.0))))
    wd = jnp.concatenate([w_s[c * CHUNK_I + r] for r in range(CHUNK_I)], axis=1)
    acc_s[...] += _dot(act.astype(BF16) * wd, v_ref[...])

    @pl.when(c == N_CHUNKS - 1)
    def _():
        o_ref[...] = x_ref[...] + acc_s[...]


def peer_experts(h2, ii, jj, gw, u_tab, v_tab, x, tm=512):
    n = x.shape[0]
    row = lambda w: pl.BlockSpec((tm, w), lambda i, c: (i, 0))
    return pl.pallas_call(
        functools.partial(_peer_kernel, tm=tm),
        grid=(n // tm, N_CHUNKS),
        in_specs=[row(D_MODEL), row(SLOTS), row(SLOTS), row(SLOTS),
                  pl.BlockSpec((CHUNK_E, D_MODEL), lambda i, c: (c, 0)),
                  pl.BlockSpec((CHUNK_E, D_MODEL), lambda i, c: (c, 0)),
                  row(D_MODEL)],
        out_specs=row(D_MODEL),
        out_shape=jax.ShapeDtypeStruct((n, D_MODEL), F32),
        scratch_shapes=[pltpu.VMEM((P_NKEYS, tm, P_NKEYS), BF16), pltpu.VMEM((tm, D_MODEL), F32)],
        compiler_params=_cparams(("parallel", "arbitrary")),
        name="peer_experts",
    )(h2, ii, jj, gw, u_tab, v_tab, x)


def _rmsnorm_kernel(x_ref, g_ref, o_ref):
    x = x_ref[...]
    o_ref[...] = x * lax.rsqrt(jnp.mean(x * x, axis=-1, keepdims=True) + EPS) * g_ref[...]


def rmsnorm(x, g, tm=512):
    n, d = x.shape
    return pl.pallas_call(
        _rmsnorm_kernel,
        grid=(n // tm,),
        in_specs=[pl.BlockSpec((tm, d), lambda i: (i, 0)), pl.BlockSpec((1, d), lambda i: (0, 0))],
        out_specs=pl.BlockSpec((tm, d), lambda i: (i, 0)),
        out_shape=jax.ShapeDtypeStruct((n, d), F32),
        compiler_params=_cparams(("parallel",)),
        name="rmsnorm",
    )(x, g.reshape(1, d))


def _rope_tables():
    half = A_ROPE // 2
    inv = ROPE_THETA ** (-jnp.arange(half, dtype=F32) / half)
    pos = jnp.concatenate([jnp.tile(jnp.arange(SEQ), BATCH), jnp.tile(PAST_LEN + jnp.arange(DEC_SEQ), DEC_BATCH)])
    ang = pos.astype(F32)[:, None] * inv[None, :]
    cos, sin = jnp.cos(ang), jnp.sin(ang)
    zeros = lambda w: jnp.zeros((NT, w), F32)
    ccq = jnp.concatenate([jnp.ones((NT, A_NOPE), F32), cos, cos, zeros(HEAD_PAD - A_NOPE - A_ROPE)], axis=1)
    ssq = jnp.concatenate([zeros(A_NOPE), -sin, sin, zeros(HEAD_PAD - A_NOPE - A_ROPE)], axis=1)
    cck = jnp.concatenate([cos, cos, zeros(LANES - A_ROPE)], axis=1)
    ssk = jnp.concatenate([-sin, sin, zeros(LANES - A_ROPE)], axis=1)
    return ccq, ssq, cck, ssk


def _swap_halves(w):
    half = w.shape[-1] // 2
    return jnp.concatenate([w[..., half:], w[..., :half]], axis=-1)


def _layer_weights(l, w_in, a_w_uq, a_w_uk, a_w_uv, w_branch, w_out, p_w_q, p_sub_keys, p_u, p_v):
    w = w_in[l]
    o = 0
    parts = []
    for size in (A_Q_LORA, A_KV_LORA, A_ROPE, 2 * BRANCH_WIDTH, 3 * BRANCH_WIDTH, 4 * HW, N_BRANCH * D_MODEL):
        parts.append(w[:, o:o + size])
        o += size
    cq, ckv, kr, b_in, c_in, d_in, gate = parts
    pad = jnp.zeros((D_MODEL, ZB - A_Q_LORA - A_KV_LORA - 2 * A_ROPE), F32)
    w_in_p = jnp.concatenate([cq, ckv, kr, _swap_halves(kr), pad, b_in, d_in, gate, c_in], axis=1).astype(BF16)

    wq = a_w_uq[l].reshape(A_Q_LORA, A_HEADS, A_NOPE + A_ROPE)
    nope, rope = wq[..., :A_NOPE], wq[..., A_NOPE:]
    tail = jnp.zeros((A_Q_LORA, A_HEADS, HEAD_PAD - A_NOPE - A_ROPE), F32)
    wq_main = jnp.concatenate([nope, rope, tail], axis=-1).reshape(A_Q_LORA, -1).astype(BF16)
    wq_swap = jnp.concatenate([jnp.zeros_like(nope), _swap_halves(rope), tail], axis=-1).reshape(A_Q_LORA, -1).astype(BF16)

    w_uk, w_uv = a_w_uk[l], a_w_uv[l]
    head_tail = jnp.zeros((A_KV_LORA, A_HEADS, HEAD_PAD - A_NOPE), F32)
    wuk_p = jnp.concatenate([w_uk, head_tail], axis=-1).reshape(A_KV_LORA, -1).astype(BF16)
    wuv_p = jnp.concatenate([w_uv, head_tail], axis=-1).reshape(A_KV_LORA, -1).astype(BF16)

    r = jnp.arange(LANES)[:, None]
    col = jnp.arange(A_HEADS * HEAD_PAD)[None, :]
    place = ((col % HEAD_PAD == A_NOPE + r) & (r < A_ROPE)).astype(BF16)

    blk = jnp.zeros((A_HEADS, HEAD_PAD, 2 * LANES), F32)
    blk = blk.at[:, :A_NOPE, :A_KV_LORA].set(jnp.transpose(w_uk, (1, 2, 0)))
    blk = blk.at[:, A_NOPE:A_NOPE + A_ROPE, A_KV_LORA:A_KV_LORA + A_ROPE].set(jnp.eye(A_ROPE, dtype=F32))
    eye_h = jnp.eye(A_HEADS, dtype=F32)
    wabs = (blk[:, :, None, :] * eye_h[:, None, :, None]).reshape(A_HEADS * HEAD_PAD, A_HEADS * 2 * LANES).astype(BF16)

    wb = w_branch[l]
    wa_p = jnp.concatenate([wb[0].reshape(A_HEADS, A_V, D_MODEL), jnp.zeros((A_HEADS, HEAD_PAD - A_V, D_MODEL), F32)],
                           axis=1).reshape(A_HEADS * HEAD_PAD, D_MODEL).astype(BF16)
    return dict(w_in_p=w_in_p, wq_main=wq_main, wq_swap=wq_swap, wuk_p=wuk_p, wuv_p=wuv_p, place=place, wabs=wabs,
                wa_p=wa_p, wb=wb[1].astype(BF16), wc=wb[2].astype(BF16), wd=wb[3].astype(BF16), w_out=w_out[l].astype(BF16),
                wq=p_w_q[l].astype(BF16), keys=p_sub_keys[l].reshape(HP, P_NKEYS, SUBK).astype(BF16),
                u=p_u[l].astype(BF16), v=p_v[l].astype(BF16))


def _state_to_kernel(s):
    t = jnp.transpose(s, (0, 1, 3, 2))
    eye_h = jnp.eye(D_HEADS, dtype=F32)
    return (t[:, :, :, None, :] * eye_h[None, :, None, :, None]).reshape(s.shape[0], HW, HW)


def _state_from_kernel(s_t):
    s5 = s_t.reshape(s_t.shape[0], D_HEADS, D_VAL, D_HEADS, D_KEY)
    diag = jnp.stack([s5[:, h, :, h, :] for h in range(D_HEADS)], axis=1)
    return jnp.transpose(diag, (0, 1, 3, 2))


def kernel(x_prompt, x_sample, cache_ckv, cache_krope, page_table, state_conv_b, state_conv_c, state_hgrn, norm1_g, w_in, a_q_norm_g, a_w_uq, a_kv_norm_g, a_w_uk, a_w_uv, b_conv_w, b_conv_b, b_ln_g, b_ln_b, c_conv_w, d_lower_bound, d_gnorm_g, w_branch, w_out, norm2_g, p_w_q, p_sub_keys, p_u, p_v, final_norm_g):
    lb_soft = jax.nn.softmax(d_lower_bound.astype(F32), axis=0)
    lower_bounds = jnp.cumsum(lb_soft, axis=0) - lb_soft[0:1]
    ccq, ssq, cck, ssk = _rope_tables()
    x = jnp.concatenate([x_prompt.reshape(NP, D_MODEL), x_sample.reshape(NS, D_MODEL)], axis=0)
    cache_krope_t = jnp.swapaxes(cache_krope, 2, 3)

    states_p, states_s = [], []
    for l in range(DEPTH):
        w = _layer_weights(l, w_in, a_w_uq, a_w_uk, a_w_uv, w_branch, w_out, p_w_q, p_sub_keys, p_u, p_v)
        z = norm_matmul(x, norm1_g[l], w["w_in_p"], 512, 1152)

        ckv, kr, qp, kp, vp = mla_prep(z, a_q_norm_g[l], a_kv_norm_g[l], w["wq_main"], w["wq_swap"], w["wuk_p"], w["wuv_p"],
                                       w["place"], ccq, ssq, cck, ssk)
        oa_p = flash_prompt(qp, kp, vp)
        qlat = matmul(qp[NP:], w["wabs"], 512, BF16)
        qlat = jnp.transpose(qlat.reshape(DEC_BATCH, DEC_SEQ, A_HEADS, 2 * LANES), (0, 2, 1, 3)).reshape(DEC_BATCH, DEC_ROWS, 2 * LANES)
        oa_s = mla_decode(l, page_table, qlat, cache_ckv, cache_krope_t, ckv[NP:].reshape(DEC_BATCH, DEC_SEQ, LANES),
                          kr[NP:].reshape(DEC_BATCH, DEC_SEQ, LANES), w["wuv_p"])
        o_a = jnp.concatenate([oa_p, oa_s.reshape(NS, -1)], axis=0)

        conv_w = (b_conv_w[l], b_conv_b[l], b_ln_g[l], b_ln_b[l], c_conv_w[l])
        obc_p, hb_p, hc_p = conv_branches(z, 0, BATCH, SEQ, 512, None, None, *conv_w)
        obc_s, hb_s, hc_s = conv_branches(z, NP, DEC_BATCH, DEC_SEQ, DEC_SEQ, state_conv_b[l], state_conv_c[l], *conv_w)
        o_bc = jnp.concatenate([obc_p, obc_s], axis=0)

        od_p, st_p = hgrn(z, 0, BATCH, SEQ, LANES, None, lower_bounds[l], d_gnorm_g[l])
        od_s, st_s = hgrn(z, NP, DEC_BATCH, DEC_SEQ, DEC_SEQ, _state_to_kernel(state_hgrn[l]), lower_bounds[l], d_gnorm_g[l])
        o_d = jnp.concatenate([od_p, od_s], axis=0)

        x1 = merge(o_a, o_bc, o_d, z, w["wa_p"], w["wb"], w["wc"], w["wd"], w["w_out"], x)
        h2, ii, jj, gw = peer_route(x1, norm2_g[l], w["wq"], w["keys"])
        x = peer_experts(h2, ii, jj, gw, w["u"], w["v"], x1)

        states_p.append((ckv[:NP].reshape(BATCH, SEQ, A_KV_LORA), kr[:NP, :A_ROPE].reshape(BATCH, SEQ, A_ROPE),
                         hb_p, hc_p, _state_from_kernel(st_p)))
        states_s.append((ckv[NP:].reshape(DEC_BATCH, DEC_SEQ, A_KV_LORA), kr[NP:, :A_ROPE].reshape(DEC_BATCH, DEC_SEQ, A_ROPE),
                         hb_s, hc_s, _state_from_kernel(st_s)))

    y = rmsnorm(x, final_norm_g)
    stack = lambda states: [jnp.stack([s[i] for s in states], axis=0) for i in range(5)]
    return (y[:NP].reshape(BATCH, SEQ, D_MODEL), y[NP:].reshape(DEC_BATCH, DEC_SEQ, D_MODEL), *stack(states_p), *stack(states_s))
```

```python
import functools
import math

import jax
import jax.numpy as jnp
from jax import lax
from jax.experimental import pallas as pl
from jax.experimental.pallas import tpu as pltpu

F32 = jnp.float32
BF16 = jnp.bfloat16
I32 = jnp.int32

D_MODEL = 1024
BATCH = 2
SEQ = 8192
DEPTH = 2
DEC_BATCH = 128
DEC_SEQ = 8
PAST_LEN = 8192
PAGE_SIZE = 128
N_PAGES = PAST_LEN // PAGE_SIZE
BRANCH_WIDTH = 256
N_BRANCH = 4
A_HEADS = 4
A_NOPE = 64
A_ROPE = 32
A_V = 64
A_Q_LORA = 256
A_KV_LORA = 128
ROPE_THETA = 10000.0
B_KERNEL = 31
C_KERNEL = 3
D_HEADS = 4
D_KEY = 64
D_VAL = 64
P_HEADS = 8
P_NKEYS = 128
P_DKEY = 256
P_TOPK = 16
P_EXPERTS = P_NKEYS * P_NKEYS
EPS = 1e-6
NEG_BIG = -1e30

NP = BATCH * SEQ
NS = DEC_BATCH * DEC_SEQ
NT = NP + NS

LANES = 128
SUBLANES = 8
PACK = 16
HEAD_PAD = 128
ZA, ZB, ZD, ZG, ZC = 0, 512, 1024, 2048, 6144
Z_COLS = 6912
VMEM_LIMIT = 56 * 1024 * 1024

HI = lax.Precision.HIGHEST


def _cparams(sem):
    return pltpu.CompilerParams(dimension_semantics=sem, vmem_limit_bytes=VMEM_LIMIT)


def _dot(a, b, dims=(((1,), (0,)), ((), ())), precision=None):
    return lax.dot_general(a, b, dims, precision=precision, preferred_element_type=F32)


NT_DIMS = (((1,), (1,)), ((), ()))
TN_DIMS = (((0,), (0,)), ((), ()))


def _norm_matmul_kernel(x_ref, g_ref, w_ref, o_ref, h_ref):
    @pl.when(pl.program_id(1) == 0)
    def _():
        x = x_ref[...]
        y = x * lax.rsqrt(jnp.mean(x * x, axis=-1, keepdims=True) + EPS)
        h_ref[...] = (y * g_ref[...]).astype(BF16)

    o_ref[...] = _dot(h_ref[...], w_ref[...])


def norm_matmul(x, g, w, tm, tn):
    n, d = x.shape
    cols = w.shape[1]
    return pl.pallas_call(
        _norm_matmul_kernel,
        grid=(n // tm, cols // tn),
        in_specs=[
            pl.BlockSpec((tm, d), lambda i, j: (i, 0)),
            pl.BlockSpec((1, d), lambda i, j: (0, 0)),
            pl.BlockSpec((d, tn), lambda i, j: (0, j)),
        ],
        out_specs=pl.BlockSpec((tm, tn), lambda i, j: (i, j)),
        out_shape=jax.ShapeDtypeStruct((n, cols), F32),
        scratch_shapes=[pltpu.VMEM((tm, d), BF16)],
        compiler_params=_cparams(("parallel", "arbitrary")),
        name="norm_matmul",
    )(x, g.reshape(1, d), w)


def _matmul_kernel(x_ref, w_ref, o_ref):
    o_ref[...] = _dot(x_ref[...], w_ref[...]).astype(o_ref.dtype)


def matmul(x, w, tm, out_dtype):
    n, d = x.shape
    cols = w.shape[1]
    return pl.pallas_call(
        _matmul_kernel,
        grid=(n // tm,),
        in_specs=[pl.BlockSpec((tm, d), lambda i: (i, 0)), pl.BlockSpec((d, cols), lambda i: (0, 0))],
        out_specs=pl.BlockSpec((tm, cols), lambda i: (i, 0)),
        out_shape=jax.ShapeDtypeStruct((n, cols), out_dtype),
        compiler_params=_cparams(("parallel",)),
        name="matmul",
    )(x, w)


def _tile4(t):
    return jnp.concatenate([t, t, t, t], axis=1)


def _mla_prep_kernel(z_ref, qn_ref, kvn_ref, wqm_ref, wqs_ref, wuk_ref, wuv_ref, place_ref,
                     ccq_ref, ssq_ref, cck_ref, ssk_ref,
                     ckv_ref, kr_ref, q_ref, k_ref, v_ref):
    z = z_ref[...]
    cq = z[:, 0:A_Q_LORA]
    cqn = cq * lax.rsqrt(jnp.mean(cq * cq, axis=-1, keepdims=True) + EPS) * qn_ref[...]
    cqn = cqn.astype(BF16)
    scale = 1.0 / math.sqrt(A_NOPE + A_ROPE)
    q = _dot(cqn, wqm_ref[...]) * _tile4(ccq_ref[...]) + _dot(cqn, wqs_ref[...]) * _tile4(ssq_ref[...])
    q_ref[...] = (q * scale).astype(BF16)

    c = z[:, A_Q_LORA:A_Q_LORA + A_KV_LORA]
    ckv = c * lax.rsqrt(jnp.mean(c * c, axis=-1, keepdims=True) + EPS) * kvn_ref[...]
    ckv_ref[...] = ckv
    ckv_b = ckv.astype(BF16)

    kc = z[:, A_Q_LORA + A_KV_LORA:]
    kr = kc * cck_ref[...] + pltpu.roll(kc, LANES - A_ROPE, axis=1) * ssk_ref[...]
    kr_ref[...] = kr

    k = _dot(ckv_b, wuk_ref[...]) + _dot(kr.astype(BF16), place_ref[...])
    k_ref[...] = k.astype(BF16)
    v_ref[...] = _dot(ckv_b, wuv_ref[...]).astype(BF16)


def mla_prep(z_all, qn_g, kvn_g, wq_main, wq_swap, wuk_p, wuv_p, place, ccq, ssq, cck, ssk, tm=512):
    n = z_all.shape[0]
    hp = A_HEADS * HEAD_PAD
    row = lambda w: pl.BlockSpec((tm, w), lambda i: (i, 0))
    full = lambda a: pl.BlockSpec(a.shape, lambda i: (0,) * a.ndim)
    qn_g = qn_g.reshape(1, -1)
    kvn_g = kvn_g.reshape(1, -1)
    return pl.pallas_call(
        _mla_prep_kernel,
        grid=(n // tm,),
        in_specs=[row(512), full(qn_g), full(kvn_g), full(wq_main), full(wq_swap), full(wuk_p), full(wuv_p), full(place),
                  row(LANES), row(LANES), row(LANES), row(LANES)],
        out_specs=[row(LANES), row(LANES), row(hp), row(hp), row(hp)],
        out_shape=[jax.ShapeDtypeStruct((n, LANES), F32), jax.ShapeDtypeStruct((n, LANES), F32),
                   jax.ShapeDtypeStruct((n, hp), BF16), jax.ShapeDtypeStruct((n, hp), BF16),
                   jax.ShapeDtypeStruct((n, hp), BF16)],
        compiler_params=_cparams(("parallel",)),
        name="mla_prep",
    )(z_all, qn_g, kvn_g, wq_main, wq_swap, wuk_p, wuv_p, place, ccq, ssq, cck, ssk)


def _flash_kernel(q_ref, k_ref, v_ref, o_ref, *, tq, tk):
    i = pl.program_id(2)
    q = q_ref[...]

    def step(j, carry, masked):
        m, l, acc = carry
        kj = k_ref[pl.ds(pl.multiple_of(j * tk, tk), tk), :]
        vj = v_ref[pl.ds(pl.multiple_of(j * tk, tk), tk), :]
        s = _dot(q, kj, NT_DIMS)
        if masked:
            qpos = i * tq + lax.broadcasted_iota(I32, (tq, tk), 0)
            kpos = j * tk + lax.broadcasted_iota(I32, (tq, tk), 1)
            s = jnp.where(kpos <= qpos, s, NEG_BIG)
        m_new = jnp.maximum(m, jnp.max(s, axis=-1, keepdims=True))
        alpha = jnp.exp(m - m_new)
        p = jnp.exp(s - m_new)
        l = alpha * l + jnp.sum(p, axis=-1, keepdims=True)
        acc = alpha * acc + _dot(p.astype(BF16), vj)
        return m_new, l, acc

    init = (jnp.full((tq, 1), NEG_BIG, F32), jnp.zeros((tq, 1), F32), jnp.zeros((tq, HEAD_PAD), F32))
    n_full = (i * tq) // tk
    carry = lax.fori_loop(0, n_full // 2, lambda p, c: step(2 * p + 1, step(2 * p, c, False), False), init)
    carry = lax.fori_loop(0, n_full % 2, lambda _, c: step(n_full - 1, c, False), carry)
    for d in range(tq // tk):
        carry = step(n_full + d, carry, True)
    m, l, acc = carry
    o_ref[...] = (acc / l).astype(o_ref.dtype)


def flash_prompt(qp, kp, vp, tq=1024, tk=1024):
    assert tq % tk == 0
    nq = SEQ // tq
    return pl.pallas_call(
        functools.partial(_flash_kernel, tq=tq, tk=tk),
        grid=(BATCH, A_HEADS, nq),
        in_specs=[
            pl.BlockSpec((tq, HEAD_PAD), lambda b, h, i: (b * nq + i, h)),
            pl.BlockSpec((SEQ, HEAD_PAD), lambda b, h, i: (b, h)),
            pl.BlockSpec((SEQ, HEAD_PAD), lambda b, h, i: (b, h)),
        ],
        out_specs=pl.BlockSpec((tq, HEAD_PAD), lambda b, h, i: (b * nq + i, h)),
        out_shape=jax.ShapeDtypeStruct((NP, A_HEADS * HEAD_PAD), BF16),
        compiler_params=_cparams(("parallel", "parallel", "arbitrary")),
        name="flash_prompt",
    )(qp, kp, vp)


DEC_ROWS = A_HEADS * DEC_SEQ
PAGES_PER_STEP = 32


def _decode_kernel(pt_ref, q_ref, *refs):
    del pt_ref
    pg = PAGES_PER_STEP
    ckv_refs = refs[:pg]
    kr_refs = refs[pg:2 * pg]
    cnew_ref, knew_ref, wuv_ref, o_ref, m_ref, l_ref, acc_ref = refs[2 * pg:]
    g = pl.program_id(1)

    @pl.when(g == 0)
    def _():
        m_ref[...] = jnp.full(m_ref.shape, NEG_BIG, F32)
        l_ref[...] = jnp.zeros(l_ref.shape, F32)
        acc_ref[...] = jnp.zeros(acc_ref.shape, F32)

    q = q_ref[...]
    q_lat = q[:, :A_KV_LORA]
    q_rope = q[:, A_KV_LORA:A_KV_LORA + A_ROPE]

    def update(s, v):
        m_old = m_ref[...]
        m_new = jnp.maximum(m_old, jnp.max(s, axis=-1, keepdims=True))
        alpha = jnp.exp(m_old - m_new)
        p = jnp.exp(s - m_new)
        m_ref[...] = m_new
        l_ref[...] = alpha * l_ref[...] + jnp.sum(p, axis=-1, keepdims=True)
        acc_ref[...] = alpha * acc_ref[...] + _dot(p.astype(BF16), v)

    kc = jnp.concatenate([ckv_refs[r][...].astype(BF16) for r in range(pg)], axis=0)
    kr_t = jnp.concatenate([kr_refs[r][...].astype(BF16) for r in range(pg)], axis=1)
    update(_dot(q_lat, kc, NT_DIMS) + _dot(q_rope, kr_t), kc)

    @pl.when(g == pl.num_programs(1) - 1)
    def _():
        cn = cnew_ref[...].astype(BF16)
        kn = knew_ref[...][:, :A_ROPE].astype(BF16)
        s = _dot(q_lat, cn, NT_DIMS) + _dot(q_rope, kn, NT_DIMS)
        t_q = lax.broadcasted_iota(I32, s.shape, 0) % DEC_SEQ
        t_k = lax.broadcasted_iota(I32, s.shape, 1)
        s = jnp.where(t_k <= t_q, s, NEG_BIG)
        update(s, cn)
        o_lat = (acc_ref[...] / l_ref[...]).astype(BF16)
        for h in range(A_HEADS):
            o_ref[:, h * HEAD_PAD:(h + 1) * HEAD_PAD] = _dot(
                o_lat[h * DEC_SEQ:(h + 1) * DEC_SEQ, :], wuv_ref[:, h * HEAD_PAD:(h + 1) * HEAD_PAD]).astype(o_ref.dtype)


def mla_decode(layer, page_table, qlat, cache_ckv, cache_krope_t, ckv_new, kr_new, wuv_p):
    pg = PAGES_PER_STEP

    def page_spec(rows, cols, r):
        return pl.BlockSpec((None, None, rows, cols), lambda b, g, pt: (layer, pt[b, g * pg + r], 0, 0))

    in_specs = [pl.BlockSpec((None, DEC_ROWS, 2 * LANES), lambda b, g, pt: (b, 0, 0))]
    in_specs += [page_spec(PAGE_SIZE, A_KV_LORA, r) for r in range(pg)]
    in_specs += [page_spec(A_ROPE, PAGE_SIZE, r) for r in range(pg)]
    in_specs += [
        pl.BlockSpec((None, DEC_SEQ, LANES), lambda b, g, pt: (b, 0, 0)),
        pl.BlockSpec((None, DEC_SEQ, LANES), lambda b, g, pt: (b, 0, 0)),
        pl.BlockSpec(wuv_p.shape, lambda b, g, pt: (0, 0)),
    ]
    grid_spec = pltpu.PrefetchScalarGridSpec(
        num_scalar_prefetch=1,
        grid=(DEC_BATCH, N_PAGES // pg),
        in_specs=in_specs,
        out_specs=pl.BlockSpec((None, DEC_SEQ, A_HEADS * HEAD_PAD), lambda b, g, pt: (b, 0, 0)),
        scratch_shapes=[pltpu.VMEM((DEC_ROWS, 1), F32), pltpu.VMEM((DEC_ROWS, 1), F32), pltpu.VMEM((DEC_ROWS, A_KV_LORA), F32)],
    )
    return pl.pallas_call(
        _decode_kernel,
        grid_spec=grid_spec,
        out_shape=jax.ShapeDtypeStruct((DEC_BATCH, DEC_SEQ, A_HEADS * HEAD_PAD), BF16),
        compiler_params=_cparams(("parallel", "arbitrary")),
        name="mla_decode",
    )(page_table, qlat, *([cache_ckv] * pg), *([cache_krope_t] * pg), ckv_new, kr_new, wuv_p)


CTX_B = 32
CTX_C = 8


def _conv_kernel(*refs, tt, has_hist):
    if has_hist:
        zb_ref, zc_ref, hb_ref, hc_ref = refs[:4]
        refs = refs[4:]
    else:
        zb_ref, zc_ref = refs[:2]
        refs = refs[2:]
    bw_ref, bb_ref, lg_ref, lb_ref, cw_ref, o_ref, nhb_ref, nhc_ref, xb_ref, xc_ref = refs
    i = pl.program_id(1)
    hb_rows, hc_rows = B_KERNEL - 1, C_KERNEL - 1

    @pl.when(i == 0)
    def _():
        xb_ref[0:CTX_B, :] = jnp.zeros((CTX_B, BRANCH_WIDTH), F32)
        xc_ref[0:CTX_C, :] = jnp.zeros((CTX_C, BRANCH_WIDTH), F32)
        if has_hist:
            xb_ref[CTX_B - hb_rows:CTX_B, :] = hb_ref[...]
            xc_ref[CTX_C - hc_rows:CTX_C, :] = hc_ref[...]

    zb = zb_ref[...]
    xb_ref[CTX_B:CTX_B + tt, :] = zb[:, :BRANCH_WIDTH] * jax.nn.sigmoid(zb[:, BRANCH_WIDTH:])
    acc = jnp.zeros((tt, BRANCH_WIDTH), F32)
    for k in range(B_KERNEL):
        acc = acc + xb_ref[pl.ds(CTX_B - hb_rows + k, tt), :] * bw_ref[k:k + 1, :]
    y = acc + bb_ref[...]
    mu = jnp.mean(y, axis=-1, keepdims=True)
    yc = y - mu
    var = jnp.mean(yc * yc, axis=-1, keepdims=True)
    o_b = jax.nn.silu(yc * lax.rsqrt(var + EPS) * lg_ref[...] + lb_ref[...])

    zc = zc_ref[...]
    gb = zc[:, :BRANCH_WIDTH]
    xc_ref[CTX_C:CTX_C + tt, :] = zc[:, BRANCH_WIDTH:2 * BRANCH_WIDTH] * zc[:, 2 * BRANCH_WIDTH:]
    acc_c = jnp.zeros((tt, BRANCH_WIDTH), F32)
    for k in range(C_KERNEL):
        acc_c = acc_c + xc_ref[pl.ds(CTX_C - hc_rows + k, tt), :] * cw_ref[k:k + 1, :]
    o_ref[...] = jnp.concatenate([o_b, gb * acc_c], axis=1).astype(o_ref.dtype)

    new_b = xb_ref[tt:tt + CTX_B, :]
    new_c = xc_ref[tt:tt + CTX_C, :]
    xb_ref[0:CTX_B, :] = new_b
    xc_ref[0:CTX_C, :] = new_c

    @pl.when(i == pl.num_programs(1) - 1)
    def _():
        nhb_ref[...] = new_b[CTX_B - hb_rows:, :]
        nhc_ref[...] = new_c[CTX_C - hc_rows:, :]


def conv_branches(z_all, row0, nb, t, tt, hist_b, hist_c, b_cw, b_cb, ln_g, ln_b, c_cw):
    nt = t // tt
    r0 = row0 // tt
    has_hist = hist_b is not None
    w = BRANCH_WIDTH
    in_specs = [
        pl.BlockSpec((tt, 2 * w), lambda b, i: (r0 + b * nt + i, ZB // (2 * w))),
        pl.BlockSpec((tt, 3 * w), lambda b, i: (r0 + b * nt + i, ZC // (3 * w))),
    ]
    args = [z_all, z_all]
    if has_hist:
        in_specs += [pl.BlockSpec((None, B_KERNEL - 1, w), lambda b, i: (b, 0, 0)),
                     pl.BlockSpec((None, C_KERNEL - 1, w), lambda b, i: (b, 0, 0))]
        args += [hist_b, hist_c]
    params = [b_cw, b_cb.reshape(1, w), ln_g.reshape(1, w), ln_b.reshape(1, w), c_cw]
    in_specs += [pl.BlockSpec(p.shape, lambda b, i: (0, 0)) for p in params]
    return pl.pallas_call(
        functools.partial(_conv_kernel, tt=tt, has_hist=has_hist),
        grid=(nb, nt),
        in_specs=in_specs,
        out_specs=[pl.BlockSpec((tt, 2 * w), lambda b, i: (b * nt + i, 0)),
                   pl.BlockSpec((None, B_KERNEL - 1, w), lambda b, i: (b, 0, 0)),
                   pl.BlockSpec((None, C_KERNEL - 1, w), lambda b, i: (b, 0, 0))],
        out_shape=[jax.ShapeDtypeStruct((nb * t, 2 * w), BF16),
                   jax.ShapeDtypeStruct((nb, B_KERNEL - 1, w), F32),
                   jax.ShapeDtypeStruct((nb, C_KERNEL - 1, w), F32)],
        scratch_shapes=[pltpu.VMEM((CTX_B + tt, w), F32), pltpu.VMEM((CTX_C + tt, w), F32)],
        compiler_params=_cparams(("parallel", "arbitrary")),
        name="conv_branches",
    )(*args, *params)


HW = D_HEADS * D_KEY
SUB = 16


def _split_dot(a, b_bf16):
    hi = a.astype(BF16)
    lo = (a - hi.astype(F32)).astype(BF16)
    return _dot(hi, b_bf16) + _dot(lo, b_bf16)


def _hgrn_kernel(*refs, tb, has_state):
    if has_state:
        zd_ref, s0_ref = refs[:2]
        refs = refs[2:]
    else:
        zd_ref = refs[0]
        refs = refs[1:]
    lb_ref, gn_ref, o_ref, sfin_ref, st_ref, vpad_ref, kpad_ref = refs
    i = pl.program_id(1)
    c = min(SUB, tb)

    @pl.when(i == 0)
    def _():
        st_ref[...] = s0_ref[...] if has_state else jnp.zeros(st_ref.shape, F32)
        if tb < LANES:
            vpad_ref[...] = jnp.zeros(vpad_ref.shape, F32)
            kpad_ref[...] = jnp.zeros(kpad_ref.shape, F32)

    zd = zd_ref[...]
    lb = lb_ref[...]
    q = jax.nn.silu(zd[:, 0:HW])
    f = lb + (1.0 - lb) * jax.nn.sigmoid(zd[:, HW:2 * HW])
    logf = jnp.log(f)
    kk = 1.0 - f
    v = zd[:, 2 * HW:3 * HW]
    gate = jax.nn.silu(zd[:, 3 * HW:4 * HW])

    row = lax.broadcasted_iota(I32, (tb, tb), 0)
    col = lax.broadcasted_iota(I32, (tb, tb), 1)
    tril_sub = ((row // c == col // c) & (col <= row)).astype(F32)
    g = _dot(tril_sub, logf, precision=HI)

    hrow = lax.broadcasted_iota(I32, (HW, HW), 0) // D_KEY
    hcol = lax.broadcasted_iota(I32, (HW, HW), 1) // D_KEY
    same_head = hrow == hcol
    head_ones = same_head.astype(BF16)

    if tb < LANES:
        vpad_ref[0:tb, :] = v
        v_t = vpad_ref[...].T
    else:
        v_t = v.T
    v_t = v_t.astype(BF16)
    tok = lax.broadcasted_iota(I32, (tb, HW), 0)
    s_idx = lax.broadcasted_iota(I32, (c, HW), 0)

    n_sub = tb // c
    pieces = []
    for j in range(n_sub):
        r0 = j * c
        gj, qj, kj = g[r0:r0 + c], q[r0:r0 + c], kk[r0:r0 + c]
        for t in range(c):
            mask = s_idx <= t
            e = jnp.exp(jnp.where(mask, gj[t:t + 1] - gj, 0.0))
            pieces.append(jnp.where(mask, qj[t:t + 1] * kj * e, 0.0))
    att = _split_dot(jnp.concatenate(pieces, axis=0), head_ones)
    o_intra = jnp.sum(att.reshape(tb, c, HW) * v.reshape(n_sub, 1, c, HW).repeat(c, axis=1).reshape(tb, c, HW), axis=1)
    q_dec = (q * jnp.exp(g)).astype(BF16)

    upds, decays = [], []
    for j in range(n_sub):
        r0 = j * c
        g_last = g[r0 + c - 1:r0 + c]
        in_sub = (tok >= r0) & (tok < r0 + c)
        kd = jnp.where(in_sub, kk * jnp.exp(jnp.where(in_sub, g_last - g, 0.0)), 0.0)
        if tb < LANES:
            kpad_ref[0:tb, :] = kd
            kd = kpad_ref[...]
        upds.append(jnp.where(same_head, _dot(v_t, kd.astype(BF16)), 0.0))
        decays.append(jnp.exp(g_last))

    st = st_ref[...]
    o_inter = []
    for j in range(n_sub):
        o_inter.append(_dot(q_dec[j * c:(j + 1) * c], st.astype(BF16), NT_DIMS))
        st = st * decays[j] + upds[j]
    st_ref[...] = st

    o = (jnp.concatenate(o_inter, axis=0) if n_sub > 1 else o_inter[0]) + o_intra
    ms = _split_dot(o * o, head_ones) * (1.0 / D_VAL)
    o_ref[...] = (o * lax.rsqrt(ms + EPS) * gn_ref[...] * gate).astype(o_ref.dtype)

    @pl.when(i == pl.num_programs(1) - 1)
    def _():
        sfin_ref[...] = st_ref[...]


def hgrn(z_all, row0, nb, t, tb, s0_t, lb, gn):
    nt = t // tb
    r0 = row0 // tb
    has_state = s0_t is not None
    in_specs = [pl.BlockSpec((tb, 4 * HW), lambda b, i: (r0 + b * nt + i, ZD // (4 * HW)))]
    args = [z_all]
    if has_state:
        in_specs.append(pl.BlockSpec((None, HW, HW), lambda b, i: (b, 0, 0)))
        args.append(s0_t)
    in_specs += [pl.BlockSpec((1, HW), lambda b, i: (0, 0)), pl.BlockSpec((1, HW), lambda b, i: (0, 0))]
    args += [lb.reshape(1, HW), jnp.tile(gn, D_HEADS).reshape(1, HW)]
    pad_rows = LANES if tb < LANES else 8
    return pl.pallas_call(
        functools.partial(_hgrn_kernel, tb=tb, has_state=has_state),
        grid=(nb, nt),
        in_specs=in_specs,
        out_specs=[pl.BlockSpec((tb, HW), lambda b, i: (b * nt + i, 0)),
                   pl.BlockSpec((None, HW, HW), lambda b, i: (b, 0, 0))],
        out_shape=[jax.ShapeDtypeStruct((nb * t, HW), BF16), jax.ShapeDtypeStruct((nb, HW, HW), F32)],
        scratch_shapes=[pltpu.VMEM((HW, HW), F32), pltpu.VMEM((pad_rows, HW), F32), pltpu.VMEM((pad_rows, HW), F32)],
        compiler_params=_cparams(("parallel", "arbitrary")),
        name="hgrn",
    )(*args)


def _merge_kernel(oa_ref, obc_ref, od_ref, ga_ref, gb_ref, gc_ref, gd_ref, wa_ref, wb_ref, wc_ref, wd_ref, wo_ref, x_ref, o_ref):
    obc = obc_ref[...]
    merged = jax.nn.sigmoid(ga_ref[...]) * _dot(oa_ref[...], wa_ref[...])
    merged += jax.nn.sigmoid(gb_ref[...]) * _dot(obc[:, :BRANCH_WIDTH], wb_ref[...])
    merged += jax.nn.sigmoid(gc_ref[...]) * _dot(obc[:, BRANCH_WIDTH:], wc_ref[...])
    merged += jax.nn.sigmoid(gd_ref[...]) * _dot(od_ref[...], wd_ref[...])
    o_ref[...] = x_ref[...] + _dot(merged.astype(BF16), wo_ref[...])


def merge(o_a, o_bc, o_d, z_all, wa_p, wb, wc, wd, w_out, x, tm=512):
    n = x.shape[0]
    row = lambda w: pl.BlockSpec((tm, w), lambda i: (i, 0))
    full = lambda a: pl.BlockSpec(a.shape, lambda i: (0, 0))
    gate = lambda k: pl.BlockSpec((tm, D_MODEL), lambda i: (i, ZG // D_MODEL + k))
    return pl.pallas_call(
        _merge_kernel,
        grid=(n // tm,),
        in_specs=[row(o_a.shape[1]), row(o_bc.shape[1]), row(o_d.shape[1]), gate(0), gate(1), gate(2), gate(3),
                  full(wa_p), full(wb), full(wc), full(wd), full(w_out), row(D_MODEL)],
        out_specs=row(D_MODEL),
        out_shape=jax.ShapeDtypeStruct((n, D_MODEL), F32),
        compiler_params=_cparams(("parallel",)),
        name="merge",
    )(o_a, o_bc, o_d, z_all, z_all, z_all, z_all, wa_p, wb, wc, wd, w_out, x)


HP = P_HEADS * 2
SLOTS = P_HEADS * P_TOPK
SUBK = P_DKEY // 2
STAGE1_UNROLL = 8


def _take_max(s, index, n):
    m = jnp.max(s, axis=0, keepdims=True)
    idx = jnp.min(jnp.where(s == m, index, n), axis=0, keepdims=True)
    return m, idx, jnp.where(index == idx, -jnp.inf, s)


_CAND_BLOCKS = [(0, P_TOPK)] + [(a, 8) for a in range(1, 8)]
_CAND_ROWS = sum(nb for _, nb in _CAND_BLOCKS) + 8


def _route_kernel(x_ref, g_ref, wq_ref, keys_ref, h_ref, ii_ref, jj_ref, gw_ref, q_s, sv_s, si_s, oi_s, oj_s, og_s, *, tm):
    x = x_ref[...]
    h = (x * lax.rsqrt(jnp.mean(x * x, axis=-1, keepdims=True) + EPS) * g_ref[...]).astype(BF16)
    h_ref[...] = h
    q_s[...] = _dot(h, wq_ref[...]).astype(BF16)

    half = P_NKEYS // 2
    n_lt = tm // LANES
    iota_lo = lax.broadcasted_iota(I32, (half, LANES), 0).astype(F32)
    iota_hi = iota_lo + float(half)

    def stage1(it, carry):
        hp = it // n_lt
        toks = pl.ds(pl.multiple_of((it % n_lt) * LANES, LANES), LANES)
        q = q_s[toks, pl.ds(pl.multiple_of(hp * SUBK, SUBK), SUBK)]
        s = _dot(keys_ref[hp], q, NT_DIMS)
        a, b = s[:half], s[half:]
        first = a >= b
        cur, cur_i = jnp.where(first, a, b), jnp.where(first, iota_lo, iota_hi)
        nxt, nxt_i = jnp.where(first, b, a), jnp.where(first, iota_hi, iota_lo)
        vals, idxs = [], []
        for _ in range(P_TOPK):
            m = jnp.max(cur, axis=0, keepdims=True)
            idx = jnp.min(jnp.where(cur == m, cur_i, float(P_NKEYS)), axis=0, keepdims=True)
            hit = cur_i == idx
            cur, cur_i, nxt = jnp.where(hit, nxt, cur), jnp.where(hit, nxt_i, cur_i), jnp.where(hit, -jnp.inf, nxt)
            vals.append(m)
            idxs.append(idx)
        sv_s[hp, :, toks] = jnp.concatenate(vals, axis=0)
        si_s[hp, :, toks] = jnp.concatenate(idxs, axis=0)
        return carry

    lax.fori_loop(0, HP * n_lt, stage1, 0, unroll=STAGE1_UNROLL)

    r = lax.broadcasted_iota(I32, (_CAND_ROWS, tm), 0)
    mid = r - P_TOPK
    flat = jnp.where(r < P_TOPK, r, jnp.where(r < _CAND_ROWS - 8, (1 + mid // 8) * P_TOPK + mid % 8, (r - (_CAND_ROWS - 16)) * P_TOPK))
    flat = flat.astype(F32)
    iota_t = lax.broadcasted_iota(I32, (P_TOPK, tm), 0).astype(F32)

    def stage2(hd, carry):
        sv1, sv2 = sv_s[2 * hd], sv_s[2 * hd + 1]
        si1, si2 = si_s[2 * hd], si_s[2 * hd + 1]
        cand = jnp.concatenate([sv1[a:a + 1] + sv2[0:nb] for a, nb in _CAND_BLOCKS] + [sv1[8:P_TOPK] + sv2[0:1]], axis=0)
        fv, ei, ej = [], [], []
        for _ in range(P_TOPK):
            m, idx, cand = _take_max(cand, flat, float(P_TOPK * P_TOPK))
            a = jnp.floor(idx * (1.0 / P_TOPK))
            b = idx - a * P_TOPK
            fv.append(m)
            ei.append(jnp.sum(jnp.where(iota_t == a, si1, 0.0), axis=0, keepdims=True))
            ej.append(jnp.sum(jnp.where(iota_t == b, si2, 0.0), axis=0, keepdims=True))
        fv = jnp.concatenate(fv, axis=0)
        e = jnp.exp(fv - fv[0:1])
        rows = pl.ds(pl.multiple_of(hd * P_TOPK, P_TOPK), P_TOPK)
        og_s[rows, :] = e / jnp.sum(e, axis=0, keepdims=True)
        oi_s[rows, :] = jnp.concatenate(ei, axis=0)
        oj_s[rows, :] = jnp.concatenate(ej, axis=0)
        return carry

    lax.fori_loop(0, P_HEADS, stage2, 0, unroll=2)
    ii_ref[...] = oi_s[...].T.astype(I32)
    jj_ref[...] = oj_s[...].T.astype(I32)
    gw_ref[...] = og_s[...].T


def peer_route(x, g, wq, keys, tm=256):
    n = x.shape[0]
    row = lambda w: pl.BlockSpec((tm, w), lambda i: (i, 0))
    return pl.pallas_call(
        functools.partial(_route_kernel, tm=tm),
        grid=(n // tm,),
        in_specs=[row(D_MODEL), pl.BlockSpec((1, D_MODEL), lambda i: (0, 0)),
                  pl.BlockSpec(wq.shape, lambda i: (0, 0)), pl.BlockSpec(keys.shape, lambda i: (0, 0, 0))],
        out_specs=[row(D_MODEL), row(SLOTS), row(SLOTS), row(SLOTS)],
        out_shape=[jax.ShapeDtypeStruct((n, D_MODEL), BF16), jax.ShapeDtypeStruct((n, SLOTS), I32),
                   jax.ShapeDtypeStruct((n, SLOTS), I32), jax.ShapeDtypeStruct((n, SLOTS), F32)],
        scratch_shapes=[pltpu.VMEM((tm, HP * SUBK), BF16), pltpu.VMEM((HP, P_TOPK, tm), F32), pltpu.VMEM((HP, P_TOPK, tm), F32),
                        pltpu.VMEM((SLOTS, tm), F32), pltpu.VMEM((SLOTS, tm), F32), pltpu.VMEM((SLOTS, tm), F32)],
        compiler_params=_cparams(("parallel",)),
        name="peer_route",
    )(x, g.reshape(1, D_MODEL), wq, keys)


CHUNK_I = 16
CHUNK_E = CHUNK_I * P_NKEYS
N_CHUNKS = P_NKEYS // CHUNK_I


def _peer_kernel(h_ref, ii_ref, jj_ref, gw_ref, u_ref, v_ref, x_ref, o_ref, w_s, acc_s, *, tm):
    c = pl.program_id(1)

    @pl.when(c == 0)
    def _():
        acc_s[...] = jnp.zeros(acc_s.shape, F32)
        iota = lax.broadcasted_iota(I32, (P_NKEYS, SLOTS), 0)

        def build(g, carry):
            t0 = pl.multiple_of(g * PACK, PACK)
            ws = []
            for t in range(PACK):
                irow = ii_ref[pl.ds(t0 + t, 1), :]
                jrow = jj_ref[pl.ds(t0 + t, 1), :]
                grow = gw_ref[pl.ds(t0 + t, 1), :]
                p_t = jnp.where(iota == irow, grow, 0.0).astype(BF16)
                q_t = jnp.where(iota == jrow, 1.0, 0.0).astype(BF16)
                ws.append(_dot(p_t, q_t, NT_DIMS).astype(BF16))
            w_s[:, pl.ds(t0, PACK), :] = pltpu.einshape("tij->itj", jnp.stack(ws, axis=0))
            return carry

        lax.fori_loop(0, tm // PACK, build, 0, unroll=2)

    a = _dot(h_ref[...], u_ref[...], NT_DIMS)
    act = 0.5 * a * (1.0 + lax.erf(a * (1.0 / math.sqrt(2.0))))
    wd = jnp.concatenate([w_s[c * CHUNK_I + r] for r in range(CHUNK_I)], axis=1)
    acc_s[...] += _dot(act.astype(BF16) * wd, v_ref[...])

    @pl.when(c == N_CHUNKS - 1)
    def _():
        o_ref[...] = x_ref[...] + acc_s[...]


def peer_experts(h2, ii, jj, gw, u_tab, v_tab, x, tm=512):
    n = x.shape[0]
    row = lambda w: pl.BlockSpec((tm, w), lambda i, c: (i, 0))
    return pl.pallas_call(
        functools.partial(_peer_kernel, tm=tm),
        grid=(n // tm, N_CHUNKS),
        in_specs=[row(D_MODEL), row(SLOTS), row(SLOTS), row(SLOTS),
                  pl.BlockSpec((CHUNK_E, D_MODEL), lambda i, c: (c, 0)),
                  pl.BlockSpec((CHUNK_E, D_MODEL), lambda i, c: (c, 0)),
                  row(D_MODEL)],
        out_specs=row(D_MODEL),
        out_shape=jax.ShapeDtypeStruct((n, D_MODEL), F32),
        scratch_shapes=[pltpu.VMEM((P_NKEYS, tm, P_NKEYS), BF16), pltpu.VMEM((tm, D_MODEL), F32)],
        compiler_params=_cparams(("parallel", "arbitrary")),
        name="peer_experts",
    )(h2, ii, jj, gw, u_tab, v_tab, x)


def _rmsnorm_kernel(x_ref, g_ref, o_ref):
    x = x_ref[...]
    o_ref[...] = x * lax.rsqrt(jnp.mean(x * x, axis=-1, keepdims=True) + EPS) * g_ref[...]


def rmsnorm(x, g, row0, n, tm=512):
    d = x.shape[1]
    r0 = row0 // tm
    return pl.pallas_call(
        _rmsnorm_kernel,
        grid=(n // tm,),
        in_specs=[pl.BlockSpec((tm, d), lambda i: (r0 + i, 0)), pl.BlockSpec((1, d), lambda i: (0, 0))],
        out_specs=pl.BlockSpec((tm, d), lambda i: (i, 0)),
        out_shape=jax.ShapeDtypeStruct((n, d), F32),
        compiler_params=_cparams(("parallel",)),
        name="rmsnorm",
    )(x, g.reshape(1, d))


def _rope_tables():
    half = A_ROPE // 2
    inv = ROPE_THETA ** (-jnp.arange(half, dtype=F32) / half)
    pos = jnp.concatenate([jnp.tile(jnp.arange(SEQ), BATCH), jnp.tile(PAST_LEN + jnp.arange(DEC_SEQ), DEC_BATCH)])
    ang = pos.astype(F32)[:, None] * inv[None, :]
    cos, sin = jnp.cos(ang), jnp.sin(ang)
    zeros = lambda w: jnp.zeros((NT, w), F32)
    ccq = jnp.concatenate([jnp.ones((NT, A_NOPE), F32), cos, cos, zeros(HEAD_PAD - A_NOPE - A_ROPE)], axis=1)
    ssq = jnp.concatenate([zeros(A_NOPE), -sin, sin, zeros(HEAD_PAD - A_NOPE - A_ROPE)], axis=1)
    cck = jnp.concatenate([cos, cos, zeros(LANES - A_ROPE)], axis=1)
    ssk = jnp.concatenate([-sin, sin, zeros(LANES - A_ROPE)], axis=1)
    return ccq, ssq, cck, ssk


def _swap_halves(w):
    half = w.shape[-1] // 2
    return jnp.concatenate([w[..., half:], w[..., :half]], axis=-1)


def _layer_weights(l, w_in, a_w_uq, a_w_uk, a_w_uv, w_branch, w_out, p_w_q, p_sub_keys, p_u, p_v):
    w = w_in[l]
    o = 0
    parts = []
    for size in (A_Q_LORA, A_KV_LORA, A_ROPE, 2 * BRANCH_WIDTH, 3 * BRANCH_WIDTH, 4 * HW, N_BRANCH * D_MODEL):
        parts.append(w[:, o:o + size])
        o += size
    cq, ckv, kr, b_in, c_in, d_in, gate = parts
    pad = jnp.zeros((D_MODEL, ZB - A_Q_LORA - A_KV_LORA - 2 * A_ROPE), F32)
    w_in_p = jnp.concatenate([cq, ckv, kr, _swap_halves(kr), pad, b_in, d_in, gate, c_in], axis=1).astype(BF16)

    wq = a_w_uq[l].reshape(A_Q_LORA, A_HEADS, A_NOPE + A_ROPE)
    nope, rope = wq[..., :A_NOPE], wq[..., A_NOPE:]
    tail = jnp.zeros((A_Q_LORA, A_HEADS, HEAD_PAD - A_NOPE - A_ROPE), F32)
    wq_main = jnp.concatenate([nope, rope, tail], axis=-1).reshape(A_Q_LORA, -1).astype(BF16)
    wq_swap = jnp.concatenate([jnp.zeros_like(nope), _swap_halves(rope), tail], axis=-1).reshape(A_Q_LORA, -1).astype(BF16)

    w_uk, w_uv = a_w_uk[l], a_w_uv[l]
    head_tail = jnp.zeros((A_KV_LORA, A_HEADS, HEAD_PAD - A_NOPE), F32)
    wuk_p = jnp.concatenate([w_uk, head_tail], axis=-1).reshape(A_KV_LORA, -1).astype(BF16)
    wuv_p = jnp.concatenate([w_uv, head_tail], axis=-1).reshape(A_KV_LORA, -1).astype(BF16)

    r = jnp.arange(LANES)[:, None]
    col = jnp.arange(A_HEADS * HEAD_PAD)[None, :]
    place = ((col % HEAD_PAD == A_NOPE + r) & (r < A_ROPE)).astype(BF16)

    blk = jnp.zeros((A_HEADS, HEAD_PAD, 2 * LANES), F32)
    blk = blk.at[:, :A_NOPE, :A_KV_LORA].set(jnp.transpose(w_uk, (1, 2, 0)))
    blk = blk.at[:, A_NOPE:A_NOPE + A_ROPE, A_KV_LORA:A_KV_LORA + A_ROPE].set(jnp.eye(A_ROPE, dtype=F32))
    eye_h = jnp.eye(A_HEADS, dtype=F32)
    wabs = (blk[:, :, None, :] * eye_h[:, None, :, None]).reshape(A_HEADS * HEAD_PAD, A_HEADS * 2 * LANES).astype(BF16)

    wb = w_branch[l]
    wa_p = jnp.concatenate([wb[0].reshape(A_HEADS, A_V, D_MODEL), jnp.zeros((A_HEADS, HEAD_PAD - A_V, D_MODEL), F32)],
                           axis=1).reshape(A_HEADS * HEAD_PAD, D_MODEL).astype(BF16)
    return dict(w_in_p=w_in_p, wq_main=wq_main, wq_swap=wq_swap, wuk_p=wuk_p, wuv_p=wuv_p, place=place, wabs=wabs,
                wa_p=wa_p, wb=wb[1].astype(BF16), wc=wb[2].astype(BF16), wd=wb[3].astype(BF16), w_out=w_out[l].astype(BF16),
                wq=p_w_q[l].astype(BF16), keys=p_sub_keys[l].reshape(HP, P_NKEYS, SUBK).astype(BF16),
                u=p_u[l].astype(BF16), v=p_v[l].astype(BF16))


def _state_to_kernel(s):
    t = jnp.transpose(s, (0, 1, 3, 2))
    eye_h = jnp.eye(D_HEADS, dtype=F32)
    return (t[:, :, :, None, :] * eye_h[None, :, None, :, None]).reshape(s.shape[0], HW, HW)


def _state_from_kernel(s_t):
    s5 = s_t.reshape(s_t.shape[0], D_HEADS, D_VAL, D_HEADS, D_KEY)
    diag = jnp.stack([s5[:, h, :, h, :] for h in range(D_HEADS)], axis=1)
    return jnp.transpose(diag, (0, 1, 3, 2))


def kernel(x_prompt, x_sample, cache_ckv, cache_krope, page_table, state_conv_b, state_conv_c, state_hgrn, norm1_g, w_in, a_q_norm_g, a_w_uq, a_kv_norm_g, a_w_uk, a_w_uv, b_conv_w, b_conv_b, b_ln_g, b_ln_b, c_conv_w, d_lower_bound, d_gnorm_g, w_branch, w_out, norm2_g, p_w_q, p_sub_keys, p_u, p_v, final_norm_g):
    lb_soft = jax.nn.softmax(d_lower_bound.astype(F32), axis=0)
    lower_bounds = jnp.cumsum(lb_soft, axis=0) - lb_soft[0:1]
    ccq, ssq, cck, ssk = _rope_tables()
    x = jnp.concatenate([x_prompt.reshape(NP, D_MODEL), x_sample.reshape(NS, D_MODEL)], axis=0)
    cache_krope_t = jnp.swapaxes(cache_krope, 2, 3)

    states_p, states_s = [], []
    for l in range(DEPTH):
        w = _layer_weights(l, w_in, a_w_uq, a_w_uk, a_w_uv, w_branch, w_out, p_w_q, p_sub_keys, p_u, p_v)
        z = norm_matmul(x, norm1_g[l], w["w_in_p"], 1024, 2304)

        ckv, kr, qp, kp, vp = mla_prep(z, a_q_norm_g[l], a_kv_norm_g[l], w["wq_main"], w["wq_swap"], w["wuk_p"], w["wuv_p"],
                                       w["place"], ccq, ssq, cck, ssk)
        oa_p = flash_prompt(qp, kp, vp)
        qlat = matmul(qp[NP:], w["wabs"], 512, BF16)
        qlat = jnp.transpose(qlat.reshape(DEC_BATCH, DEC_SEQ, A_HEADS, 2 * LANES), (0, 2, 1, 3)).reshape(DEC_BATCH, DEC_ROWS, 2 * LANES)
        oa_s = mla_decode(l, page_table, qlat, cache_ckv, cache_krope_t, ckv[NP:].reshape(DEC_BATCH, DEC_SEQ, LANES),
                          kr[NP:].reshape(DEC_BATCH, DEC_SEQ, LANES), w["wuv_p"])
        o_a = jnp.concatenate([oa_p, oa_s.reshape(NS, -1)], axis=0)

        conv_w = (b_conv_w[l], b_conv_b[l], b_ln_g[l], b_ln_b[l], c_conv_w[l])
        obc_p, hb_p, hc_p = conv_branches(z, 0, BATCH, SEQ, 512, None, None, *conv_w)
        obc_s, hb_s, hc_s = conv_branches(z, NP, DEC_BATCH, DEC_SEQ, DEC_SEQ, state_conv_b[l], state_conv_c[l], *conv_w)
        o_bc = jnp.concatenate([obc_p, obc_s], axis=0)

        od_p, st_p = hgrn(z, 0, BATCH, SEQ, LANES, None, lower_bounds[l], d_gnorm_g[l])
        od_s, st_s = hgrn(z, NP, DEC_BATCH, DEC_SEQ, DEC_SEQ, _state_to_kernel(state_hgrn[l]), lower_bounds[l], d_gnorm_g[l])
        o_d = jnp.concatenate([od_p, od_s], axis=0)

        x1 = merge(o_a, o_bc, o_d, z, w["wa_p"], w["wb"], w["wc"], w["wd"], w["w_out"], x)
        h2, ii, jj, gw = peer_route(x1, norm2_g[l], w["wq"], w["keys"])
        x = peer_experts(h2, ii, jj, gw, w["u"], w["v"], x1)

        states_p.append((ckv[:NP].reshape(BATCH, SEQ, A_KV_LORA), kr[:NP, :A_ROPE].reshape(BATCH, SEQ, A_ROPE),
                         hb_p, hc_p, _state_from_kernel(st_p)))
        states_s.append((ckv[NP:].reshape(DEC_BATCH, DEC_SEQ, A_KV_LORA), kr[NP:, :A_ROPE].reshape(DEC_BATCH, DEC_SEQ, A_ROPE),
                         hb_s, hc_s, _state_from_kernel(st_s)))

    y_p = rmsnorm(x, final_norm_g, 0, NP)
    y_s = rmsnorm(x, final_norm_g, NP, NS)
    stack = lambda states: [jnp.stack([s[i] for s in states], axis=0) for i in range(5)]
    return (y_p.reshape(BATCH, SEQ, D_MODEL), y_s.reshape(DEC_BATCH, DEC_SEQ, D_MODEL), *stack(states_p), *stack(states_s))
```

```python
import functools
import math

import jax
import jax.numpy as jnp
from jax import lax
from jax.experimental import pallas as pl
from jax.experimental.pallas import tpu as pltpu

F32 = jnp.float32
BF16 = jnp.bfloat16
I32 = jnp.int32

D_MODEL = 1024
BATCH = 2
SEQ = 8192
DEPTH = 2
DEC_BATCH = 128
DEC_SEQ = 8
PAST_LEN = 8192
PAGE_SIZE = 128
N_PAGES = PAST_LEN // PAGE_SIZE
BRANCH_WIDTH = 256
N_BRANCH = 4
A_HEADS = 4
A_NOPE = 64
A_ROPE = 32
A_V = 64
A_Q_LORA = 256
A_KV_LORA = 128
ROPE_THETA = 10000.0
B_KERNEL = 31
C_KERNEL = 3
D_HEADS = 4
D_KEY = 64
D_VAL = 64
P_HEADS = 8
P_NKEYS = 128
P_DKEY = 256
P_TOPK = 16
P_EXPERTS = P_NKEYS * P_NKEYS
EPS = 1e-6
NEG_BIG = -1e30

NP = BATCH * SEQ
NS = DEC_BATCH * DEC_SEQ
NT = NP + NS

LANES = 128
SUBLANES = 8
PACK = 16
HEAD_PAD = 128
ZA, ZB, ZD, ZG, ZC = 0, 512, 1024, 2048, 6144
Z_COLS = 6912
VMEM_LIMIT = 56 * 1024 * 1024

HI = lax.Precision.HIGHEST


def _cparams(sem):
    return pltpu.CompilerParams(dimension_semantics=sem, vmem_limit_bytes=VMEM_LIMIT)


def _dot(a, b, dims=(((1,), (0,)), ((), ())), precision=None):
    return lax.dot_general(a, b, dims, precision=precision, preferred_element_type=F32)


NT_DIMS = (((1,), (1,)), ((), ()))
TN_DIMS = (((0,), (0,)), ((), ()))


def _norm_matmul_kernel(x_ref, g_ref, w_ref, o_ref, h_ref):
    @pl.when(pl.program_id(1) == 0)
    def _():
        x = x_ref[...]
        y = x * lax.rsqrt(jnp.mean(x * x, axis=-1, keepdims=True) + EPS)
        h_ref[...] = (y * g_ref[...]).astype(BF16)

    o_ref[...] = _dot(h_ref[...], w_ref[...])


def norm_matmul(x, g, w, tm, tn):
    n, d = x.shape
    cols = w.shape[1]
    return pl.pallas_call(
        _norm_matmul_kernel,
        grid=(n // tm, cols // tn),
        in_specs=[
            pl.BlockSpec((tm, d), lambda i, j: (i, 0)),
            pl.BlockSpec((1, d), lambda i, j: (0, 0)),
            pl.BlockSpec((d, tn), lambda i, j: (0, j)),
        ],
        out_specs=pl.BlockSpec((tm, tn), lambda i, j: (i, j)),
        out_shape=jax.ShapeDtypeStruct((n, cols), F32),
        scratch_shapes=[pltpu.VMEM((tm, d), BF16)],
        compiler_params=_cparams(("parallel", "arbitrary")),
        name="norm_matmul",
    )(x, g.reshape(1, d), w)


def _matmul_kernel(x_ref, w_ref, o_ref):
    o_ref[...] = _dot(x_ref[...], w_ref[...]).astype(o_ref.dtype)


def matmul(x, w, tm, out_dtype):
    n, d = x.shape
    cols = w.shape[1]
    return pl.pallas_call(
        _matmul_kernel,
        grid=(n // tm,),
        in_specs=[pl.BlockSpec((tm, d), lambda i: (i, 0)), pl.BlockSpec((d, cols), lambda i: (0, 0))],
        out_specs=pl.BlockSpec((tm, cols), lambda i: (i, 0)),
        out_shape=jax.ShapeDtypeStruct((n, cols), out_dtype),
        compiler_params=_cparams(("parallel",)),
        name="matmul",
    )(x, w)


def _tile4(t):
    return jnp.concatenate([t, t, t, t], axis=1)


def _mla_prep_kernel(z_ref, qn_ref, kvn_ref, wqm_ref, wqs_ref, wuk_ref, wuv_ref, place_ref,
                     ccq_ref, ssq_ref, cck_ref, ssk_ref,
                     ckv_ref, kr_ref, q_ref, k_ref, v_ref):
    z = z_ref[...]
    cq = z[:, 0:A_Q_LORA]
    cqn = cq * lax.rsqrt(jnp.mean(cq * cq, axis=-1, keepdims=True) + EPS) * qn_ref[...]
    cqn = cqn.astype(BF16)
    scale = 1.0 / math.sqrt(A_NOPE + A_ROPE)
    q = _dot(cqn, wqm_ref[...]) * _tile4(ccq_ref[...]) + _dot(cqn, wqs_ref[...]) * _tile4(ssq_ref[...])
    q_ref[...] = (q * scale).astype(BF16)

    c = z[:, A_Q_LORA:A_Q_LORA + A_KV_LORA]
    ckv = c * lax.rsqrt(jnp.mean(c * c, axis=-1, keepdims=True) + EPS) * kvn_ref[...]
    ckv_ref[...] = ckv
    ckv_b = ckv.astype(BF16)

    kc = z[:, A_Q_LORA + A_KV_LORA:]
    kr = kc * cck_ref[...] + pltpu.roll(kc, LANES - A_ROPE, axis=1) * ssk_ref[...]
    kr_ref[...] = kr

    k = _dot(ckv_b, wuk_ref[...]) + _dot(kr.astype(BF16), place_ref[...])
    k_ref[...] = k.astype(BF16)
    v_ref[...] = _dot(ckv_b, wuv_ref[...]).astype(BF16)


def mla_prep(z_all, qn_g, kvn_g, wq_main, wq_swap, wuk_p, wuv_p, place, ccq, ssq, cck, ssk, tm=512):
    n = z_all.shape[0]
    hp = A_HEADS * HEAD_PAD
    row = lambda w: pl.BlockSpec((tm, w), lambda i: (i, 0))
    full = lambda a: pl.BlockSpec(a.shape, lambda i: (0,) * a.ndim)
    qn_g = qn_g.reshape(1, -1)
    kvn_g = kvn_g.reshape(1, -1)
    return pl.pallas_call(
        _mla_prep_kernel,
        grid=(n // tm,),
        in_specs=[row(512), full(qn_g), full(kvn_g), full(wq_main), full(wq_swap), full(wuk_p), full(wuv_p), full(place),
                  row(LANES), row(LANES), row(LANES), row(LANES)],
        out_specs=[row(LANES), row(LANES), row(hp), row(hp), row(hp)],
        out_shape=[jax.ShapeDtypeStruct((n, LANES), F32), jax.ShapeDtypeStruct((n, LANES), F32),
                   jax.ShapeDtypeStruct((n, hp), BF16), jax.ShapeDtypeStruct((n, hp), BF16),
                   jax.ShapeDtypeStruct((n, hp), BF16)],
        compiler_params=_cparams(("parallel",)),
        name="mla_prep",
    )(z_all, qn_g, kvn_g, wq_main, wq_swap, wuk_p, wuv_p, place, ccq, ssq, cck, ssk)


def _flash_kernel(q_ref, k_ref, v_ref, o_ref, *, tq, tk):
    i = pl.program_id(2)
    q = q_ref[...]

    def step(j, carry, masked):
        m, l, acc = carry
        kj = k_ref[pl.ds(pl.multiple_of(j * tk, tk), tk), :]
        vj = v_ref[pl.ds(pl.multiple_of(j * tk, tk), tk), :]
        s = _dot(q, kj, NT_DIMS)
        if masked:
            qpos = i * tq + lax.broadcasted_iota(I32, (tq, tk), 0)
            kpos = j * tk + lax.broadcasted_iota(I32, (tq, tk), 1)
            s = jnp.where(kpos <= qpos, s, NEG_BIG)
        m_new = jnp.maximum(m, jnp.max(s, axis=-1, keepdims=True))
        alpha = jnp.exp(m - m_new)
        p = jnp.exp(s - m_new)
        l = alpha * l + jnp.sum(p, axis=-1, keepdims=True)
        acc = alpha * acc + _dot(p.astype(BF16), vj)
        return m_new, l, acc

    init = (jnp.full((tq, 1), NEG_BIG, F32), jnp.zeros((tq, 1), F32), jnp.zeros((tq, HEAD_PAD), F32))
    n_full = (i * tq) // tk
    carry = lax.fori_loop(0, n_full // 2, lambda p, c: step(2 * p + 1, step(2 * p, c, False), False), init)
    carry = lax.fori_loop(0, n_full % 2, lambda _, c: step(n_full - 1, c, False), carry)
    for d in range(tq // tk):
        carry = step(n_full + d, carry, True)
    m, l, acc = carry
    o_ref[...] = (acc / l).astype(o_ref.dtype)


def flash_prompt(qp, kp, vp, tq=1024, tk=1024):
    assert tq % tk == 0
    nq = SEQ // tq
    return pl.pallas_call(
        functools.partial(_flash_kernel, tq=tq, tk=tk),
        grid=(BATCH, A_HEADS, nq),
        in_specs=[
            pl.BlockSpec((tq, HEAD_PAD), lambda b, h, i: (b * nq + i, h)),
            pl.BlockSpec((SEQ, HEAD_PAD), lambda b, h, i: (b, h)),
            pl.BlockSpec((SEQ, HEAD_PAD), lambda b, h, i: (b, h)),
        ],
        out_specs=pl.BlockSpec((tq, HEAD_PAD), lambda b, h, i: (b * nq + i, h)),
        out_shape=jax.ShapeDtypeStruct((NP, A_HEADS * HEAD_PAD), BF16),
        compiler_params=_cparams(("parallel", "parallel", "arbitrary")),
        name="flash_prompt",
    )(qp, kp, vp)


DEC_ROWS = A_HEADS * DEC_SEQ


def _decode_kernel(pt_ref, q_ref, cnew_ref, knew_ref, wuv_ref, ckv_hbm, kr_hbm, o_ref, kbuf, krbuf, sem, *, page0):
    b = pl.program_id(0)
    slot = b % 2

    def page_copies(seq, sl, r):
        page = page0 + pt_ref[seq, r]
        return (pltpu.make_async_copy(ckv_hbm.at[page], kbuf.at[sl, r], sem.at[0, sl]),
                pltpu.make_async_copy(kr_hbm.at[page], krbuf.at[sl, r], sem.at[1, sl]))

    def start_fetch(seq, sl):
        def body(r, carry):
            for cp in page_copies(seq, sl, r):
                cp.start()
            return carry
        lax.fori_loop(0, N_PAGES, body, 0)

    @pl.when(b == 0)
    def _():
        start_fetch(0, 0)

    @pl.when(b + 1 < pl.num_programs(0))
    def _():
        start_fetch(b + 1, 1 - slot)

    def wait_body(r, carry):
        for cp in page_copies(b, slot, r):
            cp.wait()
        return carry

    lax.fori_loop(0, N_PAGES, wait_body, 0)

    q = q_ref[...]
    q_lat = q[:, :A_KV_LORA]
    q_rope = q[:, A_KV_LORA:A_KV_LORA + A_ROPE]
    kc = kbuf[slot].reshape(PAST_LEN, A_KV_LORA).astype(BF16)
    kr_t = jnp.concatenate([krbuf[slot, r] for r in range(N_PAGES)], axis=1).astype(BF16)
    s_past = _dot(q_lat, kc, NT_DIMS) + _dot(q_rope, kr_t)

    cn = cnew_ref[...].astype(BF16)
    kn = knew_ref[...][:, :A_ROPE].astype(BF16)
    s_new = _dot(q_lat, cn, NT_DIMS) + _dot(q_rope, kn, NT_DIMS)
    t_q = lax.broadcasted_iota(I32, s_new.shape, 0) % DEC_SEQ
    t_k = lax.broadcasted_iota(I32, s_new.shape, 1)
    s_new = jnp.where(t_k <= t_q, s_new, NEG_BIG)

    m = jnp.maximum(jnp.max(s_past, axis=-1, keepdims=True), jnp.max(s_new, axis=-1, keepdims=True))
    p_past = jnp.exp(s_past - m)
    p_new = jnp.exp(s_new - m)
    l = jnp.sum(p_past, axis=-1, keepdims=True) + jnp.sum(p_new, axis=-1, keepdims=True)
    acc = _dot(p_past.astype(BF16), kc) + _dot(p_new.astype(BF16), cn)
    o_lat = (acc / l).astype(BF16)
    for h in range(A_HEADS):
        o_ref[:, h * HEAD_PAD:(h + 1) * HEAD_PAD] = _dot(
            o_lat[h * DEC_SEQ:(h + 1) * DEC_SEQ, :], wuv_ref[:, h * HEAD_PAD:(h + 1) * HEAD_PAD]).astype(o_ref.dtype)


def mla_decode(layer, page_table, qlat, cache_ckv, cache_krope_t, ckv_new, kr_new, wuv_p):
    n_pool = cache_ckv.shape[1]
    hbm = pl.BlockSpec(memory_space=pl.ANY)
    grid_spec = pltpu.PrefetchScalarGridSpec(
        num_scalar_prefetch=1,
        grid=(DEC_BATCH,),
        in_specs=[
            pl.BlockSpec((None, DEC_ROWS, 2 * LANES), lambda b, pt: (b, 0, 0)),
            pl.BlockSpec((None, DEC_SEQ, LANES), lambda b, pt: (b, 0, 0)),
            pl.BlockSpec((None, DEC_SEQ, LANES), lambda b, pt: (b, 0, 0)),
            pl.BlockSpec(wuv_p.shape, lambda b, pt: (0, 0)),
            hbm, hbm,
        ],
        out_specs=pl.BlockSpec((None, DEC_SEQ, A_HEADS * HEAD_PAD), lambda b, pt: (b, 0, 0)),
        scratch_shapes=[pltpu.VMEM((2, N_PAGES, PAGE_SIZE, A_KV_LORA), F32), pltpu.VMEM((2, N_PAGES, A_ROPE, PAGE_SIZE), F32),
                        pltpu.SemaphoreType.DMA((2, 2))],
    )
    return pl.pallas_call(
        functools.partial(_decode_kernel, page0=layer * n_pool),
        grid_spec=grid_spec,
        out_shape=jax.ShapeDtypeStruct((DEC_BATCH, DEC_SEQ, A_HEADS * HEAD_PAD), BF16),
        compiler_params=_cparams(("arbitrary",)),
        name="mla_decode",
    )(page_table, qlat, ckv_new, kr_new, wuv_p, cache_ckv.reshape(-1, PAGE_SIZE, A_KV_LORA), cache_krope_t.reshape(-1, A_ROPE, PAGE_SIZE))


CTX_B = 32
CTX_C = 8


def _conv_kernel(*refs, tt, has_hist):
    if has_hist:
        zb_ref, zc_ref, hb_ref, hc_ref = refs[:4]
        refs = refs[4:]
    else:
        zb_ref, zc_ref = refs[:2]
        refs = refs[2:]
    bw_ref, bb_ref, lg_ref, lb_ref, cw_ref, o_ref, nhb_ref, nhc_ref, xb_ref, xc_ref = refs
    i = pl.program_id(1)
    hb_rows, hc_rows = B_KERNEL - 1, C_KERNEL - 1

    @pl.when(i == 0)
    def _():
        xb_ref[0:CTX_B, :] = jnp.zeros((CTX_B, BRANCH_WIDTH), F32)
        xc_ref[0:CTX_C, :] = jnp.zeros((CTX_C, BRANCH_WIDTH), F32)
        if has_hist:
            xb_ref[CTX_B - hb_rows:CTX_B, :] = hb_ref[...]
            xc_ref[CTX_C - hc_rows:CTX_C, :] = hc_ref[...]

    zb = zb_ref[...]
    xb_ref[CTX_B:CTX_B + tt, :] = zb[:, :BRANCH_WIDTH] * jax.nn.sigmoid(zb[:, BRANCH_WIDTH:])
    acc = jnp.zeros((tt, BRANCH_WIDTH), F32)
    for k in range(B_KERNEL):
        acc = acc + xb_ref[pl.ds(CTX_B - hb_rows + k, tt), :] * bw_ref[k:k + 1, :]
    y = acc + bb_ref[...]
    mu = jnp.mean(y, axis=-1, keepdims=True)
    yc = y - mu
    var = jnp.mean(yc * yc, axis=-1, keepdims=True)
    o_b = jax.nn.silu(yc * lax.rsqrt(var + EPS) * lg_ref[...] + lb_ref[...])

    zc = zc_ref[...]
    gb = zc[:, :BRANCH_WIDTH]
    xc_ref[CTX_C:CTX_C + tt, :] = zc[:, BRANCH_WIDTH:2 * BRANCH_WIDTH] * zc[:, 2 * BRANCH_WIDTH:]
    acc_c = jnp.zeros((tt, BRANCH_WIDTH), F32)
    for k in range(C_KERNEL):
        acc_c = acc_c + xc_ref[pl.ds(CTX_C - hc_rows + k, tt), :] * cw_ref[k:k + 1, :]
    o_ref[...] = jnp.concatenate([o_b, gb * acc_c], axis=1).astype(o_ref.dtype)

    new_b = xb_ref[tt:tt + CTX_B, :]
    new_c = xc_ref[tt:tt + CTX_C, :]
    xb_ref[0:CTX_B, :] = new_b
    xc_ref[0:CTX_C, :] = new_c

    @pl.when(i == pl.num_programs(1) - 1)
    def _():
        nhb_ref[...] = new_b[CTX_B - hb_rows:, :]
        nhc_ref[...] = new_c[CTX_C - hc_rows:, :]


def conv_branches(z_all, row0, nb, t, tt, hist_b, hist_c, b_cw, b_cb, ln_g, ln_b, c_cw):
    nt = t // tt
    r0 = row0 // tt
    has_hist = hist_b is not None
    w = BRANCH_WIDTH
    in_specs = [
        pl.BlockSpec((tt, 2 * w), lambda b, i: (r0 + b * nt + i, ZB // (2 * w))),
        pl.BlockSpec((tt, 3 * w), lambda b, i: (r0 + b * nt + i, ZC // (3 * w))),
    ]
    args = [z_all, z_all]
    if has_hist:
        in_specs += [pl.BlockSpec((None, B_KERNEL - 1, w), lambda b, i: (b, 0, 0)),
                     pl.BlockSpec((None, C_KERNEL - 1, w), lambda b, i: (b, 0, 0))]
        args += [hist_b, hist_c]
    params = [b_cw, b_cb.reshape(1, w), ln_g.reshape(1, w), ln_b.reshape(1, w), c_cw]
    in_specs += [pl.BlockSpec(p.shape, lambda b, i: (0, 0)) for p in params]
    return pl.pallas_call(
        functools.partial(_conv_kernel, tt=tt, has_hist=has_hist),
        grid=(nb, nt),
        in_specs=in_specs,
        out_specs=[pl.BlockSpec((tt, 2 * w), lambda b, i: (b * nt + i, 0)),
                   pl.BlockSpec((None, B_KERNEL - 1, w), lambda b, i: (b, 0, 0)),
                   pl.BlockSpec((None, C_KERNEL - 1, w), lambda b, i: (b, 0, 0))],
        out_shape=[jax.ShapeDtypeStruct((nb * t, 2 * w), BF16),
                   jax.ShapeDtypeStruct((nb, B_KERNEL - 1, w), F32),
                   jax.ShapeDtypeStruct((nb, C_KERNEL - 1, w), F32)],
        scratch_shapes=[pltpu.VMEM((CTX_B + tt, w), F32), pltpu.VMEM((CTX_C + tt, w), F32)],
        compiler_params=_cparams(("parallel", "arbitrary")),
        name="conv_branches",
    )(*args, *params)


HW = D_HEADS * D_KEY
SUB = 16


def _split_dot(a, b_bf16):
    hi = a.astype(BF16)
    lo = (a - hi.astype(F32)).astype(BF16)
    return _dot(hi, b_bf16) + _dot(lo, b_bf16)


def _hgrn_kernel(*refs, tb, has_state):
    if has_state:
        zd_ref, s0_ref = refs[:2]
        refs = refs[2:]
    else:
        zd_ref = refs[0]
        refs = refs[1:]
    lb_ref, gn_ref, o_ref, sfin_ref, st_ref, vpad_ref, kpad_ref = refs
    i = pl.program_id(1)
    c = min(SUB, tb)

    @pl.when(i == 0)
    def _():
        st_ref[...] = s0_ref[...] if has_state else jnp.zeros(st_ref.shape, F32)
        if tb < LANES:
            vpad_ref[...] = jnp.zeros(vpad_ref.shape, F32)
            kpad_ref[...] = jnp.zeros(kpad_ref.shape, F32)

    zd = zd_ref[...]
    lb = lb_ref[...]
    q = jax.nn.silu(zd[:, 0:HW])
    f = lb + (1.0 - lb) * jax.nn.sigmoid(zd[:, HW:2 * HW])
    logf = jnp.log(f)
    kk = 1.0 - f
    v = zd[:, 2 * HW:3 * HW]
    gate = jax.nn.silu(zd[:, 3 * HW:4 * HW])

    row = lax.broadcasted_iota(I32, (tb, tb), 0)
    col = lax.broadcasted_iota(I32, (tb, tb), 1)
    tril_sub = ((row // c == col // c) & (col <= row)).astype(F32)
    g = _dot(tril_sub, logf, precision=HI)

    hrow = lax.broadcasted_iota(I32, (HW, HW), 0) // D_KEY
    hcol = lax.broadcasted_iota(I32, (HW, HW), 1) // D_KEY
    same_head = hrow == hcol
    head_ones = same_head.astype(BF16)

    if tb < LANES:
        vpad_ref[0:tb, :] = v
        v_t = vpad_ref[...].T
    else:
        v_t = v.T
    v_t = v_t.astype(BF16)
    tok = lax.broadcasted_iota(I32, (tb, HW), 0)
    s_idx = lax.broadcasted_iota(I32, (c, HW), 0)

    n_sub = tb // c
    pieces = []
    for j in range(n_sub):
        r0 = j * c
        gj, qj, kj = g[r0:r0 + c], q[r0:r0 + c], kk[r0:r0 + c]
        for t in range(c):
            mask = s_idx <= t
            e = jnp.exp(jnp.where(mask, gj[t:t + 1] - gj, 0.0))
            pieces.append(jnp.where(mask, qj[t:t + 1] * kj * e, 0.0))
    att = _split_dot(jnp.concatenate(pieces, axis=0), head_ones)
    o_intra = jnp.sum(att.reshape(tb, c, HW) * v.reshape(n_sub, 1, c, HW).repeat(c, axis=1).reshape(tb, c, HW), axis=1)
    q_dec = (q * jnp.exp(g)).astype(BF16)

    upds, decays = [], []
    for j in range(n_sub):
        r0 = j * c
        g_last = g[r0 + c - 1:r0 + c]
        in_sub = (tok >= r0) & (tok < r0 + c)
        kd = jnp.where(in_sub, kk * jnp.exp(jnp.where(in_sub, g_last - g, 0.0)), 0.0)
        if tb < LANES:
            kpad_ref[0:tb, :] = kd
            kd = kpad_ref[...]
        upds.append(jnp.where(same_head, _dot(v_t, kd.astype(BF16)), 0.0))
        decays.append(jnp.exp(g_last))

    st = st_ref[...]
    o_inter = []
    for j in range(n_sub):
        o_inter.append(_dot(q_dec[j * c:(j + 1) * c], st.astype(BF16), NT_DIMS))
        st = st * decays[j] + upds[j]
    st_ref[...] = st

    o = (jnp.concatenate(o_inter, axis=0) if n_sub > 1 else o_inter[0]) + o_intra
    ms = _split_dot(o * o, head_ones) * (1.0 / D_VAL)
    o_ref[...] = (o * lax.rsqrt(ms + EPS) * gn_ref[...] * gate).astype(o_ref.dtype)

    @pl.when(i == pl.num_programs(1) - 1)
    def _():
        sfin_ref[...] = st_ref[...]


def hgrn(z_all, row0, nb, t, tb, s0_t, lb, gn):
    nt = t // tb
    r0 = row0 // tb
    has_state = s0_t is not None
    in_specs = [pl.BlockSpec((tb, 4 * HW), lambda b, i: (r0 + b * nt + i, ZD // (4 * HW)))]
    args = [z_all]
    if has_state:
        in_specs.append(pl.BlockSpec((None, HW, HW), lambda b, i: (b, 0, 0)))
        args.append(s0_t)
    in_specs += [pl.BlockSpec((1, HW), lambda b, i: (0, 0)), pl.BlockSpec((1, HW), lambda b, i: (0, 0))]
    args += [lb.reshape(1, HW), jnp.tile(gn, D_HEADS).reshape(1, HW)]
    pad_rows = LANES if tb < LANES else 8
    return pl.pallas_call(
        functools.partial(_hgrn_kernel, tb=tb, has_state=has_state),
        grid=(nb, nt),
        in_specs=in_specs,
        out_specs=[pl.BlockSpec((tb, HW), lambda b, i: (b * nt + i, 0)),
                   pl.BlockSpec((None, HW, HW), lambda b, i: (b, 0, 0))],
        out_shape=[jax.ShapeDtypeStruct((nb * t, HW), BF16), jax.ShapeDtypeStruct((nb, HW, HW), F32)],
        scratch_shapes=[pltpu.VMEM((HW, HW), F32), pltpu.VMEM((pad_rows, HW), F32), pltpu.VMEM((pad_rows, HW), F32)],
        compiler_params=_cparams(("parallel", "arbitrary")),
        name="hgrn",
    )(*args)


def _merge_kernel(oa_ref, obc_ref, od_ref, ga_ref, gb_ref, gc_ref, gd_ref, wa_ref, wb_ref, wc_ref, wd_ref, wo_ref, x_ref, o_ref):
    obc = obc_ref[...]
    merged = jax.nn.sigmoid(ga_ref[...]) * _dot(oa_ref[...], wa_ref[...])
    merged += jax.nn.sigmoid(gb_ref[...]) * _dot(obc[:, :BRANCH_WIDTH], wb_ref[...])
    merged += jax.nn.sigmoid(gc_ref[...]) * _dot(obc[:, BRANCH_WIDTH:], wc_ref[...])
    merged += jax.nn.sigmoid(gd_ref[...]) * _dot(od_ref[...], wd_ref[...])
    o_ref[...] = x_ref[...] + _dot(merged.astype(BF16), wo_ref[...])


def merge(o_a, o_bc, o_d, z_all, wa_p, wb, wc, wd, w_out, x, tm=512):
    n = x.shape[0]
    row = lambda w: pl.BlockSpec((tm, w), lambda i: (i, 0))
    full = lambda a: pl.BlockSpec(a.shape, lambda i: (0, 0))
    gate = lambda k: pl.BlockSpec((tm, D_MODEL), lambda i: (i, ZG // D_MODEL + k))
    return pl.pallas_call(
        _merge_kernel,
        grid=(n // tm,),
        in_specs=[row(o_a.shape[1]), row(o_bc.shape[1]), row(o_d.shape[1]), gate(0), gate(1), gate(2), gate(3),
                  full(wa_p), full(wb), full(wc), full(wd), full(w_out), row(D_MODEL)],
        out_specs=row(D_MODEL),
        out_shape=jax.ShapeDtypeStruct((n, D_MODEL), F32),
        compiler_params=_cparams(("parallel",)),
        name="merge",
    )(o_a, o_bc, o_d, z_all, z_all, z_all, z_all, wa_p, wb, wc, wd, w_out, x)


HP = P_HEADS * 2
SLOTS = P_HEADS * P_TOPK
SUBK = P_DKEY // 2
STAGE1_UNROLL = 8


def _take_max(s, index, n):
    m = jnp.max(s, axis=0, keepdims=True)
    idx = jnp.min(jnp.where(s == m, index, n), axis=0, keepdims=True)
    return m, idx, jnp.where(index == idx, -jnp.inf, s)


_CAND_BLOCKS = [(0, P_TOPK)] + [(a, 8) for a in range(1, 8)]
_CAND_ROWS = sum(nb for _, nb in _CAND_BLOCKS) + 8


def _route_kernel(x_ref, g_ref, wq_ref, keys_ref, h_ref, ii_ref, jj_ref, gw_ref, q_s, sv_s, si_s, oi_s, oj_s, og_s, *, tm):
    x = x_ref[...]
    h = (x * lax.rsqrt(jnp.mean(x * x, axis=-1, keepdims=True) + EPS) * g_ref[...]).astype(BF16)
    h_ref[...] = h
    q_s[...] = _dot(h, wq_ref[...]).astype(BF16)

    half = P_NKEYS // 2
    n_lt = tm // LANES
    iota_lo = lax.broadcasted_iota(I32, (half, LANES), 0).astype(F32)
    iota_hi = iota_lo + float(half)

    def stage1(it, carry):
        hp = it // n_lt
        toks = pl.ds(pl.multiple_of((it % n_lt) * LANES, LANES), LANES)
        q = q_s[toks, pl.ds(pl.multiple_of(hp * SUBK, SUBK), SUBK)]
        s = _dot(keys_ref[hp], q, NT_DIMS)
        a, b = s[:half], s[half:]
        first = a >= b
        cur, cur_i = jnp.where(first, a, b), jnp.where(first, iota_lo, iota_hi)
        nxt, nxt_i = jnp.where(first, b, a), jnp.where(first, iota_hi, iota_lo)
        vals, idxs = [], []
        for _ in range(P_TOPK):
            m = jnp.max(cur, axis=0, keepdims=True)
            idx = jnp.min(jnp.where(cur == m, cur_i, float(P_NKEYS)), axis=0, keepdims=True)
            hit = cur_i == idx
            cur, cur_i, nxt = jnp.where(hit, nxt, cur), jnp.where(hit, nxt_i, cur_i), jnp.where(hit, -jnp.inf, nxt)
            vals.append(m)
            idxs.append(idx)
        sv_s[hp, :, toks] = jnp.concatenate(vals, axis=0)
        si_s[hp, :, toks] = jnp.concatenate(idxs, axis=0)
        return carry

    lax.fori_loop(0, HP * n_lt, stage1, 0, unroll=STAGE1_UNROLL)

    r = lax.broadcasted_iota(I32, (_CAND_ROWS, tm), 0)
    mid = r - P_TOPK
    flat = jnp.where(r < P_TOPK, r, jnp.where(r < _CAND_ROWS - 8, (1 + mid // 8) * P_TOPK + mid % 8, (r - (_CAND_ROWS - 16)) * P_TOPK))
    flat = flat.astype(F32)
    iota_t = lax.broadcasted_iota(I32, (P_TOPK, tm), 0).astype(F32)

    def stage2(hd, carry):
        sv1, sv2 = sv_s[2 * hd], sv_s[2 * hd + 1]
        si1, si2 = si_s[2 * hd], si_s[2 * hd + 1]
        cand = jnp.concatenate([sv1[a:a + 1] + sv2[0:nb] for a, nb in _CAND_BLOCKS] + [sv1[8:P_TOPK] + sv2[0:1]], axis=0)
        fv, ei, ej = [], [], []
        for _ in range(P_TOPK):
            m, idx, cand = _take_max(cand, flat, float(P_TOPK * P_TOPK))
            a = jnp.floor(idx * (1.0 / P_TOPK))
            b = idx - a * P_TOPK
            fv.append(m)
            ei.append(jnp.sum(jnp.where(iota_t == a, si1, 0.0), axis=0, keepdims=True))
            ej.append(jnp.sum(jnp.where(iota_t == b, si2, 0.0), axis=0, keepdims=True))
        fv = jnp.concatenate(fv, axis=0)
        e = jnp.exp(fv - fv[0:1])
        rows = pl.ds(pl.multiple_of(hd * P_TOPK, P_TOPK), P_TOPK)
        og_s[rows, :] = e / jnp.sum(e, axis=0, keepdims=True)
        oi_s[rows, :] = jnp.concatenate(ei, axis=0)
        oj_s[rows, :] = jnp.concatenate(ej, axis=0)
        return carry

    lax.fori_loop(0, P_HEADS, stage2, 0, unroll=2)
    ii_ref[...] = oi_s[...].T.astype(I32)
    jj_ref[...] = oj_s[...].T.astype(I32)
    gw_ref[...] = og_s[...].T


def peer_route(x, g, wq, keys, tm=256):
    n = x.shape[0]
    row = lambda w: pl.BlockSpec((tm, w), lambda i: (i, 0))
    return pl.pallas_call(
        functools.partial(_route_kernel, tm=tm),
        grid=(n // tm,),
        in_specs=[row(D_MODEL), pl.BlockSpec((1, D_MODEL), lambda i: (0, 0)),
                  pl.BlockSpec(wq.shape, lambda i: (0, 0)), pl.BlockSpec(keys.shape, lambda i: (0, 0, 0))],
        out_specs=[row(D_MODEL), row(SLOTS), row(SLOTS), row(SLOTS)],
        out_shape=[jax.ShapeDtypeStruct((n, D_MODEL), BF16), jax.ShapeDtypeStruct((n, SLOTS), I32),
                   jax.ShapeDtypeStruct((n, SLOTS), I32), jax.ShapeDtypeStruct((n, SLOTS), F32)],
        scratch_shapes=[pltpu.VMEM((tm, HP * SUBK), BF16), pltpu.VMEM((HP, P_TOPK, tm), F32), pltpu.VMEM((HP, P_TOPK, tm), F32),
                        pltpu.VMEM((SLOTS, tm), F32), pltpu.VMEM((SLOTS, tm), F32), pltpu.VMEM((SLOTS, tm), F32)],
        compiler_params=_cparams(("parallel",)),
        name="peer_route",
    )(x, g.reshape(1, D_MODEL), wq, keys)


CHUNK_I = 16
CHUNK_E = CHUNK_I * P_NKEYS
N_CHUNKS = P_NKEYS // CHUNK_I


def _peer_kernel(h_ref, ii_ref, jj_ref, gw_ref, u_ref, v_ref, x_ref, o_ref, w_s, acc_s, *, tm):
    c = pl.program_id(1)

    @pl.when(c == 0)
    def _():
        acc_s[...] = jnp.zeros(acc_s.shape, F32)
        iota = lax.broadcasted_iota(I32, (P_NKEYS, SLOTS), 0)

        def build(g, carry):
            t0 = pl.multiple_of(g * PACK, PACK)
            ws = []
            for t in range(PACK):
                irow = ii_ref[pl.ds(t0 + t, 1), :]
                jrow = jj_ref[pl.ds(t0 + t, 1), :]
                grow = gw_ref[pl.ds(t0 + t, 1), :]
                p_t = jnp.where(iota == irow, grow, 0.0).astype(BF16)
                q_t = jnp.where(iota == jrow, 1.0, 0.0).astype(BF16)
                ws.append(_dot(p_t, q_t, NT_DIMS).astype(BF16))
            w_s[:, pl.ds(t0, PACK), :] = pltpu.einshape("tij->itj", jnp.stack(ws, axis=0))
            return carry

        lax.fori_loop(0, tm // PACK, build, 0, unroll=2)

    a = _dot(h_ref[...], u_ref[...], NT_DIMS)
    act = 0.5 * a * (1.0 + lax.erf(a * (1.0 / math.sqrt(2.0))))
    wd = jnp.concatenate([w_s[c * CHUNK_I + r] for r in range(CHUNK_I)], axis=1)
    acc_s[...] += _dot(act.astype(BF16) * wd, v_ref[...])

    @pl.when(c == N_CHUNKS - 1)
    def _():
        o_ref[...] = x_ref[...] + acc_s[...]


def peer_experts(layer, h2, ii, jj, gw, u_tabs, v_tabs, x, tm=512):
    n = x.shape[0]
    row = lambda w: pl.BlockSpec((tm, w), lambda i, c: (i, 0))
    return pl.pallas_call(
        functools.partial(_peer_kernel, tm=tm),
        grid=(n // tm, N_CHUNKS),
        in_specs=[row(D_MODEL), row(SLOTS), row(SLOTS), row(SLOTS),
                  pl.BlockSpec((None, CHUNK_E, D_MODEL), lambda i, c: (layer, c, 0)),
                  pl.BlockSpec((None, CHUNK_E, D_MODEL), lambda i, c: (layer, c, 0)),
                  row(D_MODEL)],
        out_specs=row(D_MODEL),
        out_shape=jax.ShapeDtypeStruct((n, D_MODEL), F32),
        scratch_shapes=[pltpu.VMEM((P_NKEYS, tm, P_NKEYS), BF16), pltpu.VMEM((tm, D_MODEL), F32)],
        compiler_params=_cparams(("parallel", "arbitrary")),
        name="peer_experts",
    )(h2, ii, jj, gw, u_tabs, v_tabs, x)


def _rmsnorm_kernel(x_ref, g_ref, o_ref):
    x = x_ref[...]
    o_ref[...] = x * lax.rsqrt(jnp.mean(x * x, axis=-1, keepdims=True) + EPS) * g_ref[...]


def rmsnorm(x, g, row0, n, tm=512):
    d = x.shape[1]
    r0 = row0 // tm
    return pl.pallas_call(
        _rmsnorm_kernel,
        grid=(n // tm,),
        in_specs=[pl.BlockSpec((tm, d), lambda i: (r0 + i, 0)), pl.BlockSpec((1, d), lambda i: (0, 0))],
        out_specs=pl.BlockSpec((tm, d), lambda i: (i, 0)),
        out_shape=jax.ShapeDtypeStruct((n, d), F32),
        compiler_params=_cparams(("parallel",)),
        name="rmsnorm",
    )(x, g.reshape(1, d))


def _rope_tables():
    half = A_ROPE // 2
    inv = ROPE_THETA ** (-jnp.arange(half, dtype=F32) / half)
    pos = jnp.concatenate([jnp.tile(jnp.arange(SEQ), BATCH), jnp.tile(PAST_LEN + jnp.arange(DEC_SEQ), DEC_BATCH)])
    ang = pos.astype(F32)[:, None] * inv[None, :]
    cos, sin = jnp.cos(ang), jnp.sin(ang)
    zeros = lambda w: jnp.zeros((NT, w), F32)
    ccq = jnp.concatenate([jnp.ones((NT, A_NOPE), F32), cos, cos, zeros(HEAD_PAD - A_NOPE - A_ROPE)], axis=1)
    ssq = jnp.concatenate([zeros(A_NOPE), -sin, sin, zeros(HEAD_PAD - A_NOPE - A_ROPE)], axis=1)
    cck = jnp.concatenate([cos, cos, zeros(LANES - A_ROPE)], axis=1)
    ssk = jnp.concatenate([-sin, sin, zeros(LANES - A_ROPE)], axis=1)
    return ccq, ssq, cck, ssk


def _swap_halves(w):
    half = w.shape[-1] // 2
    return jnp.concatenate([w[..., half:], w[..., :half]], axis=-1)


def _layer_weights(l, w_in, a_w_uq, a_w_uk, a_w_uv, w_branch, w_out, p_w_q, p_sub_keys):
    w = w_in[l]
    o = 0
    parts = []
    for size in (A_Q_LORA, A_KV_LORA, A_ROPE, 2 * BRANCH_WIDTH, 3 * BRANCH_WIDTH, 4 * HW, N_BRANCH * D_MODEL):
        parts.append(w[:, o:o + size])
        o += size
    cq, ckv, kr, b_in, c_in, d_in, gate = parts
    pad = jnp.zeros((D_MODEL, ZB - A_Q_LORA - A_KV_LORA - 2 * A_ROPE), F32)
    w_in_p = jnp.concatenate([cq, ckv, kr, _swap_halves(kr), pad, b_in, d_in, gate, c_in], axis=1).astype(BF16)

    wq = a_w_uq[l].reshape(A_Q_LORA, A_HEADS, A_NOPE + A_ROPE)
    nope, rope = wq[..., :A_NOPE], wq[..., A_NOPE:]
    tail = jnp.zeros((A_Q_LORA, A_HEADS, HEAD_PAD - A_NOPE - A_ROPE), F32)
    wq_main = jnp.concatenate([nope, rope, tail], axis=-1).reshape(A_Q_LORA, -1).astype(BF16)
    wq_swap = jnp.concatenate([jnp.zeros_like(nope), _swap_halves(rope), tail], axis=-1).reshape(A_Q_LORA, -1).astype(BF16)

    w_uk, w_uv = a_w_uk[l], a_w_uv[l]
    head_tail = jnp.zeros((A_KV_LORA, A_HEADS, HEAD_PAD - A_NOPE), F32)
    wuk_p = jnp.concatenate([w_uk, head_tail], axis=-1).reshape(A_KV_LORA, -1).astype(BF16)
    wuv_p = jnp.concatenate([w_uv, head_tail], axis=-1).reshape(A_KV_LORA, -1).astype(BF16)

    r = jnp.arange(LANES)[:, None]
    col = jnp.arange(A_HEADS * HEAD_PAD)[None, :]
    place = ((col % HEAD_PAD == A_NOPE + r) & (r < A_ROPE)).astype(BF16)

    blk = jnp.zeros((A_HEADS, HEAD_PAD, 2 * LANES), F32)
    blk = blk.at[:, :A_NOPE, :A_KV_LORA].set(jnp.transpose(w_uk, (1, 2, 0)))
    blk = blk.at[:, A_NOPE:A_NOPE + A_ROPE, A_KV_LORA:A_KV_LORA + A_ROPE].set(jnp.eye(A_ROPE, dtype=F32))
    eye_h = jnp.eye(A_HEADS, dtype=F32)
    wabs = (blk[:, :, None, :] * eye_h[:, None, :, None]).reshape(A_HEADS * HEAD_PAD, A_HEADS * 2 * LANES).astype(BF16)

    wb = w_branch[l]
    wa_p = jnp.concatenate([wb[0].reshape(A_HEADS, A_V, D_MODEL), jnp.zeros((A_HEADS, HEAD_PAD - A_V, D_MODEL), F32)],
                           axis=1).reshape(A_HEADS * HEAD_PAD, D_MODEL).astype(BF16)
    return dict(w_in_p=w_in_p, wq_main=wq_main, wq_swap=wq_swap, wuk_p=wuk_p, wuv_p=wuv_p, place=place, wabs=wabs,
                wa_p=wa_p, wb=wb[1].astype(BF16), wc=wb[2].astype(BF16), wd=wb[3].astype(BF16), w_out=w_out[l].astype(BF16),
                wq=p_w_q[l].astype(BF16), keys=p_sub_keys[l].reshape(HP, P_NKEYS, SUBK).astype(BF16))


def _state_to_kernel(s):
    t = jnp.transpose(s, (0, 1, 3, 2))
    eye_h = jnp.eye(D_HEADS, dtype=F32)
    return (t[:, :, :, None, :] * eye_h[None, :, None, :, None]).reshape(s.shape[0], HW, HW)


def _state_from_kernel(s_t):
    s5 = s_t.reshape(s_t.shape[0], D_HEADS, D_VAL, D_HEADS, D_KEY)
    diag = jnp.stack([s5[:, h, :, h, :] for h in range(D_HEADS)], axis=1)
    return jnp.transpose(diag, (0, 1, 3, 2))


def kernel(x_prompt, x_sample, cache_ckv, cache_krope, page_table, state_conv_b, state_conv_c, state_hgrn, norm1_g, w_in, a_q_norm_g, a_w_uq, a_kv_norm_g, a_w_uk, a_w_uv, b_conv_w, b_conv_b, b_ln_g, b_ln_b, c_conv_w, d_lower_bound, d_gnorm_g, w_branch, w_out, norm2_g, p_w_q, p_sub_keys, p_u, p_v, final_norm_g):
    lb_soft = jax.nn.softmax(d_lower_bound.astype(F32), axis=0)
    lower_bounds = jnp.cumsum(lb_soft, axis=0) - lb_soft[0:1]
    ccq, ssq, cck, ssk = _rope_tables()
    x = jnp.concatenate([x_prompt.reshape(NP, D_MODEL), x_sample.reshape(NS, D_MODEL)], axis=0)
    cache_krope_t = jnp.swapaxes(cache_krope, 2, 3)
    u_tabs, v_tabs = p_u.astype(BF16), p_v.astype(BF16)

    states_p, states_s = [], []
    for l in range(DEPTH):
        w = _layer_weights(l, w_in, a_w_uq, a_w_uk, a_w_uv, w_branch, w_out, p_w_q, p_sub_keys)
        z = norm_matmul(x, norm1_g[l], w["w_in_p"], 1024, 2304)

        ckv, kr, qp, kp, vp = mla_prep(z, a_q_norm_g[l], a_kv_norm_g[l], w["wq_main"], w["wq_swap"], w["wuk_p"], w["wuv_p"],
                                       w["place"], ccq, ssq, cck, ssk)
        oa_p = flash_prompt(qp, kp, vp)
        qlat = matmul(qp[NP:], w["wabs"], 512, BF16)
        qlat = jnp.transpose(qlat.reshape(DEC_BATCH, DEC_SEQ, A_HEADS, 2 * LANES), (0, 2, 1, 3)).reshape(DEC_BATCH, DEC_ROWS, 2 * LANES)
        oa_s = mla_decode(l, page_table, qlat, cache_ckv, cache_krope_t, ckv[NP:].reshape(DEC_BATCH, DEC_SEQ, LANES),
                          kr[NP:].reshape(DEC_BATCH, DEC_SEQ, LANES), w["wuv_p"])
        o_a = jnp.concatenate([oa_p, oa_s.reshape(NS, -1)], axis=0)

        conv_w = (b_conv_w[l], b_conv_b[l], b_ln_g[l], b_ln_b[l], c_conv_w[l])
        obc_p, hb_p, hc_p = conv_branches(z, 0, BATCH, SEQ, 512, None, None, *conv_w)
        obc_s, hb_s, hc_s = conv_branches(z, NP, DEC_BATCH, DEC_SEQ, DEC_SEQ, state_conv_b[l], state_conv_c[l], *conv_w)
        o_bc = jnp.concatenate([obc_p, obc_s], axis=0)

        od_p, st_p = hgrn(z, 0, BATCH, SEQ, LANES, None, lower_bounds[l], d_gnorm_g[l])
        od_s, st_s = hgrn(z, NP, DEC_BATCH, DEC_SEQ, DEC_SEQ, _state_to_kernel(state_hgrn[l]), lower_bounds[l], d_gnorm_g[l])
        o_d = jnp.concatenate([od_p, od_s], axis=0)

        x1 = merge(o_a, o_bc, o_d, z, w["wa_p"], w["wb"], w["wc"], w["wd"], w["w_out"], x)
        h2, ii, jj, gw = peer_route(x1, norm2_g[l], w["wq"], w["keys"])
        x = peer_experts(l, h2, ii, jj, gw, u_tabs, v_tabs, x1)

        states_p.append((ckv[:NP].reshape(BATCH, SEQ, A_KV_LORA), kr[:NP, :A_ROPE].reshape(BATCH, SEQ, A_ROPE),
                         hb_p, hc_p, _state_from_kernel(st_p)))
        states_s.append((ckv[NP:].reshape(DEC_BATCH, DEC_SEQ, A_KV_LORA), kr[NP:, :A_ROPE].reshape(DEC_BATCH, DEC_SEQ, A_ROPE),
                         hb_s, hc_s, _state_from_kernel(st_s)))

    y_p = rmsnorm(x, final_norm_g, 0, NP)
    y_s = rmsnorm(x, final_norm_g, NP, NS)
    stack = lambda states: [jnp.stack([s[i] for s in states], axis=0) for i in range(5)]
    return (y_p.reshape(BATCH, SEQ, D_MODEL), y_s.reshape(DEC_BATCH, DEC_SEQ, D_MODEL), *stack(states_p), *stack(states_s))
```

```python
import functools
import math

import jax
import jax.numpy as jnp
from jax import lax
from jax.experimental import pallas as pl
from jax.experimental.pallas import tpu as pltpu

F32 = jnp.float32
BF16 = jnp.bfloat16
I32 = jnp.int32

D_MODEL = 1024
BATCH = 2
SEQ = 8192
DEPTH = 2
DEC_BATCH = 128
DEC_SEQ = 8
PAST_LEN = 8192
PAGE_SIZE = 128
N_PAGES = PAST_LEN // PAGE_SIZE
BRANCH_WIDTH = 256
N_BRANCH = 4
A_HEADS = 4
A_NOPE = 64
A_ROPE = 32
A_V = 64
A_Q_LORA = 256
A_KV_LORA = 128
ROPE_THETA = 10000.0
B_KERNEL = 31
C_KERNEL = 3
D_HEADS = 4
D_KEY = 64
D_VAL = 64
P_HEADS = 8
P_NKEYS = 128
P_DKEY = 256
P_TOPK = 16
P_EXPERTS = P_NKEYS * P_NKEYS
EPS = 1e-6
NEG_BIG = -1e30

NP = BATCH * SEQ
NS = DEC_BATCH * DEC_SEQ
NT = NP + NS

LANES = 128
SUBLANES = 8
PACK = 16
HEAD_PAD = 128
ZA, ZB, ZD, ZG, ZC = 0, 512, 1024, 2048, 6144
Z_COLS = 6912
VMEM_LIMIT = 56 * 1024 * 1024

HI = lax.Precision.HIGHEST


def _cparams(sem):
    return pltpu.CompilerParams(dimension_semantics=sem, vmem_limit_bytes=VMEM_LIMIT)


def _dot(a, b, dims=(((1,), (0,)), ((), ())), precision=None):
    return lax.dot_general(a, b, dims, precision=precision, preferred_element_type=F32)


NT_DIMS = (((1,), (1,)), ((), ()))
TN_DIMS = (((0,), (0,)), ((), ()))


def _norm_matmul_kernel(x_ref, g_ref, w_ref, o_ref, h_ref):
    @pl.when(pl.program_id(1) == 0)
    def _():
        x = x_ref[...]
        y = x * lax.rsqrt(jnp.mean(x * x, axis=-1, keepdims=True) + EPS)
        h_ref[...] = (y * g_ref[...]).astype(BF16)

    o_ref[...] = _dot(h_ref[...], w_ref[...])


def norm_matmul(x, g, w, tm, tn):
    n, d = x.shape
    cols = w.shape[1]
    return pl.pallas_call(
        _norm_matmul_kernel,
        grid=(n // tm, cols // tn),
        in_specs=[
            pl.BlockSpec((tm, d), lambda i, j: (i, 0)),
            pl.BlockSpec((1, d), lambda i, j: (0, 0)),
            pl.BlockSpec((d, tn), lambda i, j: (0, j)),
        ],
        out_specs=pl.BlockSpec((tm, tn), lambda i, j: (i, j)),
        out_shape=jax.ShapeDtypeStruct((n, cols), F32),
        scratch_shapes=[pltpu.VMEM((tm, d), BF16)],
        compiler_params=_cparams(("parallel", "arbitrary")),
        name="norm_matmul",
    )(x, g.reshape(1, d), w)


def _matmul_kernel(x_ref, w_ref, o_ref):
    o_ref[...] = _dot(x_ref[...], w_ref[...]).astype(o_ref.dtype)


def matmul(x, w, tm, out_dtype):
    n, d = x.shape
    cols = w.shape[1]
    return pl.pallas_call(
        _matmul_kernel,
        grid=(n // tm,),
        in_specs=[pl.BlockSpec((tm, d), lambda i: (i, 0)), pl.BlockSpec((d, cols), lambda i: (0, 0))],
        out_specs=pl.BlockSpec((tm, cols), lambda i: (i, 0)),
        out_shape=jax.ShapeDtypeStruct((n, cols), out_dtype),
        compiler_params=_cparams(("parallel",)),
        name="matmul",
    )(x, w)


def _tile4(t):
    return jnp.concatenate([t, t, t, t], axis=1)


def _mla_prep_kernel(z_ref, qn_ref, kvn_ref, wqm_ref, wqs_ref, wuk_ref, wuv_ref, place_ref,
                     ccq_ref, ssq_ref, cck_ref, ssk_ref,
                     ckv_ref, kr_ref, q_ref, k_ref, v_ref):
    z = z_ref[...]
    cq = z[:, 0:A_Q_LORA]
    cqn = cq * lax.rsqrt(jnp.mean(cq * cq, axis=-1, keepdims=True) + EPS) * qn_ref[...]
    cqn = cqn.astype(BF16)
    scale = 1.0 / math.sqrt(A_NOPE + A_ROPE)
    q = _dot(cqn, wqm_ref[...]) * _tile4(ccq_ref[...]) + _dot(cqn, wqs_ref[...]) * _tile4(ssq_ref[...])
    q_ref[...] = (q * scale).astype(BF16)

    c = z[:, A_Q_LORA:A_Q_LORA + A_KV_LORA]
    ckv = c * lax.rsqrt(jnp.mean(c * c, axis=-1, keepdims=True) + EPS) * kvn_ref[...]
    ckv_ref[...] = ckv
    ckv_b = ckv.astype(BF16)

    kc = z[:, A_Q_LORA + A_KV_LORA:]
    kr = kc * cck_ref[...] + pltpu.roll(kc, LANES - A_ROPE, axis=1) * ssk_ref[...]
    kr_ref[...] = kr

    k = _dot(ckv_b, wuk_ref[...]) + _dot(kr.astype(BF16), place_ref[...])
    k_ref[...] = k.astype(BF16)
    v_ref[...] = _dot(ckv_b, wuv_ref[...]).astype(BF16)


def mla_prep(z_all, qn_g, kvn_g, wq_main, wq_swap, wuk_p, wuv_p, place, ccq, ssq, cck, ssk, tm=512):
    n = z_all.shape[0]
    hp = A_HEADS * HEAD_PAD
    row = lambda w: pl.BlockSpec((tm, w), lambda i: (i, 0))
    full = lambda a: pl.BlockSpec(a.shape, lambda i: (0,) * a.ndim)
    qn_g = qn_g.reshape(1, -1)
    kvn_g = kvn_g.reshape(1, -1)
    return pl.pallas_call(
        _mla_prep_kernel,
        grid=(n // tm,),
        in_specs=[row(512), full(qn_g), full(kvn_g), full(wq_main), full(wq_swap), full(wuk_p), full(wuv_p), full(place),
                  row(LANES), row(LANES), row(LANES), row(LANES)],
        out_specs=[row(LANES), row(LANES), row(hp), row(hp), row(hp)],
        out_shape=[jax.ShapeDtypeStruct((n, LANES), F32), jax.ShapeDtypeStruct((n, LANES), F32),
                   jax.ShapeDtypeStruct((n, hp), BF16), jax.ShapeDtypeStruct((n, hp), BF16),
                   jax.ShapeDtypeStruct((n, hp), BF16)],
        compiler_params=_cparams(("parallel",)),
        name="mla_prep",
    )(z_all, qn_g, kvn_g, wq_main, wq_swap, wuk_p, wuv_p, place, ccq, ssq, cck, ssk)


def _flash_kernel(q_ref, k_ref, v_ref, o_ref, *, tq, tk):
    i = pl.program_id(2)
    q = q_ref[...]

    def step(j, carry, masked):
        m, l, acc = carry
        kj = k_ref[pl.ds(pl.multiple_of(j * tk, tk), tk), :]
        vj = v_ref[pl.ds(pl.multiple_of(j * tk, tk), tk), :]
        s = _dot(q, kj, NT_DIMS)
        if masked:
            qpos = i * tq + lax.broadcasted_iota(I32, (tq, tk), 0)
            kpos = j * tk + lax.broadcasted_iota(I32, (tq, tk), 1)
            s = jnp.where(kpos <= qpos, s, NEG_BIG)
        m_new = jnp.maximum(m, jnp.max(s, axis=-1, keepdims=True))
        alpha = jnp.exp(m - m_new)
        p = jnp.exp(s - m_new)
        l = alpha * l + jnp.sum(p, axis=-1, keepdims=True)
        acc = alpha * acc + _dot(p.astype(BF16), vj)
        return m_new, l, acc

    init = (jnp.full((tq, 1), NEG_BIG, F32), jnp.zeros((tq, 1), F32), jnp.zeros((tq, HEAD_PAD), F32))
    n_full = (i * tq) // tk
    carry = lax.fori_loop(0, n_full // 2, lambda p, c: step(2 * p + 1, step(2 * p, c, False), False), init)
    carry = lax.fori_loop(0, n_full % 2, lambda _, c: step(n_full - 1, c, False), carry)
    for d in range(tq // tk):
        carry = step(n_full + d, carry, True)
    m, l, acc = carry
    o_ref[...] = (acc / l).astype(o_ref.dtype)


def flash_prompt(qp, kp, vp, tq=1024, tk=1024):
    assert tq % tk == 0
    nq = SEQ // tq
    return pl.pallas_call(
        functools.partial(_flash_kernel, tq=tq, tk=tk),
        grid=(BATCH, A_HEADS, nq),
        in_specs=[
            pl.BlockSpec((tq, HEAD_PAD), lambda b, h, i: (b * nq + i, h)),
            pl.BlockSpec((SEQ, HEAD_PAD), lambda b, h, i: (b, h)),
            pl.BlockSpec((SEQ, HEAD_PAD), lambda b, h, i: (b, h)),
        ],
        out_specs=pl.BlockSpec((tq, HEAD_PAD), lambda b, h, i: (b * nq + i, h)),
        out_shape=jax.ShapeDtypeStruct((NP, A_HEADS * HEAD_PAD), BF16),
        compiler_params=_cparams(("parallel", "parallel", "arbitrary")),
        name="flash_prompt",
    )(qp, kp, vp)


DEC_ROWS = A_HEADS * DEC_SEQ


def _decode_kernel(pt_ref, q_ref, cnew_ref, knew_ref, wuv_ref, ckv_hbm, kr_hbm, o_ref, kbuf, krbuf, sem, *, page0):
    b = pl.program_id(0)
    slot = b % 2

    def page_copies(seq, sl, r):
        page = page0 + pt_ref[seq, r]
        return (pltpu.make_async_copy(ckv_hbm.at[page], kbuf.at[sl, r], sem.at[0, sl]),
                pltpu.make_async_copy(kr_hbm.at[page], krbuf.at[sl, r], sem.at[1, sl]))

    def start_fetch(seq, sl):
        def body(r, carry):
            for cp in page_copies(seq, sl, r):
                cp.start()
            return carry
        lax.fori_loop(0, N_PAGES, body, 0)

    @pl.when(b == 0)
    def _():
        start_fetch(0, 0)

    @pl.when(b + 1 < pl.num_programs(0))
    def _():
        start_fetch(b + 1, 1 - slot)

    def wait_body(r, carry):
        for cp in page_copies(b, slot, r):
            cp.wait()
        return carry

    lax.fori_loop(0, N_PAGES, wait_body, 0)

    q = q_ref[...]
    q_lat = q[:, :A_KV_LORA]
    q_rope = q[:, A_KV_LORA:A_KV_LORA + A_ROPE]
    kc = kbuf[slot].reshape(PAST_LEN, A_KV_LORA).astype(BF16)
    kr_t = jnp.concatenate([krbuf[slot, r] for r in range(N_PAGES)], axis=1).astype(BF16)
    s_past = _dot(q_lat, kc, NT_DIMS) + _dot(q_rope, kr_t)

    cn = cnew_ref[...].astype(BF16)
    kn = knew_ref[...][:, :A_ROPE].astype(BF16)
    s_new = _dot(q_lat, cn, NT_DIMS) + _dot(q_rope, kn, NT_DIMS)
    t_q = lax.broadcasted_iota(I32, s_new.shape, 0) % DEC_SEQ
    t_k = lax.broadcasted_iota(I32, s_new.shape, 1)
    s_new = jnp.where(t_k <= t_q, s_new, NEG_BIG)

    m = jnp.maximum(jnp.max(s_past, axis=-1, keepdims=True), jnp.max(s_new, axis=-1, keepdims=True))
    p_past = jnp.exp(s_past - m)
    p_new = jnp.exp(s_new - m)
    l = jnp.sum(p_past, axis=-1, keepdims=True) + jnp.sum(p_new, axis=-1, keepdims=True)
    acc = _dot(p_past.astype(BF16), kc) + _dot(p_new.astype(BF16), cn)
    o_lat = (acc / l).astype(BF16)
    for h in range(A_HEADS):
        o_ref[:, h * HEAD_PAD:(h + 1) * HEAD_PAD] = _dot(
            o_lat[h * DEC_SEQ:(h + 1) * DEC_SEQ, :], wuv_ref[:, h * HEAD_PAD:(h + 1) * HEAD_PAD]).astype(o_ref.dtype)


def mla_decode(layer, page_table, qlat, cache_ckv, cache_krope_t, ckv_new, kr_new, wuv_p):
    n_pool = cache_ckv.shape[1]
    hbm = pl.BlockSpec(memory_space=pl.ANY)
    grid_spec = pltpu.PrefetchScalarGridSpec(
        num_scalar_prefetch=1,
        grid=(DEC_BATCH,),
        in_specs=[
            pl.BlockSpec((None, DEC_ROWS, 2 * LANES), lambda b, pt: (b, 0, 0)),
            pl.BlockSpec((None, DEC_SEQ, LANES), lambda b, pt: (b, 0, 0)),
            pl.BlockSpec((None, DEC_SEQ, LANES), lambda b, pt: (b, 0, 0)),
            pl.BlockSpec(wuv_p.shape, lambda b, pt: (0, 0)),
            hbm, hbm,
        ],
        out_specs=pl.BlockSpec((None, DEC_SEQ, A_HEADS * HEAD_PAD), lambda b, pt: (b, 0, 0)),
        scratch_shapes=[pltpu.VMEM((2, N_PAGES, PAGE_SIZE, A_KV_LORA), F32), pltpu.VMEM((2, N_PAGES, A_ROPE, PAGE_SIZE), F32),
                        pltpu.SemaphoreType.DMA((2, 2))],
    )
    return pl.pallas_call(
        functools.partial(_decode_kernel, page0=layer * n_pool),
        grid_spec=grid_spec,
        out_shape=jax.ShapeDtypeStruct((DEC_BATCH, DEC_SEQ, A_HEADS * HEAD_PAD), BF16),
        compiler_params=_cparams(("arbitrary",)),
        name="mla_decode",
    )(page_table, qlat, ckv_new, kr_new, wuv_p, cache_ckv.reshape(-1, PAGE_SIZE, A_KV_LORA), cache_krope_t.reshape(-1, A_ROPE, PAGE_SIZE))


CTX_B = 32
CTX_C = 8
SEQS_PER_STEP = 8


def _conv_kernel(*refs, tt, nbb, has_hist):
    if has_hist:
        zb_ref, zc_ref, hb_ref, hc_ref = refs[:4]
        refs = refs[4:]
    else:
        zb_ref, zc_ref = refs[:2]
        refs = refs[2:]
    bw_ref, bb_ref, lg_ref, lb_ref, cw_ref, o_ref, nhb_ref, nhc_ref, xb_ref, xc_ref = refs
    i = pl.program_id(1)
    hb_rows, hc_rows = B_KERNEL - 1, C_KERNEL - 1

    @pl.when(i == 0)
    def _():
        xb_ref[:, 0:CTX_B, :] = jnp.zeros((nbb, CTX_B, BRANCH_WIDTH), F32)
        xc_ref[:, 0:CTX_C, :] = jnp.zeros((nbb, CTX_C, BRANCH_WIDTH), F32)
        if has_hist:
            xb_ref[:, CTX_B - hb_rows:CTX_B, :] = hb_ref[...]
            xc_ref[:, CTX_C - hc_rows:CTX_C, :] = hc_ref[...]

    zb = zb_ref[...]
    zc = zc_ref[...]
    glu = zb[:, :BRANCH_WIDTH] * jax.nn.sigmoid(zb[:, BRANCH_WIDTH:])
    u_c = zc[:, BRANCH_WIDTH:2 * BRANCH_WIDTH] * zc[:, 2 * BRANCH_WIDTH:]
    accs_b, accs_c = [], []
    for s in range(nbb):
        xb_ref[s, CTX_B:CTX_B + tt, :] = glu[s * tt:(s + 1) * tt]
        xc_ref[s, CTX_C:CTX_C + tt, :] = u_c[s * tt:(s + 1) * tt]
        acc = jnp.zeros((tt, BRANCH_WIDTH), F32)
        for k in range(B_KERNEL):
            acc = acc + xb_ref[s, pl.ds(CTX_B - hb_rows + k, tt), :] * bw_ref[k:k + 1, :]
        accs_b.append(acc)
        acc = jnp.zeros((tt, BRANCH_WIDTH), F32)
        for k in range(C_KERNEL):
            acc = acc + xc_ref[s, pl.ds(CTX_C - hc_rows + k, tt), :] * cw_ref[k:k + 1, :]
        accs_c.append(acc)
    cat = lambda parts: parts[0] if nbb == 1 else jnp.concatenate(parts, axis=0)
    y = cat(accs_b) + bb_ref[...]
    mu = jnp.mean(y, axis=-1, keepdims=True)
    yc = y - mu
    var = jnp.mean(yc * yc, axis=-1, keepdims=True)
    o_b = jax.nn.silu(yc * lax.rsqrt(var + EPS) * lg_ref[...] + lb_ref[...])
    o_ref[...] = jnp.concatenate([o_b, zc[:, :BRANCH_WIDTH] * cat(accs_c)], axis=1).astype(o_ref.dtype)

    new_b = xb_ref[:, tt:tt + CTX_B, :]
    new_c = xc_ref[:, tt:tt + CTX_C, :]
    xb_ref[:, 0:CTX_B, :] = new_b
    xc_ref[:, 0:CTX_C, :] = new_c

    @pl.when(i == pl.num_programs(1) - 1)
    def _():
        nhb_ref[...] = new_b[:, CTX_B - hb_rows:, :]
        nhc_ref[...] = new_c[:, CTX_C - hc_rows:, :]


def conv_branches(z_all, row0, nb, t, tt, nbb, hist_b, hist_c, b_cw, b_cb, ln_g, ln_b, c_cw):
    assert nbb == 1 or tt == t
    nt = t // tt
    rows = nbb * tt
    r0 = row0 // rows
    has_hist = hist_b is not None
    w = BRANCH_WIDTH
    in_specs = [
        pl.BlockSpec((rows, 2 * w), lambda b, i: (r0 + b * nt + i, ZB // (2 * w))),
        pl.BlockSpec((rows, 3 * w), lambda b, i: (r0 + b * nt + i, ZC // (3 * w))),
    ]
    args = [z_all, z_all]
    if has_hist:
        in_specs += [pl.BlockSpec((nbb, B_KERNEL - 1, w), lambda b, i: (b, 0, 0)),
                     pl.BlockSpec((nbb, C_KERNEL - 1, w), lambda b, i: (b, 0, 0))]
        args += [hist_b, hist_c]
    params = [b_cw, b_cb.reshape(1, w), ln_g.reshape(1, w), ln_b.reshape(1, w), c_cw]
    in_specs += [pl.BlockSpec(p.shape, lambda b, i: (0, 0)) for p in params]
    return pl.pallas_call(
        functools.partial(_conv_kernel, tt=tt, nbb=nbb, has_hist=has_hist),
        grid=(nb // nbb, nt),
        in_specs=in_specs,
        out_specs=[pl.BlockSpec((rows, 2 * w), lambda b, i: (b * nt + i, 0)),
                   pl.BlockSpec((nbb, B_KERNEL - 1, w), lambda b, i: (b, 0, 0)),
                   pl.BlockSpec((nbb, C_KERNEL - 1, w), lambda b, i: (b, 0, 0))],
        out_shape=[jax.ShapeDtypeStruct((nb * t, 2 * w), BF16),
                   jax.ShapeDtypeStruct((nb, B_KERNEL - 1, w), F32),
                   jax.ShapeDtypeStruct((nb, C_KERNEL - 1, w), F32)],
        scratch_shapes=[pltpu.VMEM((nbb, CTX_B + tt, w), F32), pltpu.VMEM((nbb, CTX_C + tt, w), F32)],
        compiler_params=_cparams(("parallel", "arbitrary")),
        name="conv_branches",
    )(*args, *params)


HW = D_HEADS * D_KEY
SUB = 16


def _split_dot(a, b_bf16):
    hi = a.astype(BF16)
    lo = (a - hi.astype(F32)).astype(BF16)
    return _dot(hi, b_bf16) + _dot(lo, b_bf16)


def _hgrn_kernel(*refs, tb, seqs, has_state):
    if has_state:
        zd_ref, s0_ref = refs[:2]
        refs = refs[2:]
    else:
        zd_ref = refs[0]
        refs = refs[1:]
    lb_ref, gn_ref, o_ref, sfin_ref, st_ref = refs
    i = pl.program_id(1)
    c = min(SUB, tb // seqs)
    n_sub = tb // c
    assert seqs == 1 or (seqs == n_sub and has_state)

    if seqs == 1:
        @pl.when(i == 0)
        def _():
            st_ref[...] = s0_ref[0] if has_state else jnp.zeros(st_ref.shape, F32)

    zd = zd_ref[...]
    lb = lb_ref[...]
    q = jax.nn.silu(zd[:, 0:HW])
    f = lb + (1.0 - lb) * jax.nn.sigmoid(zd[:, HW:2 * HW])
    logf = jnp.log(f)
    kk = 1.0 - f
    v = zd[:, 2 * HW:3 * HW]
    gate = jax.nn.silu(zd[:, 3 * HW:4 * HW])

    row = lax.broadcasted_iota(I32, (tb, tb), 0)
    col = lax.broadcasted_iota(I32, (tb, tb), 1)
    tril_sub = ((row // c == col // c) & (col <= row)).astype(F32)
    g = _dot(tril_sub, logf, precision=HI)

    hrow = lax.broadcasted_iota(I32, (HW, HW), 0) // D_KEY
    hcol = lax.broadcasted_iota(I32, (HW, HW), 1) // D_KEY
    same_head = hrow == hcol
    head_ones = same_head.astype(BF16)

    def pad_tokens(a):
        return a if tb >= LANES else jnp.concatenate([a, jnp.zeros((LANES - tb, HW), a.dtype)], axis=0)

    v_t = pad_tokens(v).T.astype(BF16)
    tok = lax.broadcasted_iota(I32, (tb, HW), 0)
    s_idx = lax.broadcasted_iota(I32, (c, HW), 0)

    pieces = []
    for j in range(n_sub):
        r0 = j * c
        gj, qj, kj = g[r0:r0 + c], q[r0:r0 + c], kk[r0:r0 + c]
        for t in range(c):
            mask = s_idx <= t
            e = jnp.exp(jnp.where(mask, gj[t:t + 1] - gj, 0.0))
            pieces.append(jnp.where(mask, qj[t:t + 1] * kj * e, 0.0))
    att = _split_dot(jnp.concatenate(pieces, axis=0), head_ones)
    o_intra = jnp.sum(att.reshape(tb, c, HW) * v.reshape(n_sub, 1, c, HW).repeat(c, axis=1).reshape(tb, c, HW), axis=1)
    q_dec = (q * jnp.exp(g)).astype(BF16)

    upds, decays = [], []
    for j in range(n_sub):
        r0 = j * c
        g_last = g[r0 + c - 1:r0 + c]
        in_sub = (tok >= r0) & (tok < r0 + c)
        kd = jnp.where(in_sub, kk * jnp.exp(jnp.where(in_sub, g_last - g, 0.0)), 0.0)
        upds.append(jnp.where(same_head, _dot(v_t, pad_tokens(kd).astype(BF16)), 0.0))
        decays.append(jnp.exp(g_last))

    o_inter = []
    if seqs == 1:
        st = st_ref[...]
        for j in range(n_sub):
            o_inter.append(_dot(q_dec[j * c:(j + 1) * c], st.astype(BF16), NT_DIMS))
            st = st * decays[j] + upds[j]
        st_ref[...] = st

        @pl.when(i == pl.num_programs(1) - 1)
        def _():
            sfin_ref[0] = st
    else:
        for j in range(n_sub):
            st = s0_ref[j]
            o_inter.append(_dot(q_dec[j * c:(j + 1) * c], st.astype(BF16), NT_DIMS))
            sfin_ref[j] = st * decays[j] + upds[j]

    o = (jnp.concatenate(o_inter, axis=0) if n_sub > 1 else o_inter[0]) + o_intra
    ms = _split_dot(o * o, head_ones) * (1.0 / D_VAL)
    o_ref[...] = (o * lax.rsqrt(ms + EPS) * gn_ref[...] * gate).astype(o_ref.dtype)


def hgrn(z_all, row0, nb, t, tb, seqs, s0_t, lb, gn):
    assert (seqs == 1 and t % tb == 0) or tb == seqs * t
    nt = max(t // tb, 1)
    r0 = row0 // tb
    has_state = s0_t is not None
    in_specs = [pl.BlockSpec((tb, 4 * HW), lambda b, i: (r0 + b * nt + i, ZD // (4 * HW)))]
    args = [z_all]
    if has_state:
        in_specs.append(pl.BlockSpec((seqs, HW, HW), lambda b, i: (b, 0, 0)))
        args.append(s0_t)
    in_specs += [pl.BlockSpec((1, HW), lambda b, i: (0, 0)), pl.BlockSpec((1, HW), lambda b, i: (0, 0))]
    args += [lb.reshape(1, HW), jnp.tile(gn, D_HEADS).reshape(1, HW)]
    return pl.pallas_call(
        functools.partial(_hgrn_kernel, tb=tb, seqs=seqs, has_state=has_state),
        grid=(nb // seqs, nt),
        in_specs=in_specs,
        out_specs=[pl.BlockSpec((tb, HW), lambda b, i: (b * nt + i, 0)),
                   pl.BlockSpec((seqs, HW, HW), lambda b, i: (b, 0, 0))],
        out_shape=[jax.ShapeDtypeStruct((nb * t, HW), BF16), jax.ShapeDtypeStruct((nb, HW, HW), F32)],
        scratch_shapes=[pltpu.VMEM((HW, HW), F32)],
        compiler_params=_cparams(("parallel", "arbitrary")),
        name="hgrn",
    )(*args)


def _merge_kernel(oa_ref, obc_ref, od_ref, ga_ref, gb_ref, gc_ref, gd_ref, wa_ref, wb_ref, wc_ref, wd_ref, wo_ref, x_ref, o_ref):
    obc = obc_ref[...]
    merged = jax.nn.sigmoid(ga_ref[...]) * _dot(oa_ref[...], wa_ref[...])
    merged += jax.nn.sigmoid(gb_ref[...]) * _dot(obc[:, :BRANCH_WIDTH], wb_ref[...])
    merged += jax.nn.sigmoid(gc_ref[...]) * _dot(obc[:, BRANCH_WIDTH:], wc_ref[...])
    merged += jax.nn.sigmoid(gd_ref[...]) * _dot(od_ref[...], wd_ref[...])
    o_ref[...] = x_ref[...] + _dot(merged.astype(BF16), wo_ref[...])


def merge(o_a, o_bc, o_d, z_all, wa_p, wb, wc, wd, w_out, x, tm=512):
    n = x.shape[0]
    row = lambda w: pl.BlockSpec((tm, w), lambda i: (i, 0))
    full = lambda a: pl.BlockSpec(a.shape, lambda i: (0, 0))
    gate = lambda k: pl.BlockSpec((tm, D_MODEL), lambda i: (i, ZG // D_MODEL + k))
    return pl.pallas_call(
        _merge_kernel,
        grid=(n // tm,),
        in_specs=[row(o_a.shape[1]), row(o_bc.shape[1]), row(o_d.shape[1]), gate(0), gate(1), gate(2), gate(3),
                  full(wa_p), full(wb), full(wc), full(wd), full(w_out), row(D_MODEL)],
        out_specs=row(D_MODEL),
        out_shape=jax.ShapeDtypeStruct((n, D_MODEL), F32),
        compiler_params=_cparams(("parallel",)),
        name="merge",
    )(o_a, o_bc, o_d, z_all, z_all, z_all, z_all, wa_p, wb, wc, wd, w_out, x)


HP = P_HEADS * 2
SLOTS = P_HEADS * P_TOPK
SUBK = P_DKEY // 2
STAGE1_UNROLL = 8


def _take_max(s, index, n):
    m = jnp.max(s, axis=0, keepdims=True)
    idx = jnp.min(jnp.where(s == m, index, n), axis=0, keepdims=True)
    return m, idx, jnp.where(index == idx, -jnp.inf, s)


_CAND_BLOCKS = [(0, P_TOPK)] + [(a, 8) for a in range(1, 8)]
_CAND_ROWS = sum(nb for _, nb in _CAND_BLOCKS) + 8


def _route_kernel(x_ref, g_ref, wq_ref, keys_ref, h_ref, ii_ref, jj_ref, gw_ref, q_s, sv_s, si_s, oi_s, oj_s, og_s, *, tm):
    x = x_ref[...]
    h = (x * lax.rsqrt(jnp.mean(x * x, axis=-1, keepdims=True) + EPS) * g_ref[...]).astype(BF16)
    h_ref[...] = h
    q_s[...] = _dot(h, wq_ref[...]).astype(BF16)

    half = P_NKEYS // 2
    n_lt = tm // LANES
    iota_lo = lax.broadcasted_iota(I32, (half, LANES), 0).astype(F32)
    iota_hi = iota_lo + float(half)

    def stage1(it, carry):
        hp = it // n_lt
        toks = pl.ds(pl.multiple_of((it % n_lt) * LANES, LANES), LANES)
        q = q_s[toks, pl.ds(pl.multiple_of(hp * SUBK, SUBK), SUBK)]
        s = _dot(keys_ref[hp], q, NT_DIMS)
        a, b = s[:half], s[half:]
        first = a >= b
        cur, cur_i = jnp.where(first, a, b), jnp.where(first, iota_lo, iota_hi)
        nxt, nxt_i = jnp.where(first, b, a), jnp.where(first, iota_hi, iota_lo)
        vals, idxs = [], []
        for _ in range(P_TOPK):
            m = jnp.max(cur, axis=0, keepdims=True)
            idx = jnp.min(jnp.where(cur == m, cur_i, float(P_NKEYS)), axis=0, keepdims=True)
            hit = cur_i == idx
            cur, cur_i, nxt = jnp.where(hit, nxt, cur), jnp.where(hit, nxt_i, cur_i), jnp.where(hit, -jnp.inf, nxt)
            vals.append(m)
            idxs.append(idx)
        sv_s[hp, :, toks] = jnp.concatenate(vals, axis=0)
        si_s[hp, :, toks] = jnp.concatenate(idxs, axis=0)
        return carry

    lax.fori_loop(0, HP * n_lt, stage1, 0, unroll=STAGE1_UNROLL)

    r = lax.broadcasted_iota(I32, (_CAND_ROWS, tm), 0)
    mid = r - P_TOPK
    flat = jnp.where(r < P_TOPK, r, jnp.where(r < _CAND_ROWS - 8, (1 + mid // 8) * P_TOPK + mid % 8, (r - (_CAND_ROWS - 16)) * P_TOPK))
    flat = flat.astype(F32)
    iota_t = lax.broadcasted_iota(I32, (P_TOPK, tm), 0).astype(F32)

    def stage2(hd, carry):
        sv1, sv2 = sv_s[2 * hd], sv_s[2 * hd + 1]
        si1, si2 = si_s[2 * hd], si_s[2 * hd + 1]
        cand = jnp.concatenate([sv1[a:a + 1] + sv2[0:nb] for a, nb in _CAND_BLOCKS] + [sv1[8:P_TOPK] + sv2[0:1]], axis=0)
        fv, ei, ej = [], [], []
        for _ in range(P_TOPK):
            m, idx, cand = _take_max(cand, flat, float(P_TOPK * P_TOPK))
            a = jnp.floor(idx * (1.0 / P_TOPK))
            b = idx - a * P_TOPK
            fv.append(m)
            ei.append(jnp.sum(jnp.where(iota_t == a, si1, 0.0), axis=0, keepdims=True))
            ej.append(jnp.sum(jnp.where(iota_t == b, si2, 0.0), axis=0, keepdims=True))
        fv = jnp.concatenate(fv, axis=0)
        e = jnp.exp(fv - fv[0:1])
        rows = pl.ds(pl.multiple_of(hd * P_TOPK, P_TOPK), P_TOPK)
        og_s[rows, :] = e / jnp.sum(e, axis=0, keepdims=True)
        oi_s[rows, :] = jnp.concatenate(ei, axis=0)
        oj_s[rows, :] = jnp.concatenate(ej, axis=0)
        return carry

    lax.fori_loop(0, P_HEADS, stage2, 0, unroll=2)
    ii_ref[...] = oi_s[...].T.astype(I32)
    jj_ref[...] = oj_s[...].T.astype(I32)
    gw_ref[...] = og_s[...].T


def peer_route(x, g, wq, keys, tm=256):
    n = x.shape[0]
    row = lambda w: pl.BlockSpec((tm, w), lambda i: (i, 0))
    return pl.pallas_call(
        functools.partial(_route_kernel, tm=tm),
        grid=(n // tm,),
        in_specs=[row(D_MODEL), pl.BlockSpec((1, D_MODEL), lambda i: (0, 0)),
                  pl.BlockSpec(wq.shape, lambda i: (0, 0)), pl.BlockSpec(keys.shape, lambda i: (0, 0, 0))],
        out_specs=[row(D_MODEL), row(SLOTS), row(SLOTS), row(SLOTS)],
        out_shape=[jax.ShapeDtypeStruct((n, D_MODEL), BF16), jax.ShapeDtypeStruct((n, SLOTS), I32),
                   jax.ShapeDtypeStruct((n, SLOTS), I32), jax.ShapeDtypeStruct((n, SLOTS), F32)],
        scratch_shapes=[pltpu.VMEM((tm, HP * SUBK), BF16), pltpu.VMEM((HP, P_TOPK, tm), F32), pltpu.VMEM((HP, P_TOPK, tm), F32),
                        pltpu.VMEM((SLOTS, tm), F32), pltpu.VMEM((SLOTS, tm), F32), pltpu.VMEM((SLOTS, tm), F32)],
        compiler_params=_cparams(("parallel",)),
        name="peer_route",
    )(x, g.reshape(1, D_MODEL), wq, keys)


CHUNK_I = 16
CHUNK_E = CHUNK_I * P_NKEYS
N_CHUNKS = P_NKEYS // CHUNK_I


def _peer_kernel(h_ref, ii_ref, jj_ref, gw_ref, u_ref, v_ref, x_ref, o_ref, w_s, acc_s, *, tm):
    c = pl.program_id(1)

    @pl.when(c == 0)
    def _():
        acc_s[...] = jnp.zeros(acc_s.shape, F32)
        iota = lax.broadcasted_iota(I32, (P_NKEYS, SLOTS), 0)

        def build(g, carry):
            t0 = pl.multiple_of(g * PACK, PACK)
            ws = []
            for t in range(PACK):
                irow = ii_ref[pl.ds(t0 + t, 1), :]
                jrow = jj_ref[pl.ds(t0 + t, 1), :]
                grow = gw_ref[pl.ds(t0 + t, 1), :]
                p_t = jnp.where(iota == irow, grow, 0.0).astype(BF16)
                q_t = jnp.where(iota == jrow, 1.0, 0.0).astype(BF16)
                ws.append(_dot(p_t, q_t, NT_DIMS).astype(BF16))
            w_s[:, pl.ds(t0, PACK), :] = pltpu.einshape("tij->itj", jnp.stack(ws, axis=0))
            return carry

        lax.fori_loop(0, tm // PACK, build, 0, unroll=2)

    a = _dot(h_ref[...], u_ref[...], NT_DIMS)
    act = 0.5 * a * (1.0 + lax.erf(a * (1.0 / math.sqrt(2.0))))
    wd = jnp.concatenate([w_s[c * CHUNK_I + r] for r in range(CHUNK_I)], axis=1)
    acc_s[...] += _dot(act.astype(BF16) * wd, v_ref[...])

    @pl.when(c == N_CHUNKS - 1)
    def _():
        o_ref[...] = x_ref[...] + acc_s[...]


def peer_experts(layer, h2, ii, jj, gw, u_tabs, v_tabs, x, tm=512):
    n = x.shape[0]
    row = lambda w: pl.BlockSpec((tm, w), lambda i, c: (i, 0))
    return pl.pallas_call(
        functools.partial(_peer_kernel, tm=tm),
        grid=(n // tm, N_CHUNKS),
        in_specs=[row(D_MODEL), row(SLOTS), row(SLOTS), row(SLOTS),
                  pl.BlockSpec((None, CHUNK_E, D_MODEL), lambda i, c: (layer, c, 0)),
                  pl.BlockSpec((None, CHUNK_E, D_MODEL), lambda i, c: (layer, c, 0)),
                  row(D_MODEL)],
        out_specs=row(D_MODEL),
        out_shape=jax.ShapeDtypeStruct((n, D_MODEL), F32),
        scratch_shapes=[pltpu.VMEM((P_NKEYS, tm, P_NKEYS), BF16), pltpu.VMEM((tm, D_MODEL), F32)],
        compiler_params=_cparams(("parallel", "arbitrary")),
        name="peer_experts",
    )(h2, ii, jj, gw, u_tabs, v_tabs, x)


def _rmsnorm_kernel(x_ref, g_ref, o_ref):
    x = x_ref[...]
    o_ref[...] = x * lax.rsqrt(jnp.mean(x * x, axis=-1, keepdims=True) + EPS) * g_ref[...]


def rmsnorm(x, g, row0, n, tm=512):
    d = x.shape[1]
    r0 = row0 // tm
    return pl.pallas_call(
        _rmsnorm_kernel,
        grid=(n // tm,),
        in_specs=[pl.BlockSpec((tm, d), lambda i: (r0 + i, 0)), pl.BlockSpec((1, d), lambda i: (0, 0))],
        out_specs=pl.BlockSpec((tm, d), lambda i: (i, 0)),
        out_shape=jax.ShapeDtypeStruct((n, d), F32),
        compiler_params=_cparams(("parallel",)),
        name="rmsnorm",
    )(x, g.reshape(1, d))


def _rope_tables():
    half = A_ROPE // 2
    inv = ROPE_THETA ** (-jnp.arange(half, dtype=F32) / half)
    pos = jnp.concatenate([jnp.tile(jnp.arange(SEQ), BATCH), jnp.tile(PAST_LEN + jnp.arange(DEC_SEQ), DEC_BATCH)])
    posf = pos.astype(F32)[:, None]

    def tables(start):
        zeros = jnp.zeros((LANES - start - A_ROPE,), F32)
        freq = jnp.concatenate([jnp.zeros((start,), F32), inv, inv, zeros])
        keep = jnp.concatenate([jnp.ones((start + A_ROPE,), F32), zeros])
        sign = jnp.concatenate([jnp.zeros((start,), F32), -jnp.ones((half,), F32), jnp.ones((half,), F32), zeros])
        ang = posf * freq[None, :]
        return jnp.cos(ang) * keep[None, :], jnp.sin(ang) * sign[None, :]

    ccq, ssq = tables(A_NOPE)
    cck, ssk = tables(0)
    return ccq, ssq, cck, ssk


def _swap_halves(w):
    half = w.shape[-1] // 2
    return jnp.concatenate([w[..., half:], w[..., :half]], axis=-1)


def _layer_weights(l, w_in, a_w_uq, a_w_uk, a_w_uv, w_branch, w_out, p_w_q, p_sub_keys):
    w = w_in[l]
    o = 0
    parts = []
    for size in (A_Q_LORA, A_KV_LORA, A_ROPE, 2 * BRANCH_WIDTH, 3 * BRANCH_WIDTH, 4 * HW, N_BRANCH * D_MODEL):
        parts.append(w[:, o:o + size])
        o += size
    cq, ckv, kr, b_in, c_in, d_in, gate = parts
    pad = jnp.zeros((D_MODEL, ZB - A_Q_LORA - A_KV_LORA - 2 * A_ROPE), F32)
    w_in_p = jnp.concatenate([cq, ckv, kr, _swap_halves(kr), pad, b_in, d_in, gate, c_in], axis=1).astype(BF16)

    wq = a_w_uq[l].reshape(A_Q_LORA, A_HEADS, A_NOPE + A_ROPE)
    nope, rope = wq[..., :A_NOPE], wq[..., A_NOPE:]
    tail = jnp.zeros((A_Q_LORA, A_HEADS, HEAD_PAD - A_NOPE - A_ROPE), F32)
    wq_main = jnp.concatenate([nope, rope, tail], axis=-1).reshape(A_Q_LORA, -1).astype(BF16)
    wq_swap = jnp.concatenate([jnp.zeros_like(nope), _swap_halves(rope), tail], axis=-1).reshape(A_Q_LORA, -1).astype(BF16)

    w_uk, w_uv = a_w_uk[l], a_w_uv[l]
    head_tail = jnp.zeros((A_KV_LORA, A_HEADS, HEAD_PAD - A_NOPE), F32)
    wuk_p = jnp.concatenate([w_uk, head_tail], axis=-1).reshape(A_KV_LORA, -1).astype(BF16)
    wuv_p = jnp.concatenate([w_uv, head_tail], axis=-1).reshape(A_KV_LORA, -1).astype(BF16)

    r = jnp.arange(LANES)[:, None]
    col = jnp.arange(A_HEADS * HEAD_PAD)[None, :]
    place = ((col % HEAD_PAD == A_NOPE + r) & (r < A_ROPE)).astype(BF16)

    blk = jnp.zeros((A_HEADS, HEAD_PAD, 2 * LANES), F32)
    blk = blk.at[:, :A_NOPE, :A_KV_LORA].set(jnp.transpose(w_uk, (1, 2, 0)))
    blk = blk.at[:, A_NOPE:A_NOPE + A_ROPE, A_KV_LORA:A_KV_LORA + A_ROPE].set(jnp.eye(A_ROPE, dtype=F32))
    eye_h = jnp.eye(A_HEADS, dtype=F32)
    wabs = (blk[:, :, None, :] * eye_h[:, None, :, None]).reshape(A_HEADS * HEAD_PAD, A_HEADS * 2 * LANES).astype(BF16)

    wb = w_branch[l]
    wa_p = jnp.concatenate([wb[0].reshape(A_HEADS, A_V, D_MODEL), jnp.zeros((A_HEADS, HEAD_PAD - A_V, D_MODEL), F32)],
                           axis=1).reshape(A_HEADS * HEAD_PAD, D_MODEL).astype(BF16)
    return dict(w_in_p=w_in_p, wq_main=wq_main, wq_swap=wq_swap, wuk_p=wuk_p, wuv_p=wuv_p, place=place, wabs=wabs,
                wa_p=wa_p, wb=wb[1].astype(BF16), wc=wb[2].astype(BF16), wd=wb[3].astype(BF16), w_out=w_out[l].astype(BF16),
                wq=p_w_q[l].astype(BF16), keys=p_sub_keys[l].reshape(HP, P_NKEYS, SUBK).astype(BF16))


def _state_to_kernel(s):
    t = jnp.transpose(s, (0, 1, 3, 2))
    eye_h = jnp.eye(D_HEADS, dtype=F32)
    return (t[:, :, :, None, :] * eye_h[None, :, None, :, None]).reshape(s.shape[0], HW, HW)


def _state_from_kernel(s_t):
    s5 = s_t.reshape(s_t.shape[0], D_HEADS, D_VAL, D_HEADS, D_KEY)
    diag = jnp.stack([s5[:, h, :, h, :] for h in range(D_HEADS)], axis=1)
    return jnp.transpose(diag, (0, 1, 3, 2))


def kernel(x_prompt, x_sample, cache_ckv, cache_krope, page_table, state_conv_b, state_conv_c, state_hgrn, norm1_g, w_in, a_q_norm_g, a_w_uq, a_kv_norm_g, a_w_uk, a_w_uv, b_conv_w, b_conv_b, b_ln_g, b_ln_b, c_conv_w, d_lower_bound, d_gnorm_g, w_branch, w_out, norm2_g, p_w_q, p_sub_keys, p_u, p_v, final_norm_g):
    lb_soft = jax.nn.softmax(d_lower_bound.astype(F32), axis=0)
    lower_bounds = jnp.cumsum(lb_soft, axis=0) - lb_soft[0:1]
    ccq, ssq, cck, ssk = _rope_tables()
    x = jnp.concatenate([x_prompt.reshape(NP, D_MODEL), x_sample.reshape(NS, D_MODEL)], axis=0)
    cache_krope_t = jnp.swapaxes(cache_krope, 2, 3)
    u_tabs, v_tabs = p_u.astype(BF16), p_v.astype(BF16)

    states_p, states_s = [], []
    for l in range(DEPTH):
        w = _layer_weights(l, w_in, a_w_uq, a_w_uk, a_w_uv, w_branch, w_out, p_w_q, p_sub_keys)
        z = norm_matmul(x, norm1_g[l], w["w_in_p"], 1024, 2304)

        ckv, kr, qp, kp, vp = mla_prep(z, a_q_norm_g[l], a_kv_norm_g[l], w["wq_main"], w["wq_swap"], w["wuk_p"], w["wuv_p"],
                                       w["place"], ccq, ssq, cck, ssk)
        oa_p = flash_prompt(qp, kp, vp)
        qlat = matmul(qp[NP:], w["wabs"], 512, BF16)
        qlat = jnp.transpose(qlat.reshape(DEC_BATCH, DEC_SEQ, A_HEADS, 2 * LANES), (0, 2, 1, 3)).reshape(DEC_BATCH, DEC_ROWS, 2 * LANES)
        oa_s = mla_decode(l, page_table, qlat, cache_ckv, cache_krope_t, ckv[NP:].reshape(DEC_BATCH, DEC_SEQ, LANES),
                          kr[NP:].reshape(DEC_BATCH, DEC_SEQ, LANES), w["wuv_p"])
        o_a = jnp.concatenate([oa_p, oa_s.reshape(NS, -1)], axis=0)

        conv_w = (b_conv_w[l], b_conv_b[l], b_ln_g[l], b_ln_b[l], c_conv_w[l])
        obc_p, hb_p, hc_p = conv_branches(z, 0, BATCH, SEQ, 512, 1, None, None, *conv_w)
        obc_s, hb_s, hc_s = conv_branches(z, NP, DEC_BATCH, DEC_SEQ, DEC_SEQ, SEQS_PER_STEP, state_conv_b[l], state_conv_c[l], *conv_w)
        o_bc = jnp.concatenate([obc_p, obc_s], axis=0)

        od_p, st_p = hgrn(z, 0, BATCH, SEQ, LANES, 1, None, lower_bounds[l], d_gnorm_g[l])
        od_s, st_s = hgrn(z, NP, DEC_BATCH, DEC_SEQ, SEQS_PER_STEP * DEC_SEQ, SEQS_PER_STEP, _state_to_kernel(state_hgrn[l]), lower_bounds[l], d_gnorm_g[l])
        o_d = jnp.concatenate([od_p, od_s], axis=0)

        x1 = merge(o_a, o_bc, o_d, z, w["wa_p"], w["wb"], w["wc"], w["wd"], w["w_out"], x)
        h2, ii, jj, gw = peer_route(x1, norm2_g[l], w["wq"], w["keys"])
        x = peer_experts(l, h2, ii, jj, gw, u_tabs, v_tabs, x1)

        states_p.append((ckv[:NP].reshape(BATCH, SEQ, A_KV_LORA), kr[:NP, :A_ROPE].reshape(BATCH, SEQ, A_ROPE),
                         hb_p, hc_p, _state_from_kernel(st_p)))
        states_s.append((ckv[NP:].reshape(DEC_BATCH, DEC_SEQ, A_KV_LORA), kr[NP:, :A_ROPE].reshape(DEC_BATCH, DEC_SEQ, A_ROPE),
                         hb_s, hc_s, _state_from_kernel(st_s)))

    y_p = rmsnorm(x, final_norm_g, 0, NP)
    y_s = rmsnorm(x, final_norm_g, NP, NS)
    stack = lambda states: [jnp.stack([s[i] for s in states], axis=0) for i in range(5)]
    return (y_p.reshape(BATCH, SEQ, D_MODEL), y_s.reshape(DEC_BATCH, DEC_SEQ, D_MODEL), *stack(states_p), *stack(states_s))
```

```python
import functools
import math

import jax
import jax.numpy as jnp
from jax import lax
from jax.experimental import pallas as pl
from jax.experimental.pallas import tpu as pltpu

F32 = jnp.float32
BF16 = jnp.bfloat16
I32 = jnp.int32

D_MODEL = 1024
BATCH = 2
SEQ = 8192
DEPTH = 2
DEC_BATCH = 128
DEC_SEQ = 8
PAST_LEN = 8192
PAGE_SIZE = 128
N_PAGES = PAST_LEN // PAGE_SIZE
BRANCH_WIDTH = 256
N_BRANCH = 4
A_HEADS = 4
A_NOPE = 64
A_ROPE = 32
A_V = 64
A_Q_LORA = 256
A_KV_LORA = 128
ROPE_THETA = 10000.0
B_KERNEL = 31
C_KERNEL = 3
D_HEADS = 4
D_KEY = 64
D_VAL = 64
P_HEADS = 8
P_NKEYS = 128
P_DKEY = 256
P_TOPK = 16
P_EXPERTS = P_NKEYS * P_NKEYS
EPS = 1e-6
NEG_BIG = -1e30

NP = BATCH * SEQ
NS = DEC_BATCH * DEC_SEQ
NT = NP + NS

LANES = 128
SUBLANES = 8
PACK = 16
HEAD_PAD = 128
ZA, ZB, ZD, ZC = 0, 512, 1024, 2304
Z_COLS = 3072
G_COLS = N_BRANCH * D_MODEL
VMEM_LIMIT = 56 * 1024 * 1024

HI = lax.Precision.HIGHEST


def _cparams(sem):
    return pltpu.CompilerParams(dimension_semantics=sem, vmem_limit_bytes=VMEM_LIMIT)


def _dot(a, b, dims=(((1,), (0,)), ((), ())), precision=None):
    return lax.dot_general(a, b, dims, precision=precision, preferred_element_type=F32)


NT_DIMS = (((1,), (1,)), ((), ()))
TN_DIMS = (((0,), (0,)), ((), ()))


def _norm_matmul_kernel(x_ref, g_ref, w_ref, o_ref, h_ref):
    @pl.when(pl.program_id(1) == 0)
    def _():
        x = x_ref[...]
        y = x * lax.rsqrt(jnp.mean(x * x, axis=-1, keepdims=True) + EPS)
        h_ref[...] = (y * g_ref[...]).astype(BF16)

    o_ref[...] = _dot(h_ref[...], w_ref[...]).astype(o_ref.dtype)


def norm_matmul(x, g, w, tm, tn, out_dtype):
    n, d = x.shape
    cols = w.shape[1]
    return pl.pallas_call(
        _norm_matmul_kernel,
        grid=(n // tm, cols // tn),
        in_specs=[
            pl.BlockSpec((tm, d), lambda i, j: (i, 0)),
            pl.BlockSpec((1, d), lambda i, j: (0, 0)),
            pl.BlockSpec((d, tn), lambda i, j: (0, j)),
        ],
        out_specs=pl.BlockSpec((tm, tn), lambda i, j: (i, j)),
        out_shape=jax.ShapeDtypeStruct((n, cols), out_dtype),
        scratch_shapes=[pltpu.VMEM((tm, d), BF16)],
        compiler_params=_cparams(("parallel", "arbitrary")),
        name="norm_matmul",
    )(x, g.reshape(1, d), w)


def _matmul_kernel(x_ref, w_ref, o_ref):
    o_ref[...] = _dot(x_ref[...], w_ref[...]).astype(o_ref.dtype)


def matmul(x, w, tm, out_dtype):
    n, d = x.shape
    cols = w.shape[1]
    return pl.pallas_call(
        _matmul_kernel,
        grid=(n // tm,),
        in_specs=[pl.BlockSpec((tm, d), lambda i: (i, 0)), pl.BlockSpec((d, cols), lambda i: (0, 0))],
        out_specs=pl.BlockSpec((tm, cols), lambda i: (i, 0)),
        out_shape=jax.ShapeDtypeStruct((n, cols), out_dtype),
        compiler_params=_cparams(("parallel",)),
        name="matmul",
    )(x, w)


def _tile4(t):
    return jnp.concatenate([t, t, t, t], axis=1)


def _mla_prep_kernel(z_ref, qn_ref, kvn_ref, wqm_ref, wqs_ref, wuk_ref, wuv_ref, place_ref,
                     ccq_ref, ssq_ref, cck_ref, ssk_ref,
                     ckv_ref, kr_ref, q_ref, k_ref, v_ref):
    z = z_ref[...]
    cq = z[:, 0:A_Q_LORA]
    cqn = cq * lax.rsqrt(jnp.mean(cq * cq, axis=-1, keepdims=True) + EPS) * qn_ref[...]
    cqn = cqn.astype(BF16)
    scale = 1.0 / math.sqrt(A_NOPE + A_ROPE)
    q = _dot(cqn, wqm_ref[...]) * _tile4(ccq_ref[...]) + _dot(cqn, wqs_ref[...]) * _tile4(ssq_ref[...])
    q_ref[...] = (q * scale).astype(BF16)

    c = z[:, A_Q_LORA:A_Q_LORA + A_KV_LORA]
    ckv = c * lax.rsqrt(jnp.mean(c * c, axis=-1, keepdims=True) + EPS) * kvn_ref[...]
    ckv_ref[...] = ckv
    ckv_b = ckv.astype(BF16)

    kc = z[:, A_Q_LORA + A_KV_LORA:]
    kr = kc * cck_ref[...] + pltpu.roll(kc, LANES - A_ROPE, axis=1) * ssk_ref[...]
    kr_ref[...] = kr

    k = _dot(ckv_b, wuk_ref[...]) + _dot(kr.astype(BF16), place_ref[...])
    k_ref[...] = k.astype(BF16)
    v_ref[...] = _dot(ckv_b, wuv_ref[...]).astype(BF16)


def mla_prep(z_all, qn_g, kvn_g, wq_main, wq_swap, wuk_p, wuv_p, place, ccq, ssq, cck, ssk, tm=512):
    n = z_all.shape[0]
    hp = A_HEADS * HEAD_PAD
    row = lambda w: pl.BlockSpec((tm, w), lambda i: (i, 0))
    full = lambda a: pl.BlockSpec(a.shape, lambda i: (0,) * a.ndim)
    qn_g = qn_g.reshape(1, -1)
    kvn_g = kvn_g.reshape(1, -1)
    return pl.pallas_call(
        _mla_prep_kernel,
        grid=(n // tm,),
        in_specs=[row(512), full(qn_g), full(kvn_g), full(wq_main), full(wq_swap), full(wuk_p), full(wuv_p), full(place),
                  row(LANES), row(LANES), row(LANES), row(LANES)],
        out_specs=[row(LANES), row(LANES), row(hp), row(hp), row(hp)],
        out_shape=[jax.ShapeDtypeStruct((n, LANES), F32), jax.ShapeDtypeStruct((n, LANES), F32),
                   jax.ShapeDtypeStruct((n, hp), BF16), jax.ShapeDtypeStruct((n, hp), BF16),
                   jax.ShapeDtypeStruct((n, hp), BF16)],
        compiler_params=_cparams(("parallel",)),
        name="mla_prep",
    )(z_all, qn_g, kvn_g, wq_main, wq_swap, wuk_p, wuv_p, place, ccq, ssq, cck, ssk)


def _flash_kernel(q_ref, k_ref, v_ref, o_ref, *, tq, tk):
    i = pl.program_id(2)
    q = q_ref[...]

    def step(j, carry, masked):
        m, l, acc = carry
        kj = k_ref[pl.ds(pl.multiple_of(j * tk, tk), tk), :]
        vj = v_ref[pl.ds(pl.multiple_of(j * tk, tk), tk), :]
        s = _dot(q, kj, NT_DIMS)
        if masked:
            qpos = i * tq + lax.broadcasted_iota(I32, (tq, tk), 0)
            kpos = j * tk + lax.broadcasted_iota(I32, (tq, tk), 1)
            s = jnp.where(kpos <= qpos, s, NEG_BIG)
        m_new = jnp.maximum(m, jnp.max(s, axis=-1, keepdims=True))
        alpha = jnp.exp(m - m_new)
        p = jnp.exp(s - m_new)
        l = alpha * l + jnp.sum(p, axis=-1, keepdims=True)
        acc = alpha * acc + _dot(p.astype(BF16), vj)
        return m_new, l, acc

    init = (jnp.full((tq, 1), NEG_BIG, F32), jnp.zeros((tq, 1), F32), jnp.zeros((tq, HEAD_PAD), F32))
    n_full = (i * tq) // tk
    carry = lax.fori_loop(0, n_full // 2, lambda p, c: step(2 * p + 1, step(2 * p, c, False), False), init)
    carry = lax.fori_loop(0, n_full % 2, lambda _, c: step(n_full - 1, c, False), carry)
    for d in range(tq // tk):
        carry = step(n_full + d, carry, True)
    m, l, acc = carry
    o_ref[...] = (acc / l).astype(o_ref.dtype)


def flash_prompt(qp, kp, vp, tq=1024, tk=1024):
    assert tq % tk == 0
    nq = SEQ // tq
    return pl.pallas_call(
        functools.partial(_flash_kernel, tq=tq, tk=tk),
        grid=(BATCH, A_HEADS, nq),
        in_specs=[
            pl.BlockSpec((tq, HEAD_PAD), lambda b, h, i: (b * nq + i, h)),
            pl.BlockSpec((SEQ, HEAD_PAD), lambda b, h, i: (b, h)),
            pl.BlockSpec((SEQ, HEAD_PAD), lambda b, h, i: (b, h)),
        ],
        out_specs=pl.BlockSpec((tq, HEAD_PAD), lambda b, h, i: (b * nq + i, h)),
        out_shape=jax.ShapeDtypeStruct((NP, A_HEADS * HEAD_PAD), BF16),
        compiler_params=_cparams(("parallel", "parallel", "arbitrary")),
        name="flash_prompt",
    )(qp, kp, vp)


DEC_ROWS = A_HEADS * DEC_SEQ


def _decode_kernel(pt_ref, q_ref, cnew_ref, knew_ref, wuv_ref, ckv_hbm, kr_hbm, o_ref, kbuf, krbuf, sem, *, page0):
    b = pl.program_id(0)
    slot = b % 2

    def page_copies(seq, sl, r):
        page = page0 + pt_ref[seq, r]
        return (pltpu.make_async_copy(ckv_hbm.at[page], kbuf.at[sl, r], sem.at[0, sl]),
                pltpu.make_async_copy(kr_hbm.at[page], krbuf.at[sl, r], sem.at[1, sl]))

    def start_fetch(seq, sl):
        def body(r, carry):
            for cp in page_copies(seq, sl, r):
                cp.start()
            return carry
        lax.fori_loop(0, N_PAGES, body, 0)

    @pl.when(b == 0)
    def _():
        start_fetch(0, 0)

    @pl.when(b + 1 < pl.num_programs(0))
    def _():
        start_fetch(b + 1, 1 - slot)

    def wait_body(r, carry):
        for cp in page_copies(b, slot, r):
            cp.wait()
        return carry

    lax.fori_loop(0, N_PAGES, wait_body, 0)

    q = q_ref[...]
    q_lat = q[:, :A_KV_LORA]
    q_rope = q[:, A_KV_LORA:A_KV_LORA + A_ROPE]
    kc = kbuf[slot].reshape(PAST_LEN, A_KV_LORA).astype(BF16)
    kr_t = jnp.concatenate([krbuf[slot, r] for r in range(N_PAGES)], axis=1).astype(BF16)
    s_past = _dot(q_lat, kc, NT_DIMS) + _dot(q_rope, kr_t)

    cn = cnew_ref[...].astype(BF16)
    kn = knew_ref[...][:, :A_ROPE].astype(BF16)
    s_new = _dot(q_lat, cn, NT_DIMS) + _dot(q_rope, kn, NT_DIMS)
    t_q = lax.broadcasted_iota(I32, s_new.shape, 0) % DEC_SEQ
    t_k = lax.broadcasted_iota(I32, s_new.shape, 1)
    s_new = jnp.where(t_k <= t_q, s_new, NEG_BIG)

    m = jnp.maximum(jnp.max(s_past, axis=-1, keepdims=True), jnp.max(s_new, axis=-1, keepdims=True))
    p_past = jnp.exp(s_past - m)
    p_new = jnp.exp(s_new - m)
    l = jnp.sum(p_past, axis=-1, keepdims=True) + jnp.sum(p_new, axis=-1, keepdims=True)
    acc = _dot(p_past.astype(BF16), kc) + _dot(p_new.astype(BF16), cn)
    o_lat = (acc / l).astype(BF16)
    for h in range(A_HEADS):
        o_ref[:, h * HEAD_PAD:(h + 1) * HEAD_PAD] = _dot(
            o_lat[h * DEC_SEQ:(h + 1) * DEC_SEQ, :], wuv_ref[:, h * HEAD_PAD:(h + 1) * HEAD_PAD]).astype(o_ref.dtype)


def mla_decode(layer, page_table, qlat, cache_ckv, cache_krope_t, ckv_new, kr_new, wuv_p):
    n_pool = cache_ckv.shape[1]
    hbm = pl.BlockSpec(memory_space=pl.ANY)
    grid_spec = pltpu.PrefetchScalarGridSpec(
        num_scalar_prefetch=1,
        grid=(DEC_BATCH,),
        in_specs=[
            pl.BlockSpec((None, DEC_ROWS, 2 * LANES), lambda b, pt: (b, 0, 0)),
            pl.BlockSpec((None, DEC_SEQ, LANES), lambda b, pt: (b, 0, 0)),
            pl.BlockSpec((None, DEC_SEQ, LANES), lambda b, pt: (b, 0, 0)),
            pl.BlockSpec(wuv_p.shape, lambda b, pt: (0, 0)),
            hbm, hbm,
        ],
        out_specs=pl.BlockSpec((None, DEC_SEQ, A_HEADS * HEAD_PAD), lambda b, pt: (b, 0, 0)),
        scratch_shapes=[pltpu.VMEM((2, N_PAGES, PAGE_SIZE, A_KV_LORA), F32), pltpu.VMEM((2, N_PAGES, A_ROPE, PAGE_SIZE), F32),
                        pltpu.SemaphoreType.DMA((2, 2))],
    )
    return pl.pallas_call(
        functools.partial(_decode_kernel, page0=layer * n_pool),
        grid_spec=grid_spec,
        out_shape=jax.ShapeDtypeStruct((DEC_BATCH, DEC_SEQ, A_HEADS * HEAD_PAD), BF16),
        compiler_params=_cparams(("arbitrary",)),
        name="mla_decode",
    )(page_table, qlat, ckv_new, kr_new, wuv_p, cache_ckv.reshape(-1, PAGE_SIZE, A_KV_LORA), cache_krope_t.reshape(-1, A_ROPE, PAGE_SIZE))


CTX_B = 32
CTX_C = 8
SEQS_PER_STEP = 8


def _conv_kernel(*refs, tt, nbb, has_hist):
    if has_hist:
        zb_ref, zc_ref, hb_ref, hc_ref = refs[:4]
        refs = refs[4:]
    else:
        zb_ref, zc_ref = refs[:2]
        refs = refs[2:]
    bw_ref, bb_ref, lg_ref, lb_ref, cw_ref, o_ref, nhb_ref, nhc_ref, xb_ref, xc_ref = refs
    i = pl.program_id(1)
    hb_rows, hc_rows = B_KERNEL - 1, C_KERNEL - 1

    @pl.when(i == 0)
    def _():
        xb_ref[:, 0:CTX_B, :] = jnp.zeros((nbb, CTX_B, BRANCH_WIDTH), F32)
        xc_ref[:, 0:CTX_C, :] = jnp.zeros((nbb, CTX_C, BRANCH_WIDTH), F32)
        if has_hist:
            xb_ref[:, CTX_B - hb_rows:CTX_B, :] = hb_ref[...]
            xc_ref[:, CTX_C - hc_rows:CTX_C, :] = hc_ref[...]

    zb = zb_ref[...]
    zc = zc_ref[...]
    glu = zb[:, :BRANCH_WIDTH] * jax.nn.sigmoid(zb[:, BRANCH_WIDTH:])
    u_c = zc[:, BRANCH_WIDTH:2 * BRANCH_WIDTH] * zc[:, 2 * BRANCH_WIDTH:]
    accs_b, accs_c = [], []
    for s in range(nbb):
        xb_ref[s, CTX_B:CTX_B + tt, :] = glu[s * tt:(s + 1) * tt]
        xc_ref[s, CTX_C:CTX_C + tt, :] = u_c[s * tt:(s + 1) * tt]
        acc = jnp.zeros((tt, BRANCH_WIDTH), F32)
        for k in range(B_KERNEL):
            acc = acc + xb_ref[s, pl.ds(CTX_B - hb_rows + k, tt), :] * bw_ref[k:k + 1, :]
        accs_b.append(acc)
        acc = jnp.zeros((tt, BRANCH_WIDTH), F32)
        for k in range(C_KERNEL):
            acc = acc + xc_ref[s, pl.ds(CTX_C - hc_rows + k, tt), :] * cw_ref[k:k + 1, :]
        accs_c.append(acc)
    cat = lambda parts: parts[0] if nbb == 1 else jnp.concatenate(parts, axis=0)
    y = cat(accs_b) + bb_ref[...]
    mu = jnp.mean(y, axis=-1, keepdims=True)
    yc = y - mu
    var = jnp.mean(yc * yc, axis=-1, keepdims=True)
    o_b = jax.nn.silu(yc * lax.rsqrt(var + EPS) * lg_ref[...] + lb_ref[...])
    o_ref[...] = jnp.concatenate([o_b, zc[:, :BRANCH_WIDTH] * cat(accs_c)], axis=1).astype(o_ref.dtype)

    new_b = xb_ref[:, tt:tt + CTX_B, :]
    new_c = xc_ref[:, tt:tt + CTX_C, :]
    xb_ref[:, 0:CTX_B, :] = new_b
    xc_ref[:, 0:CTX_C, :] = new_c

    @pl.when(i == pl.num_programs(1) - 1)
    def _():
        nhb_ref[...] = new_b[:, CTX_B - hb_rows:, :]
        nhc_ref[...] = new_c[:, CTX_C - hc_rows:, :]


def conv_branches(z_all, row0, nb, t, tt, nbb, hist_b, hist_c, b_cw, b_cb, ln_g, ln_b, c_cw):
    assert nbb == 1 or tt == t
    nt = t // tt
    rows = nbb * tt
    r0 = row0 // rows
    has_hist = hist_b is not None
    w = BRANCH_WIDTH
    in_specs = [
        pl.BlockSpec((rows, 2 * w), lambda b, i: (r0 + b * nt + i, ZB // (2 * w))),
        pl.BlockSpec((rows, 3 * w), lambda b, i: (r0 + b * nt + i, ZC // (3 * w))),
    ]
    args = [z_all, z_all]
    if has_hist:
        in_specs += [pl.BlockSpec((nbb, B_KERNEL - 1, w), lambda b, i: (b, 0, 0)),
                     pl.BlockSpec((nbb, C_KERNEL - 1, w), lambda b, i: (b, 0, 0))]
        args += [hist_b, hist_c]
    params = [b_cw, b_cb.reshape(1, w), ln_g.reshape(1, w), ln_b.reshape(1, w), c_cw]
    in_specs += [pl.BlockSpec(p.shape, lambda b, i: (0, 0)) for p in params]
    return pl.pallas_call(
        functools.partial(_conv_kernel, tt=tt, nbb=nbb, has_hist=has_hist),
        grid=(nb // nbb, nt),
        in_specs=in_specs,
        out_specs=[pl.BlockSpec((rows, 2 * w), lambda b, i: (b * nt + i, 0)),
                   pl.BlockSpec((nbb, B_KERNEL - 1, w), lambda b, i: (b, 0, 0)),
                   pl.BlockSpec((nbb, C_KERNEL - 1, w), lambda b, i: (b, 0, 0))],
        out_shape=[jax.ShapeDtypeStruct((nb * t, 2 * w), BF16),
                   jax.ShapeDtypeStruct((nb, B_KERNEL - 1, w), F32),
                   jax.ShapeDtypeStruct((nb, C_KERNEL - 1, w), F32)],
        scratch_shapes=[pltpu.VMEM((nbb, CTX_B + tt, w), F32), pltpu.VMEM((nbb, CTX_C + tt, w), F32)],
        compiler_params=_cparams(("parallel", "arbitrary")),
        name="conv_branches",
    )(*args, *params)


HW = D_HEADS * D_KEY
SUB = 16


def _split_dot(a, b_bf16):
    hi = a.astype(BF16)
    lo = (a - hi.astype(F32)).astype(BF16)
    return _dot(hi, b_bf16) + _dot(lo, b_bf16)


def _hgrn_kernel(*refs, tb, seqs, has_state):
    if has_state:
        zd_ref, s0_ref = refs[:2]
        refs = refs[2:]
    else:
        zd_ref = refs[0]
        refs = refs[1:]
    lb_ref, gn_ref, o_ref, sfin_ref, st_ref = refs
    i = pl.program_id(1)
    c = min(SUB, tb // seqs)
    n_sub = tb // c
    assert seqs == 1 or (seqs == n_sub and has_state)

    blocks = [slice(h * D_KEY, (h + 1) * D_KEY) for h in range(D_HEADS)]

    def load_state(j):
        for h, blk in enumerate(blocks):
            st_ref[blk, blk] = s0_ref[j, h]
        return st_ref[...]

    def store_state(j, st):
        for h, blk in enumerate(blocks):
            sfin_ref[j, h] = st[blk, blk]

    if seqs == 1:
        @pl.when(i == 0)
        def _():
            st_ref[...] = jnp.zeros(st_ref.shape, F32)
            if has_state:
                load_state(0)
    else:
        st_ref[...] = jnp.zeros(st_ref.shape, F32)

    zd = zd_ref[...]
    lb = lb_ref[...]
    q = jax.nn.silu(zd[:, 0:HW])
    f = lb + (1.0 - lb) * jax.nn.sigmoid(zd[:, HW:2 * HW])
    logf = jnp.log(f)
    kk = 1.0 - f
    v = zd[:, 2 * HW:3 * HW]
    gate = jax.nn.silu(zd[:, 3 * HW:4 * HW])

    row = lax.broadcasted_iota(I32, (tb, tb), 0)
    col = lax.broadcasted_iota(I32, (tb, tb), 1)
    tril_sub = ((row // c == col // c) & (col <= row)).astype(F32)
    g = _dot(tril_sub, logf, precision=HI)

    hrow = lax.broadcasted_iota(I32, (HW, HW), 0) // D_KEY
    hcol = lax.broadcasted_iota(I32, (HW, HW), 1) // D_KEY
    same_head = hrow == hcol
    head_ones = same_head.astype(BF16)

    def pad_tokens(a):
        return a if tb >= LANES else jnp.concatenate([a, jnp.zeros((LANES - tb, HW), a.dtype)], axis=0)

    v_t = pad_tokens(v).T.astype(BF16)
    tok = lax.broadcasted_iota(I32, (tb, HW), 0)
    s_idx = lax.broadcasted_iota(I32, (c, HW), 0)

    pieces = []
    for j in range(n_sub):
        r0 = j * c
        gj, qj, kj = g[r0:r0 + c], q[r0:r0 + c], kk[r0:r0 + c]
        for t in range(c):
            mask = s_idx <= t
            e = jnp.exp(jnp.where(mask, gj[t:t + 1] - gj, 0.0))
            pieces.append(jnp.where(mask, qj[t:t + 1] * kj * e, 0.0))
    att = _split_dot(jnp.concatenate(pieces, axis=0), head_ones)
    o_intra = jnp.sum(att.reshape(tb, c, HW) * v.reshape(n_sub, 1, c, HW).repeat(c, axis=1).reshape(tb, c, HW), axis=1)
    q_dec = (q * jnp.exp(g)).astype(BF16)

    upds, decays = [], []
    for j in range(n_sub):
        r0 = j * c
        g_last = g[r0 + c - 1:r0 + c]
        in_sub = (tok >= r0) & (tok < r0 + c)
        kd = jnp.where(in_sub, kk * jnp.exp(jnp.where(in_sub, g_last - g, 0.0)), 0.0)
        upds.append(jnp.where(same_head, _dot(v_t, pad_tokens(kd).astype(BF16)), 0.0))
        decays.append(jnp.exp(g_last))

    o_inter = []
    if seqs == 1:
        st = st_ref[...]
        for j in range(n_sub):
            o_inter.append(_dot(q_dec[j * c:(j + 1) * c], st.astype(BF16), NT_DIMS))
            st = st * decays[j] + upds[j]
        st_ref[...] = st

        @pl.when(i == pl.num_programs(1) - 1)
        def _():
            store_state(0, st)
    else:
        for j in range(n_sub):
            st = load_state(j)
            o_inter.append(_dot(q_dec[j * c:(j + 1) * c], st.astype(BF16), NT_DIMS))
            store_state(j, st * decays[j] + upds[j])

    o = (jnp.concatenate(o_inter, axis=0) if n_sub > 1 else o_inter[0]) + o_intra
    ms = _split_dot(o * o, head_ones) * (1.0 / D_VAL)
    o_ref[...] = (o * lax.rsqrt(ms + EPS) * gn_ref[...] * gate).astype(o_ref.dtype)


def hgrn(z_all, row0, nb, t, tb, seqs, s0_t, lb, gn):
    assert (seqs == 1 and t % tb == 0) or tb == seqs * t
    nt = max(t // tb, 1)
    r0 = row0 // tb
    has_state = s0_t is not None
    in_specs = [pl.BlockSpec((tb, 4 * HW), lambda b, i: (r0 + b * nt + i, ZD // (4 * HW)))]
    args = [z_all]
    if has_state:
        in_specs.append(pl.BlockSpec((seqs, D_HEADS, D_VAL, D_KEY), lambda b, i: (b, 0, 0, 0)))
        args.append(s0_t)
    in_specs += [pl.BlockSpec((1, HW), lambda b, i: (0, 0)), pl.BlockSpec((1, HW), lambda b, i: (0, 0))]
    args += [lb.reshape(1, HW), jnp.tile(gn, D_HEADS).reshape(1, HW)]
    return pl.pallas_call(
        functools.partial(_hgrn_kernel, tb=tb, seqs=seqs, has_state=has_state),
        grid=(nb // seqs, nt),
        in_specs=in_specs,
        out_specs=[pl.BlockSpec((tb, HW), lambda b, i: (b * nt + i, 0)),
                   pl.BlockSpec((seqs, D_HEADS, D_VAL, D_KEY), lambda b, i: (b, 0, 0, 0))],
        out_shape=[jax.ShapeDtypeStruct((nb * t, HW), BF16), jax.ShapeDtypeStruct((nb, D_HEADS, D_VAL, D_KEY), F32)],
        scratch_shapes=[pltpu.VMEM((HW, HW), F32)],
        compiler_params=_cparams(("parallel", "arbitrary")),
        name="hgrn",
    )(*args)


def _merge_kernel(oa_ref, obc_ref, od_ref, ga_ref, gb_ref, gc_ref, gd_ref, wa_ref, wb_ref, wc_ref, wd_ref, wo_ref, x_ref, o_ref):
    obc = obc_ref[...]
    sig = lambda ref: jax.nn.sigmoid(ref[...].astype(F32))
    merged = sig(ga_ref) * _dot(oa_ref[...], wa_ref[...])
    merged += sig(gb_ref) * _dot(obc[:, :BRANCH_WIDTH], wb_ref[...])
    merged += sig(gc_ref) * _dot(obc[:, BRANCH_WIDTH:], wc_ref[...])
    merged += sig(gd_ref) * _dot(od_ref[...], wd_ref[...])
    o_ref[...] = x_ref[...] + _dot(merged.astype(BF16), wo_ref[...])


def merge(o_a, o_bc, o_d, gates, wa_p, wb, wc, wd, w_out, x, tm=512):
    n = x.shape[0]
    row = lambda w: pl.BlockSpec((tm, w), lambda i: (i, 0))
    full = lambda a: pl.BlockSpec(a.shape, lambda i: (0, 0))
    gate = lambda k: pl.BlockSpec((tm, D_MODEL), lambda i: (i, k))
    return pl.pallas_call(
        _merge_kernel,
        grid=(n // tm,),
        in_specs=[row(o_a.shape[1]), row(o_bc.shape[1]), row(o_d.shape[1]), gate(0), gate(1), gate(2), gate(3),
                  full(wa_p), full(wb), full(wc), full(wd), full(w_out), row(D_MODEL)],
        out_specs=row(D_MODEL),
        out_shape=jax.ShapeDtypeStruct((n, D_MODEL), F32),
        compiler_params=_cparams(("parallel",)),
        name="merge",
    )(o_a, o_bc, o_d, gates, gates, gates, gates, wa_p, wb, wc, wd, w_out, x)


HP = P_HEADS * 2
SLOTS = P_HEADS * P_TOPK
SUBK = P_DKEY // 2
STAGE1_UNROLL = 8


def _take_max(s, index, n):
    m = jnp.max(s, axis=0, keepdims=True)
    idx = jnp.min(jnp.where(s == m, index, n), axis=0, keepdims=True)
    return m, idx, jnp.where(index == idx, -jnp.inf, s)


_CAND_BLOCKS = [(0, P_TOPK)] + [(a, 8) for a in range(1, 8)]
_CAND_ROWS = sum(nb for _, nb in _CAND_BLOCKS) + 8


def _route_kernel(x_ref, g_ref, wq_ref, keys_ref, h_ref, ii_ref, jj_ref, gw_ref, q_s, sv_s, si_s, oi_s, oj_s, og_s, *, tm):
    x = x_ref[...]
    h = (x * lax.rsqrt(jnp.mean(x * x, axis=-1, keepdims=True) + EPS) * g_ref[...]).astype(BF16)
    h_ref[...] = h
    q_s[...] = _dot(h, wq_ref[...]).astype(BF16)

    half = P_NKEYS // 2
    n_lt = tm // LANES
    iota_lo = lax.broadcasted_iota(I32, (half, LANES), 0).astype(F32)
    iota_hi = iota_lo + float(half)

    def stage1(it, carry):
        hp = it // n_lt
        toks = pl.ds(pl.multiple_of((it % n_lt) * LANES, LANES), LANES)
        q = q_s[toks, pl.ds(pl.multiple_of(hp * SUBK, SUBK), SUBK)]
        s = _dot(keys_ref[hp], q, NT_DIMS)
        a, b = s[:half], s[half:]
        first = a >= b
        cur, cur_i = jnp.where(first, a, b), jnp.where(first, iota_lo, iota_hi)
        nxt, nxt_i = jnp.where(first, b, a), jnp.where(first, iota_hi, iota_lo)
        vals, idxs = [], []
        for _ in range(P_TOPK):
            m = jnp.max(cur, axis=0, keepdims=True)
            idx = jnp.min(jnp.where(cur == m, cur_i, float(P_NKEYS)), axis=0, keepdims=True)
            hit = cur_i == idx
            cur, cur_i, nxt = jnp.where(hit, nxt, cur), jnp.where(hit, nxt_i, cur_i), jnp.where(hit, -jnp.inf, nxt)
            vals.append(m)
            idxs.append(idx)
        sv_s[hp, :, toks] = jnp.concatenate(vals, axis=0)
        si_s[hp, :, toks] = jnp.concatenate(idxs, axis=0)
        return carry

    lax.fori_loop(0, HP * n_lt, stage1, 0, unroll=STAGE1_UNROLL)

    r = lax.broadcasted_iota(I32, (_CAND_ROWS, tm), 0)
    mid = r - P_TOPK
    flat = jnp.where(r < P_TOPK, r, jnp.where(r < _CAND_ROWS - 8, (1 + mid // 8) * P_TOPK + mid % 8, (r - (_CAND_ROWS - 16)) * P_TOPK))
    flat = flat.astype(F32)
    iota_t = lax.broadcasted_iota(I32, (P_TOPK, tm), 0).astype(F32)

    def stage2(hd, carry):
        sv1, sv2 = sv_s[2 * hd], sv_s[2 * hd + 1]
        si1, si2 = si_s[2 * hd], si_s[2 * hd + 1]
        cand = jnp.concatenate([sv1[a:a + 1] + sv2[0:nb] for a, nb in _CAND_BLOCKS] + [sv1[8:P_TOPK] + sv2[0:1]], axis=0)
        fv, ei, ej = [], [], []
        for _ in range(P_TOPK):
            m, idx, cand = _take_max(cand, flat, float(P_TOPK * P_TOPK))
            a = jnp.floor(idx * (1.0 / P_TOPK))
            b = idx - a * P_TOPK
            fv.append(m)
            ei.append(jnp.sum(jnp.where(iota_t == a, si1, 0.0), axis=0, keepdims=True))
            ej.append(jnp.sum(jnp.where(iota_t == b, si2, 0.0), axis=0, keepdims=True))
        fv = jnp.concatenate(fv, axis=0)
        e = jnp.exp(fv - fv[0:1])
        rows = pl.ds(pl.multiple_of(hd * P_TOPK, P_TOPK), P_TOPK)
        og_s[rows, :] = e / jnp.sum(e, axis=0, keepdims=True)
        oi_s[rows, :] = jnp.concatenate(ei, axis=0)
        oj_s[rows, :] = jnp.concatenate(ej, axis=0)
        return carry

    lax.fori_loop(0, P_HEADS, stage2, 0, unroll=2)
    ii_ref[...] = oi_s[...].T.astype(I32)
    jj_ref[...] = oj_s[...].T.astype(I32)
    gw_ref[...] = og_s[...].T


def peer_route(x, g, wq, keys, tm=256):
    n = x.shape[0]
    row = lambda w: pl.BlockSpec((tm, w), lambda i: (i, 0))
    return pl.pallas_call(
        functools.partial(_route_kernel, tm=tm),
        grid=(n // tm,),
        in_specs=[row(D_MODEL), pl.BlockSpec((1, D_MODEL), lambda i: (0, 0)),
                  pl.BlockSpec(wq.shape, lambda i: (0, 0)), pl.BlockSpec(keys.shape, lambda i: (0, 0, 0))],
        out_specs=[row(D_MODEL), row(SLOTS), row(SLOTS), row(SLOTS)],
        out_shape=[jax.ShapeDtypeStruct((n, D_MODEL), BF16), jax.ShapeDtypeStruct((n, SLOTS), I32),
                   jax.ShapeDtypeStruct((n, SLOTS), I32), jax.ShapeDtypeStruct((n, SLOTS), F32)],
        scratch_shapes=[pltpu.VMEM((tm, HP * SUBK), BF16), pltpu.VMEM((HP, P_TOPK, tm), F32), pltpu.VMEM((HP, P_TOPK, tm), F32),
                        pltpu.VMEM((SLOTS, tm), F32), pltpu.VMEM((SLOTS, tm), F32), pltpu.VMEM((SLOTS, tm), F32)],
        compiler_params=_cparams(("parallel",)),
        name="peer_route",
    )(x, g.reshape(1, D_MODEL), wq, keys)


CHUNK_I = 16
CHUNK_E = CHUNK_I * P_NKEYS
N_CHUNKS = P_NKEYS // CHUNK_I


def _peer_kernel(h_ref, ii_ref, jj_ref, gw_ref, u_ref, v_ref, x_ref, o_ref, w_s, acc_s, *, tm):
    c = pl.program_id(1)

    @pl.when(c == 0)
    def _():
        acc_s[...] = jnp.zeros(acc_s.shape, F32)
        iota = lax.broadcasted_iota(I32, (P_NKEYS, SLOTS), 0)

        def build(g, carry):
            t0 = pl.multiple_of(g * PACK, PACK)
            ws = []
            for t in range(PACK):
                irow = ii_ref[pl.ds(t0 + t, 1), :]
                jrow = jj_ref[pl.ds(t0 + t, 1), :]
                grow = gw_ref[pl.ds(t0 + t, 1), :]
                p_t = jnp.where(iota == irow, grow, 0.0).astype(BF16)
                q_t = jnp.where(iota == jrow, 1.0, 0.0).astype(BF16)
                ws.append(_dot(p_t, q_t, NT_DIMS).astype(BF16))
            w_s[:, pl.ds(t0, PACK), :] = pltpu.einshape("tij->itj", jnp.stack(ws, axis=0))
            return carry

        lax.fori_loop(0, tm // PACK, build, 0, unroll=2)

    a = _dot(h_ref[...], u_ref[...], NT_DIMS)
    act = 0.5 * a * (1.0 + lax.erf(a * (1.0 / math.sqrt(2.0))))
    wd = jnp.concatenate([w_s[c * CHUNK_I + r] for r in range(CHUNK_I)], axis=1)
    acc_s[...] += _dot(act.astype(BF16) * wd, v_ref[...])

    @pl.when(c == N_CHUNKS - 1)
    def _():
        o_ref[...] = x_ref[...] + acc_s[...]


def peer_experts(layer, h2, ii, jj, gw, u_tabs, v_tabs, x, tm=512):
    n = x.shape[0]
    row = lambda w: pl.BlockSpec((tm, w), lambda i, c: (i, 0))
    return pl.pallas_call(
        functools.partial(_peer_kernel, tm=tm),
        grid=(n // tm, N_CHUNKS),
        in_specs=[row(D_MODEL), row(SLOTS), row(SLOTS), row(SLOTS),
                  pl.BlockSpec((None, CHUNK_E, D_MODEL), lambda i, c: (layer, c, 0)),
                  pl.BlockSpec((None, CHUNK_E, D_MODEL), lambda i, c: (layer, c, 0)),
                  row(D_MODEL)],
        out_specs=row(D_MODEL),
        out_shape=jax.ShapeDtypeStruct((n, D_MODEL), F32),
        scratch_shapes=[pltpu.VMEM((P_NKEYS, tm, P_NKEYS), BF16), pltpu.VMEM((tm, D_MODEL), F32)],
        compiler_params=_cparams(("parallel", "arbitrary")),
        name="peer_experts",
    )(h2, ii, jj, gw, u_tabs, v_tabs, x)


def _rmsnorm_kernel(x_ref, g_ref, o_ref):
    x = x_ref[...]
    o_ref[...] = x * lax.rsqrt(jnp.mean(x * x, axis=-1, keepdims=True) + EPS) * g_ref[...]


def rmsnorm(x, g, row0, n, tm=512):
    d = x.shape[1]
    r0 = row0 // tm
    return pl.pallas_call(
        _rmsnorm_kernel,
        grid=(n // tm,),
        in_specs=[pl.BlockSpec((tm, d), lambda i: (r0 + i, 0)), pl.BlockSpec((1, d), lambda i: (0, 0))],
        out_specs=pl.BlockSpec((tm, d), lambda i: (i, 0)),
        out_shape=jax.ShapeDtypeStruct((n, d), F32),
        compiler_params=_cparams(("parallel",)),
        name="rmsnorm",
    )(x, g.reshape(1, d))


def _rope_tables():
    half = A_ROPE // 2
    inv = ROPE_THETA ** (-jnp.arange(half, dtype=F32) / half)
    pos = jnp.concatenate([jnp.arange(SEQ), PAST_LEN + jnp.arange(DEC_SEQ)])
    posf = pos.astype(F32)[:, None]

    def tables(start):
        zeros = jnp.zeros((LANES - start - A_ROPE,), F32)
        freq = jnp.concatenate([jnp.zeros((start,), F32), inv, inv, zeros])
        keep = jnp.concatenate([jnp.ones((start + A_ROPE,), F32), zeros])
        sign = jnp.concatenate([jnp.zeros((start,), F32), -jnp.ones((half,), F32), jnp.ones((half,), F32), zeros])
        ang = posf * freq[None, :]
        per_token = lambda t: jnp.concatenate([jnp.tile(t[:SEQ], (BATCH, 1)), jnp.tile(t[SEQ:], (DEC_BATCH, 1))], axis=0)
        return per_token(jnp.cos(ang) * keep[None, :]), per_token(jnp.sin(ang) * sign[None, :])

    ccq, ssq = tables(A_NOPE)
    cck, ssk = tables(0)
    return ccq, ssq, cck, ssk


def _swap_halves(w):
    half = w.shape[-1] // 2
    return jnp.concatenate([w[..., half:], w[..., :half]], axis=-1)


def _layer_weights(l, w_in, a_w_uq, a_w_uk, a_w_uv, w_branch, w_out, p_w_q, p_sub_keys):
    w = w_in[l]
    o = 0
    parts = []
    for size in (A_Q_LORA, A_KV_LORA, A_ROPE, 2 * BRANCH_WIDTH, 3 * BRANCH_WIDTH, 4 * HW, N_BRANCH * D_MODEL):
        parts.append(w[:, o:o + size])
        o += size
    cq, ckv, kr, b_in, c_in, d_in, gate = parts
    pad = jnp.zeros((D_MODEL, ZB - A_Q_LORA - A_KV_LORA - 2 * A_ROPE), F32)
    pad_c = jnp.zeros((D_MODEL, ZC - ZD - 4 * HW), F32)
    w_in_p = jnp.concatenate([cq, ckv, kr, _swap_halves(kr), pad, b_in, d_in, pad_c, c_in], axis=1).astype(BF16)

    wq = a_w_uq[l].reshape(A_Q_LORA, A_HEADS, A_NOPE + A_ROPE)
    nope, rope = wq[..., :A_NOPE], wq[..., A_NOPE:]
    tail = jnp.zeros((A_Q_LORA, A_HEADS, HEAD_PAD - A_NOPE - A_ROPE), F32)
    wq_main = jnp.concatenate([nope, rope, tail], axis=-1).reshape(A_Q_LORA, -1).astype(BF16)
    wq_swap = jnp.concatenate([jnp.zeros_like(nope), _swap_halves(rope), tail], axis=-1).reshape(A_Q_LORA, -1).astype(BF16)

    w_uk, w_uv = a_w_uk[l], a_w_uv[l]
    head_tail = jnp.zeros((A_KV_LORA, A_HEADS, HEAD_PAD - A_NOPE), F32)
    wuk_p = jnp.concatenate([w_uk, head_tail], axis=-1).reshape(A_KV_LORA, -1).astype(BF16)
    wuv_p = jnp.concatenate([w_uv, head_tail], axis=-1).reshape(A_KV_LORA, -1).astype(BF16)

    r = jnp.arange(LANES)[:, None]
    col = jnp.arange(A_HEADS * HEAD_PAD)[None, :]
    place = ((col % HEAD_PAD == A_NOPE + r) & (r < A_ROPE)).astype(BF16)

    blk = jnp.zeros((A_HEADS, HEAD_PAD, 2 * LANES), F32)
    blk = blk.at[:, :A_NOPE, :A_KV_LORA].set(jnp.transpose(w_uk, (1, 2, 0)))
    blk = blk.at[:, A_NOPE:A_NOPE + A_ROPE, A_KV_LORA:A_KV_LORA + A_ROPE].set(jnp.eye(A_ROPE, dtype=F32))
    eye_h = jnp.eye(A_HEADS, dtype=F32)
    wabs = (blk[:, :, None, :] * eye_h[:, None, :, None]).reshape(A_HEADS * HEAD_PAD, A_HEADS * 2 * LANES).astype(BF16)

    wb = w_branch[l]
    wa_p = jnp.concatenate([wb[0].reshape(A_HEADS, A_V, D_MODEL), jnp.zeros((A_HEADS, HEAD_PAD - A_V, D_MODEL), F32)],
                           axis=1).reshape(A_HEADS * HEAD_PAD, D_MODEL).astype(BF16)
    return dict(w_in_p=w_in_p, w_gate=gate.astype(BF16), wq_main=wq_main, wq_swap=wq_swap, wuk_p=wuk_p, wuv_p=wuv_p, place=place, wabs=wabs,
                wa_p=wa_p, wb=wb[1].astype(BF16), wc=wb[2].astype(BF16), wd=wb[3].astype(BF16), w_out=w_out[l].astype(BF16),
                wq=p_w_q[l].astype(BF16), keys=p_sub_keys[l].reshape(HP, P_NKEYS, SUBK).astype(BF16))


def _swap_state(s):
    return jnp.transpose(s, (0, 1, 3, 2))


def kernel(x_prompt, x_sample, cache_ckv, cache_krope, page_table, state_conv_b, state_conv_c, state_hgrn, norm1_g, w_in, a_q_norm_g, a_w_uq, a_kv_norm_g, a_w_uk, a_w_uv, b_conv_w, b_conv_b, b_ln_g, b_ln_b, c_conv_w, d_lower_bound, d_gnorm_g, w_branch, w_out, norm2_g, p_w_q, p_sub_keys, p_u, p_v, final_norm_g):
    lb_soft = jax.nn.softmax(d_lower_bound.astype(F32), axis=0)
    lower_bounds = jnp.cumsum(lb_soft, axis=0) - lb_soft[0:1]
    ccq, ssq, cck, ssk = _rope_tables()
    x = jnp.concatenate([x_prompt.reshape(NP, D_MODEL), x_sample.reshape(NS, D_MODEL)], axis=0)
    cache_krope_t = jnp.swapaxes(cache_krope, 2, 3)
    u_tabs, v_tabs = p_u.astype(BF16), p_v.astype(BF16)

    states_p, states_s = [], []
    for l in range(DEPTH):
        w = _layer_weights(l, w_in, a_w_uq, a_w_uk, a_w_uv, w_branch, w_out, p_w_q, p_sub_keys)
        z = norm_matmul(x, norm1_g[l], w["w_in_p"], 1024, Z_COLS // 2, F32)
        gates = norm_matmul(x, norm1_g[l], w["w_gate"], 1024, G_COLS // 2, BF16)

        ckv, kr, qp, kp, vp = mla_prep(z, a_q_norm_g[l], a_kv_norm_g[l], w["wq_main"], w["wq_swap"], w["wuk_p"], w["wuv_p"],
                                       w["place"], ccq, ssq, cck, ssk)
        oa_p = flash_prompt(qp, kp, vp)
        qlat = matmul(qp[NP:], w["wabs"], 512, BF16)
        qlat = jnp.transpose(qlat.reshape(DEC_BATCH, DEC_SEQ, A_HEADS, 2 * LANES), (0, 2, 1, 3)).reshape(DEC_BATCH, DEC_ROWS, 2 * LANES)
        oa_s = mla_decode(l, page_table, qlat, cache_ckv, cache_krope_t, ckv[NP:].reshape(DEC_BATCH, DEC_SEQ, LANES),
                          kr[NP:].reshape(DEC_BATCH, DEC_SEQ, LANES), w["wuv_p"])
        o_a = jnp.concatenate([oa_p, oa_s.reshape(NS, -1)], axis=0)

        conv_w = (b_conv_w[l], b_conv_b[l], b_ln_g[l], b_ln_b[l], c_conv_w[l])
        obc_p, hb_p, hc_p = conv_branches(z, 0, BATCH, SEQ, 512, 1, None, None, *conv_w)
        obc_s, hb_s, hc_s = conv_branches(z, NP, DEC_BATCH, DEC_SEQ, DEC_SEQ, SEQS_PER_STEP, state_conv_b[l], state_conv_c[l], *conv_w)
        o_bc = jnp.concatenate([obc_p, obc_s], axis=0)

        od_p, st_p = hgrn(z, 0, BATCH, SEQ, LANES, 1, None, lower_bounds[l], d_gnorm_g[l])
        od_s, st_s = hgrn(z, NP, DEC_BATCH, DEC_SEQ, SEQS_PER_STEP * DEC_SEQ, SEQS_PER_STEP, _swap_state(state_hgrn[l]), lower_bounds[l], d_gnorm_g[l])
        o_d = jnp.concatenate([od_p, od_s], axis=0)

        x1 = merge(o_a, o_bc, o_d, gates, w["wa_p"], w["wb"], w["wc"], w["wd"], w["w_out"], x)
        h2, ii, jj, gw = peer_route(x1, norm2_g[l], w["wq"], w["keys"])
        x = peer_experts(l, h2, ii, jj, gw, u_tabs, v_tabs, x1)

        states_p.append((ckv[:NP].reshape(BATCH, SEQ, A_KV_LORA), kr[:NP, :A_ROPE].reshape(BATCH, SEQ, A_ROPE),
                         hb_p, hc_p, _swap_state(st_p)))
        states_s.append((ckv[NP:].reshape(DEC_BATCH, DEC_SEQ, A_KV_LORA), kr[NP:, :A_ROPE].reshape(DEC_BATCH, DEC_SEQ, A_ROPE),
                         hb_s, hc_s, _swap_state(st_s)))

    y_p = rmsnorm(x, final_norm_g, 0, NP)
    y_s = rmsnorm(x, final_norm_g, NP, NS)
    stack = lambda states: [jnp.stack([s[i] for s in states], axis=0) for i in range(5)]
    return (y_p.reshape(BATCH, SEQ, D_MODEL), y_s.reshape(DEC_BATCH, DEC_SEQ, D_MODEL), *stack(states_p), *stack(states_s))
```

```python
import functools
import math

import jax
import jax.numpy as jnp
from jax import lax
from jax.experimental import pallas as pl
from jax.experimental.pallas import tpu as pltpu

F32 = jnp.float32
BF16 = jnp.bfloat16
I32 = jnp.int32

D_MODEL = 1024
BATCH = 2
SEQ = 8192
DEPTH = 2
DEC_BATCH = 128
DEC_SEQ = 8
PAST_LEN = 8192
PAGE_SIZE = 128
N_PAGES = PAST_LEN // PAGE_SIZE
BRANCH_WIDTH = 256
N_BRANCH = 4
A_HEADS = 4
A_NOPE = 64
A_ROPE = 32
A_V = 64
A_Q_LORA = 256
A_KV_LORA = 128
ROPE_THETA = 10000.0
B_KERNEL = 31
C_KERNEL = 3
D_HEADS = 4
D_KEY = 64
D_VAL = 64
P_HEADS = 8
P_NKEYS = 128
P_DKEY = 256
P_TOPK = 16
P_EXPERTS = P_NKEYS * P_NKEYS
EPS = 1e-6
NEG_BIG = -1e30

NP = BATCH * SEQ
NS = DEC_BATCH * DEC_SEQ
NT = NP + NS

LANES = 128
SUBLANES = 8
PACK = 16
HEAD_PAD = 128
ZA, ZB, ZD, ZC = 0, 512, 1024, 2304
Z_COLS = 3072
G_COLS = N_BRANCH * D_MODEL
VMEM_LIMIT = 56 * 1024 * 1024

HI = lax.Precision.HIGHEST


def _cparams(sem):
    return pltpu.CompilerParams(dimension_semantics=sem, vmem_limit_bytes=VMEM_LIMIT)


def _dot(a, b, dims=(((1,), (0,)), ((), ())), precision=None):
    return lax.dot_general(a, b, dims, precision=precision, preferred_element_type=F32)


NT_DIMS = (((1,), (1,)), ((), ()))
TN_DIMS = (((0,), (0,)), ((), ()))


def _norm_matmul_kernel(x_ref, g_ref, w_ref, o_ref, h_ref):
    @pl.when(pl.program_id(1) == 0)
    def _():
        x = x_ref[...]
        y = x * lax.rsqrt(jnp.mean(x * x, axis=-1, keepdims=True) + EPS)
        h_ref[...] = (y * g_ref[...]).astype(BF16)

    o_ref[...] = _dot(h_ref[...], w_ref[...]).astype(o_ref.dtype)


def norm_matmul(x, g, w, tm, tn, out_dtype):
    n, d = x.shape
    cols = w.shape[1]
    return pl.pallas_call(
        _norm_matmul_kernel,
        grid=(n // tm, cols // tn),
        in_specs=[
            pl.BlockSpec((tm, d), lambda i, j: (i, 0)),
            pl.BlockSpec((1, d), lambda i, j: (0, 0)),
            pl.BlockSpec((d, tn), lambda i, j: (0, j)),
        ],
        out_specs=pl.BlockSpec((tm, tn), lambda i, j: (i, j)),
        out_shape=jax.ShapeDtypeStruct((n, cols), out_dtype),
        scratch_shapes=[pltpu.VMEM((tm, d), BF16)],
        compiler_params=_cparams(("parallel", "arbitrary")),
        name="norm_matmul",
    )(x, g.reshape(1, d), w)


def _matmul_kernel(x_ref, w_ref, o_ref):
    o_ref[...] = _dot(x_ref[...], w_ref[...]).astype(o_ref.dtype)


def matmul(x, w, tm, out_dtype):
    n, d = x.shape
    cols = w.shape[1]
    return pl.pallas_call(
        _matmul_kernel,
        grid=(n // tm,),
        in_specs=[pl.BlockSpec((tm, d), lambda i: (i, 0)), pl.BlockSpec((d, cols), lambda i: (0, 0))],
        out_specs=pl.BlockSpec((tm, cols), lambda i: (i, 0)),
        out_shape=jax.ShapeDtypeStruct((n, cols), out_dtype),
        compiler_params=_cparams(("parallel",)),
        name="matmul",
    )(x, w)


def _tile4(t):
    return jnp.concatenate([t, t, t, t], axis=1)


def _mla_prep_kernel(z_ref, qn_ref, kvn_ref, wqm_ref, wqs_ref, wuk_ref, wuv_ref, place_ref,
                     ccq_ref, ssq_ref, cck_ref, ssk_ref,
                     ckv_ref, kr_ref, q_ref, k_ref, v_ref):
    z = z_ref[...]
    cq = z[:, 0:A_Q_LORA]
    cqn = cq * lax.rsqrt(jnp.mean(cq * cq, axis=-1, keepdims=True) + EPS) * qn_ref[...]
    cqn = cqn.astype(BF16)
    scale = 1.0 / math.sqrt(A_NOPE + A_ROPE)
    q = _dot(cqn, wqm_ref[...]) * _tile4(ccq_ref[...]) + _dot(cqn, wqs_ref[...]) * _tile4(ssq_ref[...])
    q_ref[...] = (q * scale).astype(BF16)

    c = z[:, A_Q_LORA:A_Q_LORA + A_KV_LORA]
    ckv = c * lax.rsqrt(jnp.mean(c * c, axis=-1, keepdims=True) + EPS) * kvn_ref[...]
    ckv_ref[...] = ckv
    ckv_b = ckv.astype(BF16)

    kc = z[:, A_Q_LORA + A_KV_LORA:]
    kr = kc * cck_ref[...] + pltpu.roll(kc, LANES - A_ROPE, axis=1) * ssk_ref[...]
    kr_ref[...] = kr

    k = _dot(ckv_b, wuk_ref[...]) + _dot(kr.astype(BF16), place_ref[...])
    k_ref[...] = k.astype(BF16)
    lane = lax.broadcasted_iota(I32, (1, A_HEADS * HEAD_PAD), 1) % HEAD_PAD
    v_ref[...] = (_dot(ckv_b, wuv_ref[...]) + (lane == A_V).astype(F32)).astype(BF16)


def mla_prep(z_all, qn_g, kvn_g, wq_main, wq_swap, wuk_p, wuv_p, place, ccq, ssq, cck, ssk, tm=512):
    n = z_all.shape[0]
    hp = A_HEADS * HEAD_PAD
    row = lambda w: pl.BlockSpec((tm, w), lambda i: (i, 0))
    full = lambda a: pl.BlockSpec(a.shape, lambda i: (0,) * a.ndim)
    qn_g = qn_g.reshape(1, -1)
    kvn_g = kvn_g.reshape(1, -1)
    return pl.pallas_call(
        _mla_prep_kernel,
        grid=(n // tm,),
        in_specs=[row(512), full(qn_g), full(kvn_g), full(wq_main), full(wq_swap), full(wuk_p), full(wuv_p), full(place),
                  row(LANES), row(LANES), row(LANES), row(LANES)],
        out_specs=[row(LANES), row(LANES), row(hp), row(hp), row(hp)],
        out_shape=[jax.ShapeDtypeStruct((n, LANES), F32), jax.ShapeDtypeStruct((n, LANES), F32),
                   jax.ShapeDtypeStruct((n, hp), BF16), jax.ShapeDtypeStruct((n, hp), BF16),
                   jax.ShapeDtypeStruct((n, hp), BF16)],
        compiler_params=_cparams(("parallel",)),
        name="mla_prep",
    )(z_all, qn_g, kvn_g, wq_main, wq_swap, wuk_p, wuv_p, place, ccq, ssq, cck, ssk)


def _flash_kernel(q_ref, k_ref, v_ref, o_ref, *, tq, tk):
    i = pl.program_id(2)
    q = q_ref[...]

    def step(j, carry, masked):
        m, acc = carry
        kj = k_ref[pl.ds(pl.multiple_of(j * tk, tk), tk), :]
        vj = v_ref[pl.ds(pl.multiple_of(j * tk, tk), tk), :]
        s = _dot(q, kj, NT_DIMS)
        if masked:
            qpos = i * tq + lax.broadcasted_iota(I32, (tq, tk), 0)
            kpos = j * tk + lax.broadcasted_iota(I32, (tq, tk), 1)
            s = jnp.where(kpos <= qpos, s, NEG_BIG)
        m_new = jnp.maximum(m, jnp.max(s, axis=-1, keepdims=True))
        p = jnp.exp(s - m_new)
        acc = jnp.exp(m - m_new) * acc + _dot(p.astype(BF16), vj)
        return m_new, acc

    init = (jnp.full((tq, 1), NEG_BIG, F32), jnp.zeros((tq, HEAD_PAD), F32))
    n_full = (i * tq) // tk
    carry = lax.fori_loop(0, n_full // 2, lambda p, c: step(2 * p + 1, step(2 * p, c, False), False), init)
    carry = lax.fori_loop(0, n_full % 2, lambda _, c: step(n_full - 1, c, False), carry)
    for d in range(tq // tk):
        carry = step(n_full + d, carry, True)
    m, acc = carry
    o_ref[...] = (acc / acc[:, A_V:A_V + 1]).astype(o_ref.dtype)


def flash_prompt(qp, kp, vp, tq=1024, tk=1024):
    assert tq % tk == 0
    nq = SEQ // tq
    return pl.pallas_call(
        functools.partial(_flash_kernel, tq=tq, tk=tk),
        grid=(BATCH, A_HEADS, nq),
        in_specs=[
            pl.BlockSpec((tq, HEAD_PAD), lambda b, h, i: (b * nq + i, h)),
            pl.BlockSpec((SEQ, HEAD_PAD), lambda b, h, i: (b, h)),
            pl.BlockSpec((SEQ, HEAD_PAD), lambda b, h, i: (b, h)),
        ],
        out_specs=pl.BlockSpec((tq, HEAD_PAD), lambda b, h, i: (b * nq + i, h)),
        out_shape=jax.ShapeDtypeStruct((NP, A_HEADS * HEAD_PAD), BF16),
        compiler_params=_cparams(("parallel", "parallel", "arbitrary")),
        name="flash_prompt",
    )(qp, kp, vp)


DEC_ROWS = A_HEADS * DEC_SEQ


def _decode_kernel(pt_ref, q_ref, cnew_ref, knew_ref, wuv_ref, ckv_hbm, kr_hbm, o_ref, kbuf, krbuf, sem, *, page0):
    b = pl.program_id(0)
    slot = b % 2

    def page_copies(seq, sl, r):
        page = page0 + pt_ref[seq, r]
        return (pltpu.make_async_copy(ckv_hbm.at[page], kbuf.at[sl, r], sem.at[0, sl]),
                pltpu.make_async_copy(kr_hbm.at[page], krbuf.at[sl, r], sem.at[1, sl]))

    def start_fetch(seq, sl):
        def body(r, carry):
            for cp in page_copies(seq, sl, r):
                cp.start()
            return carry
        lax.fori_loop(0, N_PAGES, body, 0)

    @pl.when(b == 0)
    def _():
        start_fetch(0, 0)

    @pl.when(b + 1 < pl.num_programs(0))
    def _():
        start_fetch(b + 1, 1 - slot)

    def wait_body(r, carry):
        for cp in page_copies(b, slot, r):
            cp.wait()
        return carry

    lax.fori_loop(0, N_PAGES, wait_body, 0)

    q = q_ref[...]
    q_lat = q[:, :A_KV_LORA]
    q_rope = q[:, A_KV_LORA:A_KV_LORA + A_ROPE]
    kc = kbuf[slot].reshape(PAST_LEN, A_KV_LORA).astype(BF16)
    kr_t = jnp.concatenate([krbuf[slot, r] for r in range(N_PAGES)], axis=1).astype(BF16)
    s_past = _dot(q_lat, kc, NT_DIMS) + _dot(q_rope, kr_t)

    cn = cnew_ref[...].astype(BF16)
    kn = knew_ref[...][:, :A_ROPE].astype(BF16)
    s_new = _dot(q_lat, cn, NT_DIMS) + _dot(q_rope, kn, NT_DIMS)
    t_q = lax.broadcasted_iota(I32, s_new.shape, 0) % DEC_SEQ
    t_k = lax.broadcasted_iota(I32, s_new.shape, 1)
    s_new = jnp.where(t_k <= t_q, s_new, NEG_BIG)

    m = jnp.maximum(jnp.max(s_past, axis=-1, keepdims=True), jnp.max(s_new, axis=-1, keepdims=True))
    p_past = jnp.exp(s_past - m)
    p_new = jnp.exp(s_new - m)
    l = jnp.sum(p_past, axis=-1, keepdims=True) + jnp.sum(p_new, axis=-1, keepdims=True)
    acc = _dot(p_past.astype(BF16), kc) + _dot(p_new.astype(BF16), cn)
    o_lat = (acc / l).astype(BF16)
    for h in range(A_HEADS):
        o_ref[:, h * HEAD_PAD:(h + 1) * HEAD_PAD] = _dot(
            o_lat[h * DEC_SEQ:(h + 1) * DEC_SEQ, :], wuv_ref[:, h * HEAD_PAD:(h + 1) * HEAD_PAD]).astype(o_ref.dtype)


def mla_decode(layer, page_table, qlat, cache_ckv, cache_krope_t, ckv_new, kr_new, wuv_p):
    n_pool = cache_ckv.shape[1]
    hbm = pl.BlockSpec(memory_space=pl.ANY)
    grid_spec = pltpu.PrefetchScalarGridSpec(
        num_scalar_prefetch=1,
        grid=(DEC_BATCH,),
        in_specs=[
            pl.BlockSpec((None, DEC_ROWS, 2 * LANES), lambda b, pt: (b, 0, 0)),
            pl.BlockSpec((None, DEC_SEQ, LANES), lambda b, pt: (b, 0, 0)),
            pl.BlockSpec((None, DEC_SEQ, LANES), lambda b, pt: (b, 0, 0)),
            pl.BlockSpec(wuv_p.shape, lambda b, pt: (0, 0)),
            hbm, hbm,
        ],
        out_specs=pl.BlockSpec((None, DEC_SEQ, A_HEADS * HEAD_PAD), lambda b, pt: (b, 0, 0)),
        scratch_shapes=[pltpu.VMEM((2, N_PAGES, PAGE_SIZE, A_KV_LORA), F32), pltpu.VMEM((2, N_PAGES, A_ROPE, PAGE_SIZE), F32),
                        pltpu.SemaphoreType.DMA((2, 2))],
    )
    return pl.pallas_call(
        functools.partial(_decode_kernel, page0=layer * n_pool),
        grid_spec=grid_spec,
        out_shape=jax.ShapeDtypeStruct((DEC_BATCH, DEC_SEQ, A_HEADS * HEAD_PAD), BF16),
        compiler_params=_cparams(("arbitrary",)),
        name="mla_decode",
    )(page_table, qlat, ckv_new, kr_new, wuv_p, cache_ckv.reshape(-1, PAGE_SIZE, A_KV_LORA), cache_krope_t.reshape(-1, A_ROPE, PAGE_SIZE))


CTX_B = 32
CTX_C = 8
SEQS_PER_STEP = 8


def _conv_kernel(*refs, tt, nbb, has_hist):
    if has_hist:
        zb_ref, zc_ref, hb_ref, hc_ref = refs[:4]
        refs = refs[4:]
    else:
        zb_ref, zc_ref = refs[:2]
        refs = refs[2:]
    bw_ref, bb_ref, lg_ref, lb_ref, cw_ref, o_ref, nhb_ref, nhc_ref, xb_ref, xc_ref = refs
    i = pl.program_id(1)
    hb_rows, hc_rows = B_KERNEL - 1, C_KERNEL - 1

    @pl.when(i == 0)
    def _():
        xb_ref[:, 0:CTX_B, :] = jnp.zeros((nbb, CTX_B, BRANCH_WIDTH), F32)
        xc_ref[:, 0:CTX_C, :] = jnp.zeros((nbb, CTX_C, BRANCH_WIDTH), F32)
        if has_hist:
            xb_ref[:, CTX_B - hb_rows:CTX_B, :] = hb_ref[...]
            xc_ref[:, CTX_C - hc_rows:CTX_C, :] = hc_ref[...]

    zb = zb_ref[...]
    zc = zc_ref[...]
    glu = zb[:, :BRANCH_WIDTH] * jax.nn.sigmoid(zb[:, BRANCH_WIDTH:])
    u_c = zc[:, BRANCH_WIDTH:2 * BRANCH_WIDTH] * zc[:, 2 * BRANCH_WIDTH:]
    accs_b, accs_c = [], []
    for s in range(nbb):
        xb_ref[s, CTX_B:CTX_B + tt, :] = glu[s * tt:(s + 1) * tt]
        xc_ref[s, CTX_C:CTX_C + tt, :] = u_c[s * tt:(s + 1) * tt]
        acc = jnp.zeros((tt, BRANCH_WIDTH), F32)
        for k in range(B_KERNEL):
            acc = acc + xb_ref[s, pl.ds(CTX_B - hb_rows + k, tt), :] * bw_ref[k:k + 1, :]
        accs_b.append(acc)
        acc = jnp.zeros((tt, BRANCH_WIDTH), F32)
        for k in range(C_KERNEL):
            acc = acc + xc_ref[s, pl.ds(CTX_C - hc_rows + k, tt), :] * cw_ref[k:k + 1, :]
        accs_c.append(acc)
    cat = lambda parts: parts[0] if nbb == 1 else jnp.concatenate(parts, axis=0)
    y = cat(accs_b) + bb_ref[...]
    mu = jnp.mean(y, axis=-1, keepdims=True)
    yc = y - mu
    var = jnp.mean(yc * yc, axis=-1, keepdims=True)
    o_b = jax.nn.silu(yc * lax.rsqrt(var + EPS) * lg_ref[...] + lb_ref[...])
    o_ref[...] = jnp.concatenate([o_b, zc[:, :BRANCH_WIDTH] * cat(accs_c)], axis=1).astype(o_ref.dtype)

    new_b = xb_ref[:, tt:tt + CTX_B, :]
    new_c = xc_ref[:, tt:tt + CTX_C, :]
    xb_ref[:, 0:CTX_B, :] = new_b
    xc_ref[:, 0:CTX_C, :] = new_c

    @pl.when(i == pl.num_programs(1) - 1)
    def _():
        nhb_ref[...] = new_b[:, CTX_B - hb_rows:, :]
        nhc_ref[...] = new_c[:, CTX_C - hc_rows:, :]


def conv_branches(z_all, row0, nb, t, tt, nbb, hist_b, hist_c, b_cw, b_cb, ln_g, ln_b, c_cw):
    assert nbb == 1 or tt == t
    nt = t // tt
    rows = nbb * tt
    r0 = row0 // rows
    has_hist = hist_b is not None
    w = BRANCH_WIDTH
    in_specs = [
        pl.BlockSpec((rows, 2 * w), lambda b, i: (r0 + b * nt + i, ZB // (2 * w))),
        pl.BlockSpec((rows, 3 * w), lambda b, i: (r0 + b * nt + i, ZC // (3 * w))),
    ]
    args = [z_all, z_all]
    if has_hist:
        in_specs += [pl.BlockSpec((nbb, B_KERNEL - 1, w), lambda b, i: (b, 0, 0)),
                     pl.BlockSpec((nbb, C_KERNEL - 1, w), lambda b, i: (b, 0, 0))]
        args += [hist_b, hist_c]
    params = [b_cw, b_cb.reshape(1, w), ln_g.reshape(1, w), ln_b.reshape(1, w), c_cw]
    in_specs += [pl.BlockSpec(p.shape, lambda b, i: (0, 0)) for p in params]
    return pl.pallas_call(
        functools.partial(_conv_kernel, tt=tt, nbb=nbb, has_hist=has_hist),
        grid=(nb // nbb, nt),
        in_specs=in_specs,
        out_specs=[pl.BlockSpec((rows, 2 * w), lambda b, i: (b * nt + i, 0)),
                   pl.BlockSpec((nbb, B_KERNEL - 1, w), lambda b, i: (b, 0, 0)),
                   pl.BlockSpec((nbb, C_KERNEL - 1, w), lambda b, i: (b, 0, 0))],
        out_shape=[jax.ShapeDtypeStruct((nb * t, 2 * w), BF16),
                   jax.ShapeDtypeStruct((nb, B_KERNEL - 1, w), F32),
                   jax.ShapeDtypeStruct((nb, C_KERNEL - 1, w), F32)],
        scratch_shapes=[pltpu.VMEM((nbb, CTX_B + tt, w), F32), pltpu.VMEM((nbb, CTX_C + tt, w), F32)],
        compiler_params=_cparams(("parallel", "arbitrary")),
        name="conv_branches",
    )(*args, *params)


HW = D_HEADS * D_KEY
SUB = 16


def _split_dot(a, b_bf16):
    hi = a.astype(BF16)
    lo = (a - hi.astype(F32)).astype(BF16)
    return _dot(hi, b_bf16) + _dot(lo, b_bf16)


def _hgrn_kernel(*refs, tb, seqs, has_state):
    if has_state:
        zd_ref, s0_ref = refs[:2]
        refs = refs[2:]
    else:
        zd_ref = refs[0]
        refs = refs[1:]
    lb_ref, gn_ref, o_ref, sfin_ref, st_ref = refs
    i = pl.program_id(1)
    c = min(SUB, tb // seqs)
    n_sub = tb // c
    assert seqs == 1 or (seqs == n_sub and has_state)

    blocks = [slice(h * D_KEY, (h + 1) * D_KEY) for h in range(D_HEADS)]

    def load_state(j):
        for h, blk in enumerate(blocks):
            st_ref[blk, blk] = s0_ref[j, h]
        return st_ref[...]

    def store_state(j, st):
        for h, blk in enumerate(blocks):
            sfin_ref[j, h] = st[blk, blk]

    if seqs == 1:
        @pl.when(i == 0)
        def _():
            st_ref[...] = jnp.zeros(st_ref.shape, F32)
            if has_state:
                load_state(0)
    else:
        st_ref[...] = jnp.zeros(st_ref.shape, F32)

    zd = zd_ref[...]
    lb = lb_ref[...]
    q = jax.nn.silu(zd[:, 0:HW])
    f = lb + (1.0 - lb) * jax.nn.sigmoid(zd[:, HW:2 * HW])
    logf = jnp.log(f)
    kk = 1.0 - f
    v = zd[:, 2 * HW:3 * HW]
    gate = jax.nn.silu(zd[:, 3 * HW:4 * HW])

    row = lax.broadcasted_iota(I32, (tb, tb), 0)
    col = lax.broadcasted_iota(I32, (tb, tb), 1)
    tril_sub = ((row // c == col // c) & (col <= row)).astype(F32)
    g = _dot(tril_sub, logf, precision=HI)

    hrow = lax.broadcasted_iota(I32, (HW, HW), 0) // D_KEY
    hcol = lax.broadcasted_iota(I32, (HW, HW), 1) // D_KEY
    same_head = hrow == hcol
    head_ones = same_head.astype(BF16)

    def pad_tokens(a):
        return a if tb >= LANES else jnp.concatenate([a, jnp.zeros((LANES - tb, HW), a.dtype)], axis=0)

    v_t = pad_tokens(v).T.astype(BF16)
    tok = lax.broadcasted_iota(I32, (tb, HW), 0)
    s_idx = lax.broadcasted_iota(I32, (c, HW), 0)

    pieces = []
    for j in range(n_sub):
        r0 = j * c
        gj, qj, kj = g[r0:r0 + c], q[r0:r0 + c], kk[r0:r0 + c]
        for t in range(c):
            mask = s_idx <= t
            e = jnp.exp(jnp.where(mask, gj[t:t + 1] - gj, 0.0))
            pieces.append(jnp.where(mask, qj[t:t + 1] * kj * e, 0.0))
    att = _split_dot(jnp.concatenate(pieces, axis=0), head_ones)
    o_intra = jnp.sum(att.reshape(tb, c, HW) * v.reshape(n_sub, 1, c, HW).repeat(c, axis=1).reshape(tb, c, HW), axis=1)
    q_dec = (q * jnp.exp(g)).astype(BF16)

    upds, decays = [], []
    for j in range(n_sub):
        r0 = j * c
        g_last = g[r0 + c - 1:r0 + c]
        in_sub = (tok >= r0) & (tok < r0 + c)
        kd = jnp.where(in_sub, kk * jnp.exp(jnp.where(in_sub, g_last - g, 0.0)), 0.0)
        upds.append(jnp.where(same_head, _dot(v_t, pad_tokens(kd).astype(BF16)), 0.0))
        decays.append(jnp.exp(g_last))

    o_inter = []
    if seqs == 1:
        st = st_ref[...]
        for j in range(n_sub):
            o_inter.append(_dot(q_dec[j * c:(j + 1) * c], st.astype(BF16), NT_DIMS))
            st = st * decays[j] + upds[j]
        st_ref[...] = st

        @pl.when(i == pl.num_programs(1) - 1)
        def _():
            store_state(0, st)
    else:
        for j in range(n_sub):
            st = load_state(j)
            o_inter.append(_dot(q_dec[j * c:(j + 1) * c], st.astype(BF16), NT_DIMS))
            store_state(j, st * decays[j] + upds[j])

    o = (jnp.concatenate(o_inter, axis=0) if n_sub > 1 else o_inter[0]) + o_intra
    ms = _split_dot(o * o, head_ones) * (1.0 / D_VAL)
    o_ref[...] = (o * lax.rsqrt(ms + EPS) * gn_ref[...] * gate).astype(o_ref.dtype)


def hgrn(z_all, row0, nb, t, tb, seqs, s0_t, lb, gn):
    assert (seqs == 1 and t % tb == 0) or tb == seqs * t
    nt = max(t // tb, 1)
    r0 = row0 // tb
    has_state = s0_t is not None
    in_specs = [pl.BlockSpec((tb, 4 * HW), lambda b, i: (r0 + b * nt + i, ZD // (4 * HW)))]
    args = [z_all]
    if has_state:
        in_specs.append(pl.BlockSpec((seqs, D_HEADS, D_VAL, D_KEY), lambda b, i: (b, 0, 0, 0)))
        args.append(s0_t)
    in_specs += [pl.BlockSpec((1, HW), lambda b, i: (0, 0)), pl.BlockSpec((1, HW), lambda b, i: (0, 0))]
    args += [lb.reshape(1, HW), jnp.tile(gn, D_HEADS).reshape(1, HW)]
    return pl.pallas_call(
        functools.partial(_hgrn_kernel, tb=tb, seqs=seqs, has_state=has_state),
        grid=(nb // seqs, nt),
        in_specs=in_specs,
        out_specs=[pl.BlockSpec((tb, HW), lambda b, i: (b * nt + i, 0)),
                   pl.BlockSpec((seqs, D_HEADS, D_VAL, D_KEY), lambda b, i: (b, 0, 0, 0))],
        out_shape=[jax.ShapeDtypeStruct((nb * t, HW), BF16), jax.ShapeDtypeStruct((nb, D_HEADS, D_VAL, D_KEY), F32)],
        scratch_shapes=[pltpu.VMEM((HW, HW), F32)],
        compiler_params=_cparams(("parallel", "arbitrary")),
        name="hgrn",
    )(*args)


def _merge_kernel(oa_ref, obc_ref, od_ref, ga_ref, gb_ref, gc_ref, gd_ref, wa_ref, wb_ref, wc_ref, wd_ref, wo_ref, x_ref, o_ref):
    obc = obc_ref[...]
    sig = lambda ref: jax.nn.sigmoid(ref[...].astype(F32))
    merged = sig(ga_ref) * _dot(oa_ref[...], wa_ref[...])
    merged += sig(gb_ref) * _dot(obc[:, :BRANCH_WIDTH], wb_ref[...])
    merged += sig(gc_ref) * _dot(obc[:, BRANCH_WIDTH:], wc_ref[...])
    merged += sig(gd_ref) * _dot(od_ref[...], wd_ref[...])
    o_ref[...] = x_ref[...] + _dot(merged.astype(BF16), wo_ref[...])


def merge(o_a, o_bc, o_d, gates, wa_p, wb, wc, wd, w_out, x, tm=512):
    n = x.shape[0]
    row = lambda w: pl.BlockSpec((tm, w), lambda i: (i, 0))
    full = lambda a: pl.BlockSpec(a.shape, lambda i: (0, 0))
    gate = lambda k: pl.BlockSpec((tm, D_MODEL), lambda i: (i, k))
    return pl.pallas_call(
        _merge_kernel,
        grid=(n // tm,),
        in_specs=[row(o_a.shape[1]), row(o_bc.shape[1]), row(o_d.shape[1]), gate(0), gate(1), gate(2), gate(3),
                  full(wa_p), full(wb), full(wc), full(wd), full(w_out), row(D_MODEL)],
        out_specs=row(D_MODEL),
        out_shape=jax.ShapeDtypeStruct((n, D_MODEL), F32),
        compiler_params=_cparams(("parallel",)),
        name="merge",
    )(o_a, o_bc, o_d, gates, gates, gates, gates, wa_p, wb, wc, wd, w_out, x)


HP = P_HEADS * 2
SLOTS = P_HEADS * P_TOPK
SUBK = P_DKEY // 2
STAGE1_UNROLL = 8


def _take_max(s, index, n):
    m = jnp.max(s, axis=0, keepdims=True)
    idx = jnp.min(jnp.where(s == m, index, n), axis=0, keepdims=True)
    return m, idx, jnp.where(index == idx, -jnp.inf, s)


_CAND_BLOCKS = [(0, P_TOPK)] + [(a, 8) for a in range(1, 8)]
_CAND_ROWS = sum(nb for _, nb in _CAND_BLOCKS) + 8


def _route_kernel(x_ref, g_ref, wq_ref, keys_ref, h_ref, ii_ref, jj_ref, gw_ref, q_s, sv_s, si_s, oi_s, oj_s, og_s, *, tm):
    x = x_ref[...]
    h = (x * lax.rsqrt(jnp.mean(x * x, axis=-1, keepdims=True) + EPS) * g_ref[...]).astype(BF16)
    h_ref[...] = h
    q_s[...] = _dot(h, wq_ref[...]).astype(BF16)

    half = P_NKEYS // 2
    n_lt = tm // LANES
    iota_lo = lax.broadcasted_iota(I32, (half, LANES), 0).astype(F32)
    iota_hi = iota_lo + float(half)

    def stage1(it, carry):
        hp = it // n_lt
        toks = pl.ds(pl.multiple_of((it % n_lt) * LANES, LANES), LANES)
        q = q_s[toks, pl.ds(pl.multiple_of(hp * SUBK, SUBK), SUBK)]
        s = _dot(keys_ref[hp], q, NT_DIMS)
        a, b = s[:half], s[half:]
        first = a >= b
        cur, cur_i = jnp.where(first, a, b), jnp.where(first, iota_lo, iota_hi)
        nxt, nxt_i = jnp.where(first, b, a), jnp.where(first, iota_hi, iota_lo)
        vals, idxs = [], []
        for _ in range(P_TOPK):
            m = jnp.max(cur, axis=0, keepdims=True)
            idx = jnp.min(jnp.where(cur == m, cur_i, float(P_NKEYS)), axis=0, keepdims=True)
            hit = cur_i == idx
            cur, cur_i, nxt = jnp.where(hit, nxt, cur), jnp.where(hit, nxt_i, cur_i), jnp.where(hit, -jnp.inf, nxt)
            vals.append(m)
            idxs.append(idx)
        sv_s[hp, :, toks] = jnp.concatenate(vals, axis=0)
        si_s[hp, :, toks] = jnp.concatenate(idxs, axis=0)
        return carry

    lax.fori_loop(0, HP * n_lt, stage1, 0, unroll=STAGE1_UNROLL)

    r = lax.broadcasted_iota(I32, (_CAND_ROWS, tm), 0)
    mid = r - P_TOPK
    flat = jnp.where(r < P_TOPK, r, jnp.where(r < _CAND_ROWS - 8, (1 + mid // 8) * P_TOPK + mid % 8, (r - (_CAND_ROWS - 16)) * P_TOPK))
    flat = flat.astype(F32)
    iota_t = lax.broadcasted_iota(I32, (P_TOPK, tm), 0).astype(F32)

    def stage2(hd, carry):
        sv1, sv2 = sv_s[2 * hd], sv_s[2 * hd + 1]
        si1, si2 = si_s[2 * hd], si_s[2 * hd + 1]
        cand = jnp.concatenate([sv1[a:a + 1] + sv2[0:nb] for a, nb in _CAND_BLOCKS] + [sv1[8:P_TOPK] + sv2[0:1]], axis=0)
        fv, ei, ej = [], [], []
        for _ in range(P_TOPK):
            m, idx, cand = _take_max(cand, flat, float(P_TOPK * P_TOPK))
            a = jnp.floor(idx * (1.0 / P_TOPK))
            b = idx - a * P_TOPK
            fv.append(m)
            ei.append(jnp.sum(jnp.where(iota_t == a, si1, 0.0), axis=0, keepdims=True))
            ej.append(jnp.sum(jnp.where(iota_t == b, si2, 0.0), axis=0, keepdims=True))
        fv = jnp.concatenate(fv, axis=0)
        e = jnp.exp(fv - fv[0:1])
        rows = pl.ds(pl.multiple_of(hd * P_TOPK, P_TOPK), P_TOPK)
        og_s[rows, :] = e / jnp.sum(e, axis=0, keepdims=True)
        oi_s[rows, :] = jnp.concatenate(ei, axis=0)
        oj_s[rows, :] = jnp.concatenate(ej, axis=0)
        return carry

    lax.fori_loop(0, P_HEADS, stage2, 0, unroll=2)
    ii_ref[...] = oi_s[...].T.astype(I32)
    jj_ref[...] = oj_s[...].T.astype(I32)
    gw_ref[...] = og_s[...].T


def peer_route(x, g, wq, keys, tm=256):
    n = x.shape[0]
    row = lambda w: pl.BlockSpec((tm, w), lambda i: (i, 0))
    return pl.pallas_call(
        functools.partial(_route_kernel, tm=tm),
        grid=(n // tm,),
        in_specs=[row(D_MODEL), pl.BlockSpec((1, D_MODEL), lambda i: (0, 0)),
                  pl.BlockSpec(wq.shape, lambda i: (0, 0)), pl.BlockSpec(keys.shape, lambda i: (0, 0, 0))],
        out_specs=[row(D_MODEL), row(SLOTS), row(SLOTS), row(SLOTS)],
        out_shape=[jax.ShapeDtypeStruct((n, D_MODEL), BF16), jax.ShapeDtypeStruct((n, SLOTS), I32),
                   jax.ShapeDtypeStruct((n, SLOTS), I32), jax.ShapeDtypeStruct((n, SLOTS), F32)],
        scratch_shapes=[pltpu.VMEM((tm, HP * SUBK), BF16), pltpu.VMEM((HP, P_TOPK, tm), F32), pltpu.VMEM((HP, P_TOPK, tm), F32),
                        pltpu.VMEM((SLOTS, tm), F32), pltpu.VMEM((SLOTS, tm), F32), pltpu.VMEM((SLOTS, tm), F32)],
        compiler_params=_cparams(("parallel",)),
        name="peer_route",
    )(x, g.reshape(1, D_MODEL), wq, keys)


CHUNK_I = 16
CHUNK_E = CHUNK_I * P_NKEYS
N_CHUNKS = P_NKEYS // CHUNK_I


def _peer_kernel(h_ref, ii_ref, jj_ref, gw_ref, u_ref, v_ref, x_ref, o_ref, w_s, acc_s, *, tm):
    c = pl.program_id(1)

    @pl.when(c == 0)
    def _():
        acc_s[...] = jnp.zeros(acc_s.shape, F32)
        iota = lax.broadcasted_iota(I32, (P_NKEYS, SLOTS), 0)

        def build(g, carry):
            t0 = pl.multiple_of(g * PACK, PACK)
            ws = []
            for t in range(PACK):
                irow = ii_ref[pl.ds(t0 + t, 1), :]
                jrow = jj_ref[pl.ds(t0 + t, 1), :]
                grow = gw_ref[pl.ds(t0 + t, 1), :]
                p_t = jnp.where(iota == irow, grow, 0.0).astype(BF16)
                q_t = jnp.where(iota == jrow, 1.0, 0.0).astype(BF16)
                ws.append(_dot(p_t, q_t, NT_DIMS).astype(BF16))
            w_s[:, pl.ds(t0, PACK), :] = pltpu.einshape("tij->itj", jnp.stack(ws, axis=0))
            return carry

        lax.fori_loop(0, tm // PACK, build, 0, unroll=2)

    a = _dot(h_ref[...], u_ref[...], NT_DIMS)
    act = 0.5 * a * (1.0 + lax.erf(a * (1.0 / math.sqrt(2.0))))
    wd = jnp.concatenate([w_s[c * CHUNK_I + r] for r in range(CHUNK_I)], axis=1)
    acc_s[...] += _dot(act.astype(BF16) * wd, v_ref[...])

    @pl.when(c == N_CHUNKS - 1)
    def _():
        o_ref[...] = x_ref[...] + acc_s[...]


def peer_experts(layer, h2, ii, jj, gw, u_tabs, v_tabs, x, tm=512):
    n = x.shape[0]
    row = lambda w: pl.BlockSpec((tm, w), lambda i, c: (i, 0))
    return pl.pallas_call(
        functools.partial(_peer_kernel, tm=tm),
        grid=(n // tm, N_CHUNKS),
        in_specs=[row(D_MODEL), row(SLOTS), row(SLOTS), row(SLOTS),
                  pl.BlockSpec((None, CHUNK_E, D_MODEL), lambda i, c: (layer, c, 0)),
                  pl.BlockSpec((None, CHUNK_E, D_MODEL), lambda i, c: (layer, c, 0)),
                  row(D_MODEL)],
        out_specs=row(D_MODEL),
        out_shape=jax.ShapeDtypeStruct((n, D_MODEL), F32),
        scratch_shapes=[pltpu.VMEM((P_NKEYS, tm, P_NKEYS), BF16), pltpu.VMEM((tm, D_MODEL), F32)],
        compiler_params=_cparams(("parallel", "arbitrary")),
        name="peer_experts",
    )(h2, ii, jj, gw, u_tabs, v_tabs, x)


def _rmsnorm_kernel(x_ref, g_ref, o_ref):
    x = x_ref[...]
    o_ref[...] = x * lax.rsqrt(jnp.mean(x * x, axis=-1, keepdims=True) + EPS) * g_ref[...]


def rmsnorm(x, g, row0, n, tm=512):
    d = x.shape[1]
    r0 = row0 // tm
    return pl.pallas_call(
        _rmsnorm_kernel,
        grid=(n // tm,),
        in_specs=[pl.BlockSpec((tm, d), lambda i: (r0 + i, 0)), pl.BlockSpec((1, d), lambda i: (0, 0))],
        out_specs=pl.BlockSpec((tm, d), lambda i: (i, 0)),
        out_shape=jax.ShapeDtypeStruct((n, d), F32),
        compiler_params=_cparams(("parallel",)),
        name="rmsnorm",
    )(x, g.reshape(1, d))


def _rope_tables():
    half = A_ROPE // 2
    inv = ROPE_THETA ** (-jnp.arange(half, dtype=F32) / half)
    pos = jnp.concatenate([jnp.arange(SEQ), PAST_LEN + jnp.arange(DEC_SEQ)])
    posf = pos.astype(F32)[:, None]

    def tables(start):
        zeros = jnp.zeros((LANES - start - A_ROPE,), F32)
        freq = jnp.concatenate([jnp.zeros((start,), F32), inv, inv, zeros])
        keep = jnp.concatenate([jnp.ones((start + A_ROPE,), F32), zeros])
        sign = jnp.concatenate([jnp.zeros((start,), F32), -jnp.ones((half,), F32), jnp.ones((half,), F32), zeros])
        ang = posf * freq[None, :]
        per_token = lambda t: jnp.concatenate([jnp.tile(t[:SEQ], (BATCH, 1)), jnp.tile(t[SEQ:], (DEC_BATCH, 1))], axis=0)
        return per_token(jnp.cos(ang) * keep[None, :]), per_token(jnp.sin(ang) * sign[None, :])

    ccq, ssq = tables(A_NOPE)
    cck, ssk = tables(0)
    return ccq, ssq, cck, ssk


def _swap_halves(w):
    half = w.shape[-1] // 2
    return jnp.concatenate([w[..., half:], w[..., :half]], axis=-1)


def _layer_weights(l, w_in, a_w_uq, a_w_uk, a_w_uv, w_branch, w_out, p_w_q, p_sub_keys):
    w = w_in[l]
    o = 0
    parts = []
    for size in (A_Q_LORA, A_KV_LORA, A_ROPE, 2 * BRANCH_WIDTH, 3 * BRANCH_WIDTH, 4 * HW, N_BRANCH * D_MODEL):
        parts.append(w[:, o:o + size])
        o += size
    cq, ckv, kr, b_in, c_in, d_in, gate = parts
    pad = jnp.zeros((D_MODEL, ZB - A_Q_LORA - A_KV_LORA - 2 * A_ROPE), F32)
    pad_c = jnp.zeros((D_MODEL, ZC - ZD - 4 * HW), F32)
    w_in_p = jnp.concatenate([cq, ckv, kr, _swap_halves(kr), pad, b_in, d_in, pad_c, c_in], axis=1).astype(BF16)

    wq = a_w_uq[l].reshape(A_Q_LORA, A_HEADS, A_NOPE + A_ROPE)
    nope, rope = wq[..., :A_NOPE], wq[..., A_NOPE:]
    tail = jnp.zeros((A_Q_LORA, A_HEADS, HEAD_PAD - A_NOPE - A_ROPE), F32)
    wq_main = jnp.concatenate([nope, rope, tail], axis=-1).reshape(A_Q_LORA, -1).astype(BF16)
    wq_swap = jnp.concatenate([jnp.zeros_like(nope), _swap_halves(rope), tail], axis=-1).reshape(A_Q_LORA, -1).astype(BF16)

    w_uk, w_uv = a_w_uk[l], a_w_uv[l]
    head_tail = jnp.zeros((A_KV_LORA, A_HEADS, HEAD_PAD - A_NOPE), F32)
    wuk_p = jnp.concatenate([w_uk, head_tail], axis=-1).reshape(A_KV_LORA, -1).astype(BF16)
    wuv_p = jnp.concatenate([w_uv, head_tail], axis=-1).reshape(A_KV_LORA, -1).astype(BF16)

    r = jnp.arange(LANES)[:, None]
    col = jnp.arange(A_HEADS * HEAD_PAD)[None, :]
    place = ((col % HEAD_PAD == A_NOPE + r) & (r < A_ROPE)).astype(BF16)

    blk = jnp.zeros((A_HEADS, HEAD_PAD, 2 * LANES), F32)
    blk = blk.at[:, :A_NOPE, :A_KV_LORA].set(jnp.transpose(w_uk, (1, 2, 0)))
    blk = blk.at[:, A_NOPE:A_NOPE + A_ROPE, A_KV_LORA:A_KV_LORA + A_ROPE].set(jnp.eye(A_ROPE, dtype=F32))
    eye_h = jnp.eye(A_HEADS, dtype=F32)
    wabs = (blk[:, :, None, :] * eye_h[:, None, :, None]).reshape(A_HEADS * HEAD_PAD, A_HEADS * 2 * LANES).astype(BF16)

    wb = w_branch[l]
    wa_p = jnp.concatenate([wb[0].reshape(A_HEADS, A_V, D_MODEL), jnp.zeros((A_HEADS, HEAD_PAD - A_V, D_MODEL), F32)],
                           axis=1).reshape(A_HEADS * HEAD_PAD, D_MODEL).astype(BF16)
    return dict(w_in_p=w_in_p, w_gate=gate.astype(BF16), wq_main=wq_main, wq_swap=wq_swap, wuk_p=wuk_p, wuv_p=wuv_p, place=place, wabs=wabs,
                wa_p=wa_p, wb=wb[1].astype(BF16), wc=wb[2].astype(BF16), wd=wb[3].astype(BF16), w_out=w_out[l].astype(BF16),
                wq=p_w_q[l].astype(BF16), keys=p_sub_keys[l].reshape(HP, P_NKEYS, SUBK).astype(BF16))


def _swap_state(s):
    return jnp.transpose(s, (0, 1, 3, 2))


def kernel(x_prompt, x_sample, cache_ckv, cache_krope, page_table, state_conv_b, state_conv_c, state_hgrn, norm1_g, w_in, a_q_norm_g, a_w_uq, a_kv_norm_g, a_w_uk, a_w_uv, b_conv_w, b_conv_b, b_ln_g, b_ln_b, c_conv_w, d_lower_bound, d_gnorm_g, w_branch, w_out, norm2_g, p_w_q, p_sub_keys, p_u, p_v, final_norm_g):
    lb_soft = jax.nn.softmax(d_lower_bound.astype(F32), axis=0)
    lower_bounds = jnp.cumsum(lb_soft, axis=0) - lb_soft[0:1]
    ccq, ssq, cck, ssk = _rope_tables()
    x = jnp.concatenate([x_prompt.reshape(NP, D_MODEL), x_sample.reshape(NS, D_MODEL)], axis=0)
    cache_krope_t = jnp.swapaxes(cache_krope, 2, 3)
    u_tabs, v_tabs = p_u.astype(BF16), p_v.astype(BF16)

    states_p, states_s = [], []
    for l in range(DEPTH):
        w = _layer_weights(l, w_in, a_w_uq, a_w_uk, a_w_uv, w_branch, w_out, p_w_q, p_sub_keys)
        z = norm_matmul(x, norm1_g[l], w["w_in_p"], 1024, Z_COLS // 2, F32)
        gates = norm_matmul(x, norm1_g[l], w["w_gate"], 1024, G_COLS // 2, BF16)

        ckv, kr, qp, kp, vp = mla_prep(z, a_q_norm_g[l], a_kv_norm_g[l], w["wq_main"], w["wq_swap"], w["wuk_p"], w["wuv_p"],
                                       w["place"], ccq, ssq, cck, ssk)
        oa_p = flash_prompt(qp, kp, vp)
        qlat = matmul(qp[NP:], w["wabs"], 512, BF16)
        qlat = jnp.transpose(qlat.reshape(DEC_BATCH, DEC_SEQ, A_HEADS, 2 * LANES), (0, 2, 1, 3)).reshape(DEC_BATCH, DEC_ROWS, 2 * LANES)
        oa_s = mla_decode(l, page_table, qlat, cache_ckv, cache_krope_t, ckv[NP:].reshape(DEC_BATCH, DEC_SEQ, LANES),
                          kr[NP:].reshape(DEC_BATCH, DEC_SEQ, LANES), w["wuv_p"])
        o_a = jnp.concatenate([oa_p, oa_s.reshape(NS, -1)], axis=0)

        conv_w = (b_conv_w[l], b_conv_b[l], b_ln_g[l], b_ln_b[l], c_conv_w[l])
        obc_p, hb_p, hc_p = conv_branches(z, 0, BATCH, SEQ, 512, 1, None, None, *conv_w)
        obc_s, hb_s, hc_s = conv_branches(z, NP, DEC_BATCH, DEC_SEQ, DEC_SEQ, SEQS_PER_STEP, state_conv_b[l], state_conv_c[l], *conv_w)
        o_bc = jnp.concatenate([obc_p, obc_s], axis=0)

        od_p, st_p = hgrn(z, 0, BATCH, SEQ, LANES, 1, None, lower_bounds[l], d_gnorm_g[l])
        od_s, st_s = hgrn(z, NP, DEC_BATCH, DEC_SEQ, SEQS_PER_STEP * DEC_SEQ, SEQS_PER_STEP, _swap_state(state_hgrn[l]), lower_bounds[l], d_gnorm_g[l])
        o_d = jnp.concatenate([od_p, od_s], axis=0)

        x1 = merge(o_a, o_bc, o_d, gates, w["wa_p"], w["wb"], w["wc"], w["wd"], w["w_out"], x)
        h2, ii, jj, gw = peer_route(x1, norm2_g[l], w["wq"], w["keys"])
        x = peer_experts(l, h2, ii, jj, gw, u_tabs, v_tabs, x1)

        states_p.append((ckv[:NP].reshape(BATCH, SEQ, A_KV_LORA), kr[:NP, :A_ROPE].reshape(BATCH, SEQ, A_ROPE),
                         hb_p, hc_p, _swap_state(st_p)))
        states_s.append((ckv[NP:].reshape(DEC_BATCH, DEC_SEQ, A_KV_LORA), kr[NP:, :A_ROPE].reshape(DEC_BATCH, DEC_SEQ, A_ROPE),
                         hb_s, hc_s, _swap_state(st_s)))

    y_p = rmsnorm(x, final_norm_g, 0, NP)
    y_s = rmsnorm(x, final_norm_g, NP, NS)
    stack = lambda states: [jnp.stack([s[i] for s in states], axis=0) for i in range(5)]
    return (y_p.reshape(BATCH, SEQ, D_MODEL), y_s.reshape(DEC_BATCH, DEC_SEQ, D_MODEL), *stack(states_p), *stack(states_s))
```

```python
import functools
import math

import jax
import jax.numpy as jnp
from jax import lax
from jax.experimental import pallas as pl
from jax.experimental.pallas import tpu as pltpu

F32 = jnp.float32
BF16 = jnp.bfloat16
I32 = jnp.int32

D_MODEL = 1024
BATCH = 2
SEQ = 8192
DEPTH = 2
DEC_BATCH = 128
DEC_SEQ = 8
PAST_LEN = 8192
PAGE_SIZE = 128
N_PAGES = PAST_LEN // PAGE_SIZE
BRANCH_WIDTH = 256
N_BRANCH = 4
A_HEADS = 4
A_NOPE = 64
A_ROPE = 32
A_V = 64
A_Q_LORA = 256
A_KV_LORA = 128
ROPE_THETA = 10000.0
B_KERNEL = 31
C_KERNEL = 3
D_HEADS = 4
D_KEY = 64
D_VAL = 64
P_HEADS = 8
P_NKEYS = 128
P_DKEY = 256
P_TOPK = 16
P_EXPERTS = P_NKEYS * P_NKEYS
EPS = 1e-6
NEG_BIG = -1e30

NP = BATCH * SEQ
NS = DEC_BATCH * DEC_SEQ
NT = NP + NS

LANES = 128
SUBLANES = 8
PACK = 16
HEAD_PAD = 128
ZA, ZB, ZD, ZG, ZC = 0, 512, 1024, 2048, 6144
Z_COLS = 6912
VMEM_LIMIT = 56 * 1024 * 1024

TOKEN_TILE = 512
PROJ_TILE = (1024, 2304)
ROUTE_TILE = 256
EXPERT_TILE = 512
FLASH_TILE = (1024, 1024)

HI = lax.Precision.HIGHEST


def _cparams(sem):
    return pltpu.CompilerParams(dimension_semantics=sem, vmem_limit_bytes=VMEM_LIMIT)


def _dot(a, b, dims=(((1,), (0,)), ((), ())), precision=None):
    return lax.dot_general(a, b, dims, precision=precision, preferred_element_type=F32)


NT_DIMS = (((1,), (1,)), ((), ()))
TN_DIMS = (((0,), (0,)), ((), ()))


def _norm_matmul_kernel(x_ref, g_ref, w_ref, o_ref, h_ref):
    @pl.when(pl.program_id(1) == 0)
    def _():
        x = x_ref[...]
        y = x * lax.rsqrt(jnp.mean(x * x, axis=-1, keepdims=True) + EPS)
        h_ref[...] = (y * g_ref[...]).astype(BF16)

    o_ref[...] = _dot(h_ref[...], w_ref[...])


def norm_matmul(x, g, w, tm, tn):
    n, d = x.shape
    cols = w.shape[1]
    return pl.pallas_call(
        _norm_matmul_kernel,
        grid=(n // tm, cols // tn),
        in_specs=[
            pl.BlockSpec((tm, d), lambda i, j: (i, 0)),
            pl.BlockSpec((1, d), lambda i, j: (0, 0)),
            pl.BlockSpec((d, tn), lambda i, j: (0, j)),
        ],
        out_specs=pl.BlockSpec((tm, tn), lambda i, j: (i, j)),
        out_shape=jax.ShapeDtypeStruct((n, cols), F32),
        scratch_shapes=[pltpu.VMEM((tm, d), BF16)],
        compiler_params=_cparams(("parallel", "arbitrary")),
        name="norm_matmul",
    )(x, g.reshape(1, d), w)


def _matmul_kernel(x_ref, w_ref, o_ref):
    o_ref[...] = _dot(x_ref[...], w_ref[...]).astype(o_ref.dtype)


def matmul(x, w, tm, out_dtype):
    n, d = x.shape
    cols = w.shape[1]
    return pl.pallas_call(
        _matmul_kernel,
        grid=(n // tm,),
        in_specs=[pl.BlockSpec((tm, d), lambda i: (i, 0)), pl.BlockSpec((d, cols), lambda i: (0, 0))],
        out_specs=pl.BlockSpec((tm, cols), lambda i: (i, 0)),
        out_shape=jax.ShapeDtypeStruct((n, cols), out_dtype),
        compiler_params=_cparams(("parallel",)),
        name="matmul",
    )(x, w)


def _tile4(t):
    return jnp.concatenate([t, t, t, t], axis=1)


def _mla_prep_kernel(z_ref, qn_ref, kvn_ref, wqm_ref, wqs_ref, wuk_ref, wuv_ref, place_ref,
                     ccq_ref, ssq_ref, cck_ref, ssk_ref,
                     ckv_ref, kr_ref, q_ref, k_ref, v_ref):
    z = z_ref[...]
    cq = z[:, 0:A_Q_LORA]
    cqn = cq * lax.rsqrt(jnp.mean(cq * cq, axis=-1, keepdims=True) + EPS) * qn_ref[...]
    cqn = cqn.astype(BF16)
    scale = 1.0 / math.sqrt(A_NOPE + A_ROPE)
    q = _dot(cqn, wqm_ref[...]) * _tile4(ccq_ref[...]) + _dot(cqn, wqs_ref[...]) * _tile4(ssq_ref[...])
    q_ref[...] = (q * scale).astype(BF16)

    c = z[:, A_Q_LORA:A_Q_LORA + A_KV_LORA]
    ckv = c * lax.rsqrt(jnp.mean(c * c, axis=-1, keepdims=True) + EPS) * kvn_ref[...]
    ckv_ref[...] = ckv
    ckv_b = ckv.astype(BF16)

    kc = z[:, A_Q_LORA + A_KV_LORA:]
    kr = kc * cck_ref[...] + pltpu.roll(kc, LANES - A_ROPE, axis=1) * ssk_ref[...]
    kr_ref[...] = kr

    k = _dot(ckv_b, wuk_ref[...]) + _dot(kr.astype(BF16), place_ref[...])
    k_ref[...] = k.astype(BF16)
    lane = lax.broadcasted_iota(I32, (1, A_HEADS * HEAD_PAD), 1) % HEAD_PAD
    v_ref[...] = (_dot(ckv_b, wuv_ref[...]) + (lane == A_V).astype(F32)).astype(BF16)


def mla_prep(z_all, qn_g, kvn_g, wq_main, wq_swap, wuk_p, wuv_p, place, ccq, ssq, cck, ssk, tm=TOKEN_TILE):
    n = z_all.shape[0]
    hp = A_HEADS * HEAD_PAD
    row = lambda w: pl.BlockSpec((tm, w), lambda i: (i, 0))
    full = lambda a: pl.BlockSpec(a.shape, lambda i: (0,) * a.ndim)
    qn_g = qn_g.reshape(1, -1)
    kvn_g = kvn_g.reshape(1, -1)
    return pl.pallas_call(
        _mla_prep_kernel,
        grid=(n // tm,),
        in_specs=[row(512), full(qn_g), full(kvn_g), full(wq_main), full(wq_swap), full(wuk_p), full(wuv_p), full(place),
                  row(LANES), row(LANES), row(LANES), row(LANES)],
        out_specs=[row(LANES), row(LANES), row(hp), row(hp), row(hp)],
        out_shape=[jax.ShapeDtypeStruct((n, LANES), F32), jax.ShapeDtypeStruct((n, LANES), F32),
                   jax.ShapeDtypeStruct((n, hp), BF16), jax.ShapeDtypeStruct((n, hp), BF16),
                   jax.ShapeDtypeStruct((n, hp), BF16)],
        compiler_params=_cparams(("parallel",)),
        name="mla_prep",
    )(z_all, qn_g, kvn_g, wq_main, wq_swap, wuk_p, wuv_p, place, ccq, ssq, cck, ssk)


def _flash_kernel(q_ref, k_ref, v_ref, o_ref, *, tq, tk):
    i = pl.program_id(2)
    q = q_ref[...]

    def step(j, carry, masked):
        m, acc = carry
        kj = k_ref[pl.ds(pl.multiple_of(j * tk, tk), tk), :]
        vj = v_ref[pl.ds(pl.multiple_of(j * tk, tk), tk), :]
        s = _dot(q, kj, NT_DIMS)
        if masked:
            qpos = i * tq + lax.broadcasted_iota(I32, (tq, tk), 0)
            kpos = j * tk + lax.broadcasted_iota(I32, (tq, tk), 1)
            s = jnp.where(kpos <= qpos, s, NEG_BIG)
        m_new = jnp.maximum(m, jnp.max(s, axis=-1, keepdims=True))
        p = jnp.exp(s - m_new)
        acc = jnp.exp(m - m_new) * acc + _dot(p.astype(BF16), vj)
        return m_new, acc

    init = (jnp.full((tq, 1), NEG_BIG, F32), jnp.zeros((tq, HEAD_PAD), F32))
    n_full = (i * tq) // tk
    carry = lax.fori_loop(0, n_full // 2, lambda p, c: step(2 * p + 1, step(2 * p, c, False), False), init)
    carry = lax.fori_loop(0, n_full % 2, lambda _, c: step(n_full - 1, c, False), carry)
    for d in range(tq // tk):
        carry = step(n_full + d, carry, True)
    m, acc = carry
    o_ref[...] = (acc / acc[:, A_V:A_V + 1]).astype(o_ref.dtype)


def flash_prompt(qp, kp, vp, tq=FLASH_TILE[0], tk=FLASH_TILE[1]):
    assert tq % tk == 0
    nq = SEQ // tq
    return pl.pallas_call(
        functools.partial(_flash_kernel, tq=tq, tk=tk),
        grid=(BATCH, A_HEADS, nq),
        in_specs=[
            pl.BlockSpec((tq, HEAD_PAD), lambda b, h, i: (b * nq + i, h)),
            pl.BlockSpec((SEQ, HEAD_PAD), lambda b, h, i: (b, h)),
            pl.BlockSpec((SEQ, HEAD_PAD), lambda b, h, i: (b, h)),
        ],
        out_specs=pl.BlockSpec((tq, HEAD_PAD), lambda b, h, i: (b * nq + i, h)),
        out_shape=jax.ShapeDtypeStruct((NP, A_HEADS * HEAD_PAD), BF16),
        compiler_params=_cparams(("parallel", "parallel", "arbitrary")),
        name="flash_prompt",
    )(qp, kp, vp)


DEC_ROWS = A_HEADS * DEC_SEQ
DEC_SLOTS = 2


def _decode_kernel(pt_ref, q_ref, cnew_ref, knew_ref, wuv_ref, ckv_hbm, kr_hbm, o_ref, kbuf, krbuf, sem, *, page0):
    b = pl.program_id(0)
    slot = b % DEC_SLOTS

    def page_copies(seq, sl, r):
        page = page0 + pt_ref[seq, r]
        return (pltpu.make_async_copy(ckv_hbm.at[page], kbuf.at[sl, r], sem.at[0, sl]),
                pltpu.make_async_copy(kr_hbm.at[page], krbuf.at[sl, r], sem.at[1, sl]))

    def start_fetch(seq, sl):
        def body(r, carry):
            for queue, cp in enumerate(page_copies(seq, sl, r)):
                cp.start(priority=queue)
            return carry
        lax.fori_loop(0, N_PAGES, body, 0)

    @pl.when(b == 0)
    def _():
        for ahead in range(DEC_SLOTS - 1):
            start_fetch(ahead, ahead)

    @pl.when(b + DEC_SLOTS - 1 < pl.num_programs(0))
    def _():
        start_fetch(b + DEC_SLOTS - 1, (b + DEC_SLOTS - 1) % DEC_SLOTS)

    def wait_body(r, carry):
        for cp in page_copies(b, slot, r):
            cp.wait()
        return carry

    lax.fori_loop(0, N_PAGES, wait_body, 0)

    q = q_ref[...]
    q_lat = q[:, :A_KV_LORA]
    q_rope = q[:, A_KV_LORA:A_KV_LORA + A_ROPE]
    kc = kbuf[slot].reshape(PAST_LEN, A_KV_LORA).astype(BF16)
    kr_t = jnp.concatenate([krbuf[slot, r] for r in range(N_PAGES)], axis=1).astype(BF16)
    s_past = _dot(q_lat, kc, NT_DIMS) + _dot(q_rope, kr_t)

    cn = cnew_ref[...].astype(BF16)
    kn = knew_ref[...][:, :A_ROPE].astype(BF16)
    s_new = _dot(q_lat, cn, NT_DIMS) + _dot(q_rope, kn, NT_DIMS)
    t_q = lax.broadcasted_iota(I32, s_new.shape, 0) % DEC_SEQ
    t_k = lax.broadcasted_iota(I32, s_new.shape, 1)
    s_new = jnp.where(t_k <= t_q, s_new, NEG_BIG)

    m = jnp.maximum(jnp.max(s_past, axis=-1, keepdims=True), jnp.max(s_new, axis=-1, keepdims=True))
    p_past = jnp.exp(s_past - m)
    p_new = jnp.exp(s_new - m)
    l = jnp.sum(p_past, axis=-1, keepdims=True) + jnp.sum(p_new, axis=-1, keepdims=True)
    acc = _dot(p_past.astype(BF16), kc) + _dot(p_new.astype(BF16), cn)
    o_lat = (acc / l).astype(BF16)
    for h in range(A_HEADS):
        o_ref[:, h * HEAD_PAD:(h + 1) * HEAD_PAD] = _dot(
            o_lat[h * DEC_SEQ:(h + 1) * DEC_SEQ, :], wuv_ref[:, h * HEAD_PAD:(h + 1) * HEAD_PAD]).astype(o_ref.dtype)


def mla_decode(layer, page_table, qlat, cache_ckv, cache_krope_t, ckv_new, kr_new, wuv_p):
    n_pool = cache_ckv.shape[1]
    hbm = pl.BlockSpec(memory_space=pl.ANY)
    grid_spec = pltpu.PrefetchScalarGridSpec(
        num_scalar_prefetch=1,
        grid=(DEC_BATCH,),
        in_specs=[
            pl.BlockSpec((None, DEC_ROWS, 2 * LANES), lambda b, pt: (b, 0, 0)),
            pl.BlockSpec((None, DEC_SEQ, LANES), lambda b, pt: (b, 0, 0)),
            pl.BlockSpec((None, DEC_SEQ, LANES), lambda b, pt: (b, 0, 0)),
            pl.BlockSpec(wuv_p.shape, lambda b, pt: (0, 0)),
            hbm, hbm,
        ],
        out_specs=pl.BlockSpec((None, DEC_SEQ, A_HEADS * HEAD_PAD), lambda b, pt: (b, 0, 0)),
        scratch_shapes=[pltpu.VMEM((DEC_SLOTS, N_PAGES, PAGE_SIZE, A_KV_LORA), F32),
                        pltpu.VMEM((DEC_SLOTS, N_PAGES, A_ROPE, PAGE_SIZE), F32), pltpu.SemaphoreType.DMA((2, DEC_SLOTS))],
    )
    return pl.pallas_call(
        functools.partial(_decode_kernel, page0=layer * n_pool),
        grid_spec=grid_spec,
        out_shape=jax.ShapeDtypeStruct((DEC_BATCH, DEC_SEQ, A_HEADS * HEAD_PAD), BF16),
        compiler_params=_cparams(("arbitrary",)),
        name="mla_decode",
    )(page_table, qlat, ckv_new, kr_new, wuv_p, cache_ckv.reshape(-1, PAGE_SIZE, A_KV_LORA), cache_krope_t.reshape(-1, A_ROPE, PAGE_SIZE))


CTX_B = 32
CTX_C = 8
SEQS_PER_STEP = 8


def _conv_kernel(*refs, tt, nbb, has_hist):
    if has_hist:
        zb_ref, zc_ref, hb_ref, hc_ref = refs[:4]
        refs = refs[4:]
    else:
        zb_ref, zc_ref = refs[:2]
        refs = refs[2:]
    bw_ref, bb_ref, lg_ref, lb_ref, cw_ref, o_ref, nhb_ref, nhc_ref, xb_ref, xc_ref = refs
    i = pl.program_id(1)
    hb_rows, hc_rows = B_KERNEL - 1, C_KERNEL - 1

    @pl.when(i == 0)
    def _():
        xb_ref[:, 0:CTX_B, :] = jnp.zeros((nbb, CTX_B, BRANCH_WIDTH), F32)
        xc_ref[:, 0:CTX_C, :] = jnp.zeros((nbb, CTX_C, BRANCH_WIDTH), F32)
        if has_hist:
            xb_ref[:, CTX_B - hb_rows:CTX_B, :] = hb_ref[...]
            xc_ref[:, CTX_C - hc_rows:CTX_C, :] = hc_ref[...]

    zb = zb_ref[...]
    zc = zc_ref[...]
    glu = zb[:, :BRANCH_WIDTH] * jax.nn.sigmoid(zb[:, BRANCH_WIDTH:])
    u_c = zc[:, BRANCH_WIDTH:2 * BRANCH_WIDTH] * zc[:, 2 * BRANCH_WIDTH:]
    accs_b, accs_c = [], []
    for s in range(nbb):
        xb_ref[s, CTX_B:CTX_B + tt, :] = glu[s * tt:(s + 1) * tt]
        xc_ref[s, CTX_C:CTX_C + tt, :] = u_c[s * tt:(s + 1) * tt]
        x_all = xb_ref[s]
        acc = jnp.zeros((tt, BRANCH_WIDTH), F32)
        for r in range(SUBLANES):
            x_r = x_all if r == 0 else pltpu.roll(x_all, CTX_B + tt - r, axis=0)
            for k in range(B_KERNEL):
                off = CTX_B - hb_rows + k
                if off % SUBLANES == r:
                    acc = acc + x_r[off - r:off - r + tt] * bw_ref[k:k + 1, :]
        accs_b.append(acc)
        acc = jnp.zeros((tt, BRANCH_WIDTH), F32)
        for k in range(C_KERNEL):
            acc = acc + xc_ref[s, pl.ds(CTX_C - hc_rows + k, tt), :] * cw_ref[k:k + 1, :]
        accs_c.append(acc)
    cat = lambda parts: parts[0] if nbb == 1 else jnp.concatenate(parts, axis=0)
    y = cat(accs_b) + bb_ref[...]
    mu = jnp.mean(y, axis=-1, keepdims=True)
    yc = y - mu
    var = jnp.mean(yc * yc, axis=-1, keepdims=True)
    o_b = jax.nn.silu(yc * lax.rsqrt(var + EPS) * lg_ref[...] + lb_ref[...])
    o_ref[...] = jnp.concatenate([o_b, zc[:, :BRANCH_WIDTH] * cat(accs_c)], axis=1).astype(o_ref.dtype)

    new_b = xb_ref[:, tt:tt + CTX_B, :]
    new_c = xc_ref[:, tt:tt + CTX_C, :]
    xb_ref[:, 0:CTX_B, :] = new_b
    xc_ref[:, 0:CTX_C, :] = new_c

    @pl.when(i == pl.num_programs(1) - 1)
    def _():
        nhb_ref[...] = new_b[:, CTX_B - hb_rows:, :]
        nhc_ref[...] = new_c[:, CTX_C - hc_rows:, :]


def conv_branches(z_all, row0, nb, t, tt, nbb, hist_b, hist_c, b_cw, b_cb, ln_g, ln_b, c_cw):
    assert nbb == 1 or tt == t
    nt = t // tt
    rows = nbb * tt
    r0 = row0 // rows
    has_hist = hist_b is not None
    w = BRANCH_WIDTH
    in_specs = [
        pl.BlockSpec((rows, 2 * w), lambda b, i: (r0 + b * nt + i, ZB // (2 * w))),
        pl.BlockSpec((rows, 3 * w), lambda b, i: (r0 + b * nt + i, ZC // (3 * w))),
    ]
    args = [z_all, z_all]
    if has_hist:
        in_specs += [pl.BlockSpec((nbb, B_KERNEL - 1, w), lambda b, i: (b, 0, 0)),
                     pl.BlockSpec((nbb, C_KERNEL - 1, w), lambda b, i: (b, 0, 0))]
        args += [hist_b, hist_c]
    params = [b_cw, b_cb.reshape(1, w), ln_g.reshape(1, w), ln_b.reshape(1, w), c_cw]
    in_specs += [pl.BlockSpec(p.shape, lambda b, i: (0, 0)) for p in params]
    return pl.pallas_call(
        functools.partial(_conv_kernel, tt=tt, nbb=nbb, has_hist=has_hist),
        grid=(nb // nbb, nt),
        in_specs=in_specs,
        out_specs=[pl.BlockSpec((rows, 2 * w), lambda b, i: (b * nt + i, 0)),
                   pl.BlockSpec((nbb, B_KERNEL - 1, w), lambda b, i: (b, 0, 0)),
                   pl.BlockSpec((nbb, C_KERNEL - 1, w), lambda b, i: (b, 0, 0))],
        out_shape=[jax.ShapeDtypeStruct((nb * t, 2 * w), BF16),
                   jax.ShapeDtypeStruct((nb, B_KERNEL - 1, w), F32),
                   jax.ShapeDtypeStruct((nb, C_KERNEL - 1, w), F32)],
        scratch_shapes=[pltpu.VMEM((nbb, CTX_B + tt, w), F32), pltpu.VMEM((nbb, CTX_C + tt, w), F32)],
        compiler_params=_cparams(("parallel", "arbitrary")),
        name="conv_branches",
    )(*args, *params)


HW = D_HEADS * D_KEY
SUB = 16


def _hgrn_kernel(*refs, tb, seqs, has_state):
    if has_state:
        zd_ref, s0_ref = refs[:2]
        refs = refs[2:]
    else:
        zd_ref = refs[0]
        refs = refs[1:]
    lb_ref, gn_ref, o_ref, sfin_ref, st_ref = refs
    i = pl.program_id(1)
    c = min(SUB, tb // seqs)
    n_sub = tb // c
    assert seqs == 1 or (seqs == n_sub and has_state)

    blocks = [slice(h * D_KEY, (h + 1) * D_KEY) for h in range(D_HEADS)]

    def load_state(j):
        for h, blk in enumerate(blocks):
            st_ref[blk, blk] = s0_ref[j, h]
        return st_ref[...]

    def store_state(j, st):
        for h, blk in enumerate(blocks):
            sfin_ref[j, h] = st[blk, blk]

    if seqs == 1:
        @pl.when(i == 0)
        def _():
            st_ref[...] = jnp.zeros(st_ref.shape, F32)
            if has_state:
                load_state(0)
    else:
        st_ref[...] = jnp.zeros(st_ref.shape, F32)

    zd = zd_ref[...]
    lb = lb_ref[...]
    q = jax.nn.silu(zd[:, 0:HW])
    f = lb + (1.0 - lb) * jax.nn.sigmoid(zd[:, HW:2 * HW])
    logf = jnp.log(f)
    kk = 1.0 - f
    v = zd[:, 2 * HW:3 * HW]
    gate = jax.nn.silu(zd[:, 3 * HW:4 * HW])

    row = lax.broadcasted_iota(I32, (tb, tb), 0)
    col = lax.broadcasted_iota(I32, (tb, tb), 1)
    tril_sub = ((row // c == col // c) & (col <= row)).astype(F32)
    g = _dot(tril_sub, logf, precision=HI)

    hrow = lax.broadcasted_iota(I32, (HW, HW), 0) // D_KEY
    hcol = lax.broadcasted_iota(I32, (HW, HW), 1) // D_KEY
    same_head = hrow == hcol
    head_ones = same_head.astype(BF16)

    def pad_tokens(a):
        return a if tb >= LANES else jnp.concatenate([a, jnp.zeros((LANES - tb, HW), a.dtype)], axis=0)

    v_t = pad_tokens(v).T.astype(BF16)
    tok = lax.broadcasted_iota(I32, (tb, HW), 0)
    s_idx = lax.broadcasted_iota(I32, (c, HW), 0)

    pieces = []
    for j in range(n_sub):
        r0 = j * c
        gj, qj, kj = g[r0:r0 + c], q[r0:r0 + c], kk[r0:r0 + c]
        for t in range(c):
            mask = s_idx <= t
            e = jnp.exp(jnp.where(mask, gj[t:t + 1] - gj, 0.0))
            pieces.append(jnp.where(mask, qj[t:t + 1] * kj * e, 0.0))
    att = _dot(jnp.concatenate(pieces, axis=0).astype(BF16), head_ones)
    o_intra = jnp.sum(att.reshape(tb, c, HW) * v.reshape(n_sub, 1, c, HW).repeat(c, axis=1).reshape(tb, c, HW), axis=1)
    q_dec = (q * jnp.exp(g)).astype(BF16)

    upds, decays = [], []
    for j in range(n_sub):
        r0 = j * c
        g_last = g[r0 + c - 1:r0 + c]
        in_sub = (tok >= r0) & (tok < r0 + c)
        kd = jnp.where(in_sub, kk * jnp.exp(jnp.where(in_sub, g_last - g, 0.0)), 0.0)
        upds.append(jnp.where(same_head, _dot(v_t, pad_tokens(kd).astype(BF16)), 0.0))
        decays.append(jnp.exp(g_last))

    o_inter = []
    if seqs == 1:
        st = st_ref[...]
        for j in range(n_sub):
            o_inter.append(_dot(q_dec[j * c:(j + 1) * c], st.astype(BF16), NT_DIMS))
            st = st * decays[j] + upds[j]
        st_ref[...] = st

        @pl.when(i == pl.num_programs(1) - 1)
        def _():
            store_state(0, st)
    else:
        for j in range(n_sub):
            st = load_state(j)
            o_inter.append(_dot(q_dec[j * c:(j + 1) * c], st.astype(BF16), NT_DIMS))
            store_state(j, st * decays[j] + upds[j])

    o = (jnp.concatenate(o_inter, axis=0) if n_sub > 1 else o_inter[0]) + o_intra
    ms = _dot((o * o).astype(BF16), head_ones) * (1.0 / D_VAL)
    o_ref[...] = (o * lax.rsqrt(ms + EPS) * gn_ref[...] * gate).astype(o_ref.dtype)


def hgrn(z_all, row0, nb, t, tb, seqs, s0_t, lb, gn):
    assert (seqs == 1 and t % tb == 0) or tb == seqs * t
    nt = max(t // tb, 1)
    r0 = row0 // tb
    has_state = s0_t is not None
    in_specs = [pl.BlockSpec((tb, 4 * HW), lambda b, i: (r0 + b * nt + i, ZD // (4 * HW)))]
    args = [z_all]
    if has_state:
        in_specs.append(pl.BlockSpec((seqs, D_HEADS, D_VAL, D_KEY), lambda b, i: (b, 0, 0, 0)))
        args.append(s0_t)
    in_specs += [pl.BlockSpec((1, HW), lambda b, i: (0, 0)), pl.BlockSpec((1, HW), lambda b, i: (0, 0))]
    args += [lb.reshape(1, HW), jnp.tile(gn, D_HEADS).reshape(1, HW)]
    return pl.pallas_call(
        functools.partial(_hgrn_kernel, tb=tb, seqs=seqs, has_state=has_state),
        grid=(nb // seqs, nt),
        in_specs=in_specs,
        out_specs=[pl.BlockSpec((tb, HW), lambda b, i: (b * nt + i, 0)),
                   pl.BlockSpec((seqs, D_HEADS, D_VAL, D_KEY), lambda b, i: (b, 0, 0, 0))],
        out_shape=[jax.ShapeDtypeStruct((nb * t, HW), BF16), jax.ShapeDtypeStruct((nb, D_HEADS, D_VAL, D_KEY), F32)],
        scratch_shapes=[pltpu.VMEM((HW, HW), F32)],
        compiler_params=_cparams(("parallel", "arbitrary")),
        name="hgrn",
    )(*args)


def _merge_kernel(oa_ref, obc_ref, od_ref, ga_ref, gb_ref, gc_ref, gd_ref, wa_ref, wb_ref, wc_ref, wd_ref, wo_ref, x_ref, o_ref):
    obc = obc_ref[...]
    merged = jax.nn.sigmoid(ga_ref[...]) * _dot(oa_ref[...], wa_ref[...])
    merged += jax.nn.sigmoid(gb_ref[...]) * _dot(obc[:, :BRANCH_WIDTH], wb_ref[...])
    merged += jax.nn.sigmoid(gc_ref[...]) * _dot(obc[:, BRANCH_WIDTH:], wc_ref[...])
    merged += jax.nn.sigmoid(gd_ref[...]) * _dot(od_ref[...], wd_ref[...])
    o_ref[...] = x_ref[...] + _dot(merged.astype(BF16), wo_ref[...])


def merge(o_a, o_bc, o_d, z_all, wa_p, wb, wc, wd, w_out, x, tm=TOKEN_TILE):
    n = x.shape[0]
    row = lambda w: pl.BlockSpec((tm, w), lambda i: (i, 0))
    full = lambda a: pl.BlockSpec(a.shape, lambda i: (0, 0))
    gate = lambda k: pl.BlockSpec((tm, D_MODEL), lambda i: (i, ZG // D_MODEL + k))
    return pl.pallas_call(
        _merge_kernel,
        grid=(n // tm,),
        in_specs=[row(o_a.shape[1]), row(o_bc.shape[1]), row(o_d.shape[1]), gate(0), gate(1), gate(2), gate(3),
                  full(wa_p), full(wb), full(wc), full(wd), full(w_out), row(D_MODEL)],
        out_specs=row(D_MODEL),
        out_shape=jax.ShapeDtypeStruct((n, D_MODEL), F32),
        compiler_params=_cparams(("parallel",)),
        name="merge",
    )(o_a, o_bc, o_d, z_all, z_all, z_all, z_all, wa_p, wb, wc, wd, w_out, x)


HP = P_HEADS * 2
SLOTS = P_HEADS * P_TOPK
SUBK = P_DKEY // 2
STAGE1_UNROLL = 8


def _take_max(s, index, n):
    m = jnp.max(s, axis=0, keepdims=True)
    idx = jnp.min(jnp.where(s == m, index, n), axis=0, keepdims=True)
    return m, idx, jnp.where(index == idx, -jnp.inf, s)


_CAND_BLOCKS = [(0, P_TOPK)] + [(a, 8) for a in range(1, 8)]
_CAND_ROWS = sum(nb for _, nb in _CAND_BLOCKS) + 8


def _route_kernel(x_ref, g_ref, wq_ref, keys_ref, h_ref, ii_ref, jj_ref, gw_ref, q_s, sv_s, si_s, oi_s, oj_s, og_s, *, tm):
    x = x_ref[...]
    h = (x * lax.rsqrt(jnp.mean(x * x, axis=-1, keepdims=True) + EPS) * g_ref[...]).astype(BF16)
    h_ref[...] = h
    q_s[...] = _dot(h, wq_ref[...]).astype(BF16)

    half = P_NKEYS // 2
    n_lt = tm // LANES
    iota_lo = lax.broadcasted_iota(I32, (half, LANES), 0).astype(F32)
    iota_hi = iota_lo + float(half)

    def stage1(it, carry):
        hp = it // n_lt
        toks = pl.ds(pl.multiple_of((it % n_lt) * LANES, LANES), LANES)
        q = q_s[toks, pl.ds(pl.multiple_of(hp * SUBK, SUBK), SUBK)]
        s = _dot(keys_ref[hp], q, NT_DIMS)
        a, b = s[:half], s[half:]
        first = a >= b
        cur, cur_i = jnp.where(first, a, b), jnp.where(first, iota_lo, iota_hi)
        nxt, nxt_i = jnp.where(first, b, a), jnp.where(first, iota_hi, iota_lo)
        vals, idxs = [], []
        for _ in range(P_TOPK):
            m = jnp.max(cur, axis=0, keepdims=True)
            idx = jnp.min(jnp.where(cur == m, cur_i, float(P_NKEYS)), axis=0, keepdims=True)
            hit = cur_i == idx
            cur, cur_i, nxt = jnp.where(hit, nxt, cur), jnp.where(hit, nxt_i, cur_i), jnp.where(hit, -jnp.inf, nxt)
            vals.append(m)
            idxs.append(idx)
        sv_s[hp, :, toks] = jnp.concatenate(vals, axis=0)
        si_s[hp, :, toks] = jnp.concatenate(idxs, axis=0)
        return carry

    lax.fori_loop(0, HP * n_lt, stage1, 0, unroll=STAGE1_UNROLL)

    r = lax.broadcasted_iota(I32, (_CAND_ROWS, tm), 0)
    mid = r - P_TOPK
    flat = jnp.where(r < P_TOPK, r, jnp.where(r < _CAND_ROWS - 8, (1 + mid // 8) * P_TOPK + mid % 8, (r - (_CAND_ROWS - 16)) * P_TOPK))
    flat = flat.astype(F32)
    iota_t = lax.broadcasted_iota(I32, (P_TOPK, tm), 0).astype(F32)

    def stage2(hd, carry):
        sv1, sv2 = sv_s[2 * hd], sv_s[2 * hd + 1]
        si1, si2 = si_s[2 * hd], si_s[2 * hd + 1]
        cand = jnp.concatenate([sv1[a:a + 1] + sv2[0:nb] for a, nb in _CAND_BLOCKS] + [sv1[8:P_TOPK] + sv2[0:1]], axis=0)
        fv, ei, ej = [], [], []
        for _ in range(P_TOPK):
            m, idx, cand = _take_max(cand, flat, float(P_TOPK * P_TOPK))
            a = jnp.floor(idx * (1.0 / P_TOPK))
            b = idx - a * P_TOPK
            fv.append(m)
            ei.append(jnp.sum(jnp.where(iota_t == a, si1, 0.0), axis=0, keepdims=True))
            ej.append(jnp.sum(jnp.where(iota_t == b, si2, 0.0), axis=0, keepdims=True))
        fv = jnp.concatenate(fv, axis=0)
        e = jnp.exp(fv - fv[0:1])
        rows = pl.ds(pl.multiple_of(hd * P_TOPK, P_TOPK), P_TOPK)
        og_s[rows, :] = e / jnp.sum(e, axis=0, keepdims=True)
        oi_s[rows, :] = jnp.concatenate(ei, axis=0)
        oj_s[rows, :] = jnp.concatenate(ej, axis=0)
        return carry

    lax.fori_loop(0, P_HEADS, stage2, 0, unroll=2)
    ii_ref[...] = oi_s[...].T.astype(I32)
    jj_ref[...] = oj_s[...].T.astype(I32)
    gw_ref[...] = og_s[...].T


def peer_route(x, g, wq, keys, tm=ROUTE_TILE):
    n = x.shape[0]
    row = lambda w: pl.BlockSpec((tm, w), lambda i: (i, 0))
    return pl.pallas_call(
        functools.partial(_route_kernel, tm=tm),
        grid=(n // tm,),
        in_specs=[row(D_MODEL), pl.BlockSpec((1, D_MODEL), lambda i: (0, 0)),
                  pl.BlockSpec(wq.shape, lambda i: (0, 0)), pl.BlockSpec(keys.shape, lambda i: (0, 0, 0))],
        out_specs=[row(D_MODEL), row(SLOTS), row(SLOTS), row(SLOTS)],
        out_shape=[jax.ShapeDtypeStruct((n, D_MODEL), BF16), jax.ShapeDtypeStruct((n, SLOTS), I32),
                   jax.ShapeDtypeStruct((n, SLOTS), I32), jax.ShapeDtypeStruct((n, SLOTS), F32)],
        scratch_shapes=[pltpu.VMEM((tm, HP * SUBK), BF16), pltpu.VMEM((HP, P_TOPK, tm), F32), pltpu.VMEM((HP, P_TOPK, tm), F32),
                        pltpu.VMEM((SLOTS, tm), F32), pltpu.VMEM((SLOTS, tm), F32), pltpu.VMEM((SLOTS, tm), F32)],
        compiler_params=_cparams(("parallel",)),
        name="peer_route",
    )(x, g.reshape(1, D_MODEL), wq, keys)


CHUNK_I = 16
CHUNK_E = CHUNK_I * P_NKEYS
N_CHUNKS = P_NKEYS // CHUNK_I


def _peer_kernel(h_ref, ii_ref, jj_ref, gw_ref, u_ref, v_ref, x_ref, o_ref, w_s, acc_s, *, tm):
    c = pl.program_id(1)

    @pl.when(c == 0)
    def _():
        acc_s[...] = jnp.zeros(acc_s.shape, F32)
        iota = lax.broadcasted_iota(I32, (P_NKEYS, SLOTS), 0)

        def build(g, carry):
            t0 = pl.multiple_of(g * PACK, PACK)
            ws = []
            for t in range(PACK):
                irow = ii_ref[pl.ds(t0 + t, 1), :]
                jrow = jj_ref[pl.ds(t0 + t, 1), :]
                grow = gw_ref[pl.ds(t0 + t, 1), :]
                p_t = jnp.where(iota == irow, grow, 0.0).astype(BF16)
                q_t = jnp.where(iota == jrow, 1.0, 0.0).astype(BF16)
                ws.append(_dot(p_t, q_t, NT_DIMS).astype(BF16))
            w_s[:, pl.ds(t0, PACK), :] = pltpu.einshape("tij->itj", jnp.stack(ws, axis=0))
            return carry

        lax.fori_loop(0, tm // PACK, build, 0, unroll=2)

    a = _dot(h_ref[...], u_ref[...], NT_DIMS)
    act = 0.5 * a * (1.0 + lax.erf(a * (1.0 / math.sqrt(2.0))))
    wd = jnp.concatenate([w_s[c * CHUNK_I + r] for r in range(CHUNK_I)], axis=1)
    acc_s[...] += _dot(act.astype(BF16) * wd, v_ref[...])

    @pl.when(c == N_CHUNKS - 1)
    def _():
        o_ref[...] = x_ref[...] + acc_s[...]


def peer_experts(layer, h2, ii, jj, gw, u_tabs, v_tabs, x, tm=EXPERT_TILE):
    n = x.shape[0]
    row = lambda w: pl.BlockSpec((tm, w), lambda i, c: (i, 0))
    return pl.pallas_call(
        functools.partial(_peer_kernel, tm=tm),
        grid=(n // tm, N_CHUNKS),
        in_specs=[row(D_MODEL), row(SLOTS), row(SLOTS), row(SLOTS),
                  pl.BlockSpec((None, CHUNK_E, D_MODEL), lambda i, c: (layer, c, 0)),
                  pl.BlockSpec((None, CHUNK_E, D_MODEL), lambda i, c: (layer, c, 0)),
                  row(D_MODEL)],
        out_specs=row(D_MODEL),
        out_shape=jax.ShapeDtypeStruct((n, D_MODEL), F32),
        scratch_shapes=[pltpu.VMEM((P_NKEYS, tm, P_NKEYS), BF16), pltpu.VMEM((tm, D_MODEL), F32)],
        compiler_params=_cparams(("parallel", "arbitrary")),
        name="peer_experts",
    )(h2, ii, jj, gw, u_tabs, v_tabs, x)


def _rmsnorm_kernel(x_ref, g_ref, o_ref):
    x = x_ref[...]
    o_ref[...] = x * lax.rsqrt(jnp.mean(x * x, axis=-1, keepdims=True) + EPS) * g_ref[...]


def rmsnorm(x, g, row0, n, tm=TOKEN_TILE):
    d = x.shape[1]
    r0 = row0 // tm
    return pl.pallas_call(
        _rmsnorm_kernel,
        grid=(n // tm,),
        in_specs=[pl.BlockSpec((tm, d), lambda i: (r0 + i, 0)), pl.BlockSpec((1, d), lambda i: (0, 0))],
        out_specs=pl.BlockSpec((tm, d), lambda i: (i, 0)),
        out_shape=jax.ShapeDtypeStruct((n, d), F32),
        compiler_params=_cparams(("parallel",)),
        name="rmsnorm",
    )(x, g.reshape(1, d))


def _rope_tables():
    half = A_ROPE // 2
    inv = ROPE_THETA ** (-jnp.arange(half, dtype=F32) / half)
    pos = jnp.concatenate([jnp.arange(SEQ), PAST_LEN + jnp.arange(DEC_SEQ)])
    posf = pos.astype(F32)[:, None]

    def tables(start):
        zeros = jnp.zeros((LANES - start - A_ROPE,), F32)
        freq = jnp.concatenate([jnp.zeros((start,), F32), inv, inv, zeros])
        keep = jnp.concatenate([jnp.ones((start + A_ROPE,), F32), zeros])
        sign = jnp.concatenate([jnp.zeros((start,), F32), -jnp.ones((half,), F32), jnp.ones((half,), F32), zeros])
        ang = posf * freq[None, :]
        per_token = lambda t: jnp.concatenate([jnp.tile(t[:SEQ], (BATCH, 1)), jnp.tile(t[SEQ:], (DEC_BATCH, 1))], axis=0)
        return per_token(jnp.cos(ang) * keep[None, :]), per_token(jnp.sin(ang) * sign[None, :])

    ccq, ssq = tables(A_NOPE)
    cck, ssk = tables(0)
    return ccq, ssq, cck, ssk


def _swap_halves(w):
    half = w.shape[-1] // 2
    return jnp.concatenate([w[..., half:], w[..., :half]], axis=-1)


def _layer_weights(l, w_in, a_w_uq, a_w_uk, a_w_uv, w_branch, w_out, p_w_q, p_sub_keys):
    w = w_in[l]
    o = 0
    parts = []
    for size in (A_Q_LORA, A_KV_LORA, A_ROPE, 2 * BRANCH_WIDTH, 3 * BRANCH_WIDTH, 4 * HW, N_BRANCH * D_MODEL):
        parts.append(w[:, o:o + size])
        o += size
    cq, ckv, kr, b_in, c_in, d_in, gate = parts
    pad = jnp.zeros((D_MODEL, ZB - A_Q_LORA - A_KV_LORA - 2 * A_ROPE), F32)
    w_in_p = jnp.concatenate([cq, ckv, kr, _swap_halves(kr), pad, b_in, d_in, gate, c_in], axis=1).astype(BF16)

    wq = a_w_uq[l].reshape(A_Q_LORA, A_HEADS, A_NOPE + A_ROPE)
    nope, rope = wq[..., :A_NOPE], wq[..., A_NOPE:]
    tail = jnp.zeros((A_Q_LORA, A_HEADS, HEAD_PAD - A_NOPE - A_ROPE), F32)
    wq_main = jnp.concatenate([nope, rope, tail], axis=-1).reshape(A_Q_LORA, -1).astype(BF16)
    wq_swap = jnp.concatenate([jnp.zeros_like(nope), _swap_halves(rope), tail], axis=-1).reshape(A_Q_LORA, -1).astype(BF16)

    w_uk, w_uv = a_w_uk[l], a_w_uv[l]
    head_tail = jnp.zeros((A_KV_LORA, A_HEADS, HEAD_PAD - A_NOPE), F32)
    wuk_p = jnp.concatenate([w_uk, head_tail], axis=-1).reshape(A_KV_LORA, -1).astype(BF16)
    wuv_p = jnp.concatenate([w_uv, head_tail], axis=-1).reshape(A_KV_LORA, -1).astype(BF16)

    r = jnp.arange(LANES)[:, None]
    col = jnp.arange(A_HEADS * HEAD_PAD)[None, :]
    place = ((col % HEAD_PAD == A_NOPE + r) & (r < A_ROPE)).astype(BF16)

    blk = jnp.zeros((A_HEADS, HEAD_PAD, 2 * LANES), F32)
    blk = blk.at[:, :A_NOPE, :A_KV_LORA].set(jnp.transpose(w_uk, (1, 2, 0)))
    blk = blk.at[:, A_NOPE:A_NOPE + A_ROPE, A_KV_LORA:A_KV_LORA + A_ROPE].set(jnp.eye(A_ROPE, dtype=F32))
    eye_h = jnp.eye(A_HEADS, dtype=F32)
    wabs = (blk[:, :, None, :] * eye_h[:, None, :, None]).reshape(A_HEADS * HEAD_PAD, A_HEADS * 2 * LANES).astype(BF16)

    wb = w_branch[l]
    wa_p = jnp.concatenate([wb[0].reshape(A_HEADS, A_V, D_MODEL), jnp.zeros((A_HEADS, HEAD_PAD - A_V, D_MODEL), F32)],
                           axis=1).reshape(A_HEADS * HEAD_PAD, D_MODEL).astype(BF16)
    return dict(w_in_p=w_in_p, wq_main=wq_main, wq_swap=wq_swap, wuk_p=wuk_p, wuv_p=wuv_p, place=place, wabs=wabs,
                wa_p=wa_p, wb=wb[1].astype(BF16), wc=wb[2].astype(BF16), wd=wb[3].astype(BF16), w_out=w_out[l].astype(BF16),
                wq=p_w_q[l].astype(BF16), keys=p_sub_keys[l].reshape(HP, P_NKEYS, SUBK).astype(BF16))


def _swap_state(s):
    return jnp.transpose(s, (0, 1, 3, 2))


def kernel(x_prompt, x_sample, cache_ckv, cache_krope, page_table, state_conv_b, state_conv_c, state_hgrn, norm1_g, w_in, a_q_norm_g, a_w_uq, a_kv_norm_g, a_w_uk, a_w_uv, b_conv_w, b_conv_b, b_ln_g, b_ln_b, c_conv_w, d_lower_bound, d_gnorm_g, w_branch, w_out, norm2_g, p_w_q, p_sub_keys, p_u, p_v, final_norm_g):
    lb_soft = jax.nn.softmax(d_lower_bound.astype(F32), axis=0)
    lower_bounds = jnp.cumsum(lb_soft, axis=0) - lb_soft[0:1]
    ccq, ssq, cck, ssk = _rope_tables()
    x = jnp.concatenate([x_prompt.reshape(NP, D_MODEL), x_sample.reshape(NS, D_MODEL)], axis=0)
    cache_krope_t = jnp.swapaxes(cache_krope, 2, 3)
    u_tabs, v_tabs = p_u.astype(BF16), p_v.astype(BF16)

    states_p, states_s = [], []
    for l in range(DEPTH):
        w = _layer_weights(l, w_in, a_w_uq, a_w_uk, a_w_uv, w_branch, w_out, p_w_q, p_sub_keys)
        z = norm_matmul(x, norm1_g[l], w["w_in_p"], *PROJ_TILE)

        ckv, kr, qp, kp, vp = mla_prep(z, a_q_norm_g[l], a_kv_norm_g[l], w["wq_main"], w["wq_swap"], w["wuk_p"], w["wuv_p"],
                                       w["place"], ccq, ssq, cck, ssk)
        oa_p = flash_prompt(qp, kp, vp)
        qlat = matmul(qp[NP:], w["wabs"], TOKEN_TILE, BF16)
        qlat = jnp.transpose(qlat.reshape(DEC_BATCH, DEC_SEQ, A_HEADS, 2 * LANES), (0, 2, 1, 3)).reshape(DEC_BATCH, DEC_ROWS, 2 * LANES)
        oa_s = mla_decode(l, page_table, qlat, cache_ckv, cache_krope_t, ckv[NP:].reshape(DEC_BATCH, DEC_SEQ, LANES),
                          kr[NP:].reshape(DEC_BATCH, DEC_SEQ, LANES), w["wuv_p"])
        o_a = jnp.concatenate([oa_p, oa_s.reshape(NS, -1)], axis=0)

        conv_w = (b_conv_w[l], b_conv_b[l], b_ln_g[l], b_ln_b[l], c_conv_w[l])
        obc_p, hb_p, hc_p = conv_branches(z, 0, BATCH, SEQ, TOKEN_TILE, 1, None, None, *conv_w)
        obc_s, hb_s, hc_s = conv_branches(z, NP, DEC_BATCH, DEC_SEQ, DEC_SEQ, SEQS_PER_STEP, state_conv_b[l], state_conv_c[l], *conv_w)
        o_bc = jnp.concatenate([obc_p, obc_s], axis=0)

        od_p, st_p = hgrn(z, 0, BATCH, SEQ, LANES, 1, None, lower_bounds[l], d_gnorm_g[l])
        od_s, st_s = hgrn(z, NP, DEC_BATCH, DEC_SEQ, SEQS_PER_STEP * DEC_SEQ, SEQS_PER_STEP, _swap_state(state_hgrn[l]), lower_bounds[l], d_gnorm_g[l])
        o_d = jnp.concatenate([od_p, od_s], axis=0)

        x1 = merge(o_a, o_bc, o_d, z, w["wa_p"], w["wb"], w["wc"], w["wd"], w["w_out"], x)
        h2, ii, jj, gw = peer_route(x1, norm2_g[l], w["wq"], w["keys"])
        x = peer_experts(l, h2, ii, jj, gw, u_tabs, v_tabs, x1)

        states_p.append((ckv[:NP].reshape(BATCH, SEQ, A_KV_LORA), kr[:NP, :A_ROPE].reshape(BATCH, SEQ, A_ROPE),
                         hb_p, hc_p, _swap_state(st_p)))
        states_s.append((ckv[NP:].reshape(DEC_BATCH, DEC_SEQ, A_KV_LORA), kr[NP:, :A_ROPE].reshape(DEC_BATCH, DEC_SEQ, A_ROPE),
                         hb_s, hc_s, _swap_state(st_s)))

    y_p = rmsnorm(x, final_norm_g, 0, NP)
    y_s = rmsnorm(x, final_norm_g, NP, NS)
    stack = lambda states: [jnp.stack([s[i] for s in states], axis=0) for i in range(5)]
    return (y_p.reshape(BATCH, SEQ, D_MODEL), y_s.reshape(DEC_BATCH, DEC_SEQ, D_MODEL), *stack(states_p), *stack(states_s))
```

```python
import functools
import math

import jax
import jax.numpy as jnp
from jax import lax
from jax.experimental import pallas as pl
from jax.experimental.pallas import tpu as pltpu

F32 = jnp.float32
BF16 = jnp.bfloat16
I32 = jnp.int32

D_MODEL = 1024
BATCH = 2
SEQ = 8192
DEPTH = 2
DEC_BATCH = 128
DEC_SEQ = 8
PAST_LEN = 8192
PAGE_SIZE = 128
N_PAGES = PAST_LEN // PAGE_SIZE
BRANCH_WIDTH = 256
N_BRANCH = 4
A_HEADS = 4
A_NOPE = 64
A_ROPE = 32
A_V = 64
A_Q_LORA = 256
A_KV_LORA = 128
ROPE_THETA = 10000.0
B_KERNEL = 31
C_KERNEL = 3
D_HEADS = 4
D_KEY = 64
D_VAL = 64
P_HEADS = 8
P_NKEYS = 128
P_DKEY = 256
P_TOPK = 16
P_EXPERTS = P_NKEYS * P_NKEYS
EPS = 1e-6
NEG_BIG = -1e30

NP = BATCH * SEQ
NS = DEC_BATCH * DEC_SEQ
NT = NP + NS

LANES = 128
SUBLANES = 8
PACK = 16
HEAD_PAD = 128
ZA, ZB, ZD, ZG, ZC = 0, 512, 1024, 2048, 6144
Z_COLS = 6912
VMEM_LIMIT = 56 * 1024 * 1024

TOKEN_TILE = 512
PROJ_TILE = (1024, 2304)
ROUTE_TILE = 256
EXPERT_TILE = 512
FLASH_TILE = (1024, 1024)

HI = lax.Precision.HIGHEST


def _cparams(sem):
    return pltpu.CompilerParams(dimension_semantics=sem, vmem_limit_bytes=VMEM_LIMIT)


def _dot(a, b, dims=(((1,), (0,)), ((), ())), precision=None):
    return lax.dot_general(a, b, dims, precision=precision, preferred_element_type=F32)


NT_DIMS = (((1,), (1,)), ((), ()))
TN_DIMS = (((0,), (0,)), ((), ()))


def _norm_matmul_kernel(x_ref, g_ref, w_ref, o_ref, h_ref):
    @pl.when(pl.program_id(1) == 0)
    def _():
        x = x_ref[...]
        y = x * lax.rsqrt(jnp.mean(x * x, axis=-1, keepdims=True) + EPS)
        h_ref[...] = (y * g_ref[...]).astype(BF16)

    o_ref[...] = _dot(h_ref[...], w_ref[...])


def norm_matmul(x, g, w, tm, tn):
    n, d = x.shape
    cols = w.shape[1]
    return pl.pallas_call(
        _norm_matmul_kernel,
        grid=(n // tm, cols // tn),
        in_specs=[
            pl.BlockSpec((tm, d), lambda i, j: (i, 0)),
            pl.BlockSpec((1, d), lambda i, j: (0, 0)),
            pl.BlockSpec((d, tn), lambda i, j: (0, j)),
        ],
        out_specs=pl.BlockSpec((tm, tn), lambda i, j: (i, j)),
        out_shape=jax.ShapeDtypeStruct((n, cols), F32),
        scratch_shapes=[pltpu.VMEM((tm, d), BF16)],
        compiler_params=_cparams(("parallel", "arbitrary")),
        name="norm_matmul",
    )(x, g.reshape(1, d), w)


def _matmul_kernel(x_ref, w_ref, o_ref):
    o_ref[...] = _dot(x_ref[...], w_ref[...]).astype(o_ref.dtype)


def matmul(x, w, tm, out_dtype):
    n, d = x.shape
    cols = w.shape[1]
    return pl.pallas_call(
        _matmul_kernel,
        grid=(n // tm,),
        in_specs=[pl.BlockSpec((tm, d), lambda i: (i, 0)), pl.BlockSpec((d, cols), lambda i: (0, 0))],
        out_specs=pl.BlockSpec((tm, cols), lambda i: (i, 0)),
        out_shape=jax.ShapeDtypeStruct((n, cols), out_dtype),
        compiler_params=_cparams(("parallel",)),
        name="matmul",
    )(x, w)


def _tile4(t):
    return jnp.concatenate([t, t, t, t], axis=1)


def _mla_prep_kernel(z_ref, qn_ref, kvn_ref, wqm_ref, wqs_ref, wuk_ref, wuv_ref, place_ref,
                     ccq_ref, ssq_ref, cck_ref, ssk_ref,
                     ckv_ref, kr_ref, q_ref, k_ref, v_ref):
    z = z_ref[...]
    cq = z[:, 0:A_Q_LORA]
    cqn = cq * lax.rsqrt(jnp.mean(cq * cq, axis=-1, keepdims=True) + EPS) * qn_ref[...]
    cqn = cqn.astype(BF16)
    scale = 1.0 / math.sqrt(A_NOPE + A_ROPE)
    q = _dot(cqn, wqm_ref[...]) * _tile4(ccq_ref[...]) + _dot(cqn, wqs_ref[...]) * _tile4(ssq_ref[...])
    q_ref[...] = (q * scale).astype(BF16)

    c = z[:, A_Q_LORA:A_Q_LORA + A_KV_LORA]
    ckv = c * lax.rsqrt(jnp.mean(c * c, axis=-1, keepdims=True) + EPS) * kvn_ref[...]
    ckv_ref[...] = ckv
    ckv_b = ckv.astype(BF16)

    kc = z[:, A_Q_LORA + A_KV_LORA:]
    kr = kc * cck_ref[...] + pltpu.roll(kc, LANES - A_ROPE, axis=1) * ssk_ref[...]
    kr_ref[...] = kr

    k = _dot(ckv_b, wuk_ref[...]) + _dot(kr.astype(BF16), place_ref[...])
    k_ref[...] = k.astype(BF16)
    lane = lax.broadcasted_iota(I32, (1, A_HEADS * HEAD_PAD), 1) % HEAD_PAD
    v_ref[...] = (_dot(ckv_b, wuv_ref[...]) + (lane == A_V).astype(F32)).astype(BF16)


def mla_prep(z_all, qn_g, kvn_g, wq_main, wq_swap, wuk_p, wuv_p, place, ccq, ssq, cck, ssk, tm=TOKEN_TILE):
    n = z_all.shape[0]
    hp = A_HEADS * HEAD_PAD
    row = lambda w: pl.BlockSpec((tm, w), lambda i: (i, 0))
    full = lambda a: pl.BlockSpec(a.shape, lambda i: (0,) * a.ndim)
    qn_g = qn_g.reshape(1, -1)
    kvn_g = kvn_g.reshape(1, -1)
    return pl.pallas_call(
        _mla_prep_kernel,
        grid=(n // tm,),
        in_specs=[row(512), full(qn_g), full(kvn_g), full(wq_main), full(wq_swap), full(wuk_p), full(wuv_p), full(place),
                  row(LANES), row(LANES), row(LANES), row(LANES)],
        out_specs=[row(LANES), row(LANES), row(hp), row(hp), row(hp)],
        out_shape=[jax.ShapeDtypeStruct((n, LANES), F32), jax.ShapeDtypeStruct((n, LANES), F32),
                   jax.ShapeDtypeStruct((n, hp), BF16), jax.ShapeDtypeStruct((n, hp), BF16),
                   jax.ShapeDtypeStruct((n, hp), BF16)],
        compiler_params=_cparams(("parallel",)),
        name="mla_prep",
    )(z_all, qn_g, kvn_g, wq_main, wq_swap, wuk_p, wuv_p, place, ccq, ssq, cck, ssk)


def _flash_kernel(q_ref, k_ref, v_ref, o_ref, *, tq, tk):
    i = pl.program_id(2)
    q = q_ref[...]

    def step(j, carry, masked):
        m, acc = carry
        kj = k_ref[pl.ds(pl.multiple_of(j * tk, tk), tk), :]
        vj = v_ref[pl.ds(pl.multiple_of(j * tk, tk), tk), :]
        s = _dot(q, kj, NT_DIMS)
        if masked:
            qpos = i * tq + lax.broadcasted_iota(I32, (tq, tk), 0)
            kpos = j * tk + lax.broadcasted_iota(I32, (tq, tk), 1)
            s = jnp.where(kpos <= qpos, s, NEG_BIG)
        m_new = jnp.maximum(m, jnp.max(s, axis=-1, keepdims=True))
        p = jnp.exp(s - m_new)
        acc = jnp.exp(m - m_new) * acc + _dot(p.astype(BF16), vj)
        return m_new, acc

    init = (jnp.full((tq, 1), NEG_BIG, F32), jnp.zeros((tq, HEAD_PAD), F32))
    n_full = (i * tq) // tk
    carry = lax.fori_loop(0, n_full // 2, lambda p, c: step(2 * p + 1, step(2 * p, c, False), False), init)
    carry = lax.fori_loop(0, n_full % 2, lambda _, c: step(n_full - 1, c, False), carry)
    for d in range(tq // tk):
        carry = step(n_full + d, carry, True)
    m, acc = carry
    o_ref[...] = (acc / acc[:, A_V:A_V + 1]).astype(o_ref.dtype)


def flash_prompt(qp, kp, vp, tq=FLASH_TILE[0], tk=FLASH_TILE[1]):
    assert tq % tk == 0
    nq = SEQ // tq
    return pl.pallas_call(
        functools.partial(_flash_kernel, tq=tq, tk=tk),
        grid=(BATCH, A_HEADS, nq),
        in_specs=[
            pl.BlockSpec((tq, HEAD_PAD), lambda b, h, i: (b * nq + i, h)),
            pl.BlockSpec((SEQ, HEAD_PAD), lambda b, h, i: (b, h)),
            pl.BlockSpec((SEQ, HEAD_PAD), lambda b, h, i: (b, h)),
        ],
        out_specs=pl.BlockSpec((tq, HEAD_PAD), lambda b, h, i: (b * nq + i, h)),
        out_shape=jax.ShapeDtypeStruct((NP, A_HEADS * HEAD_PAD), BF16),
        compiler_params=_cparams(("parallel", "parallel", "arbitrary")),
        name="flash_prompt",
    )(qp, kp, vp)


DEC_ROWS = A_HEADS * DEC_SEQ
DEC_SLOTS = 2


def _decode_kernel(pt_ref, q_ref, cnew_ref, knew_ref, wuv_ref, ckv_hbm, kr_hbm, o_ref, kbuf, krbuf, sem, *, page0):
    b = pl.program_id(0)
    slot = b % DEC_SLOTS

    def page_copies(seq, sl, r):
        page = page0 + pt_ref[seq, r]
        return (pltpu.make_async_copy(ckv_hbm.at[page], kbuf.at[sl, r], sem.at[0, sl]),
                pltpu.make_async_copy(kr_hbm.at[page], krbuf.at[sl, r], sem.at[1, sl]))

    def start_fetch(seq, sl):
        def body(r, carry):
            for cp in page_copies(seq, sl, r):
                cp.start()
            return carry
        lax.fori_loop(0, N_PAGES, body, 0)

    @pl.when(b == 0)
    def _():
        for ahead in range(DEC_SLOTS - 1):
            start_fetch(ahead, ahead)

    @pl.when(b + DEC_SLOTS - 1 < pl.num_programs(0))
    def _():
        start_fetch(b + DEC_SLOTS - 1, (b + DEC_SLOTS - 1) % DEC_SLOTS)

    def wait_body(r, carry):
        for cp in page_copies(b, slot, r):
            cp.wait()
        return carry

    lax.fori_loop(0, N_PAGES, wait_body, 0)

    q = q_ref[...]
    q_lat = q[:, :A_KV_LORA]
    q_rope = q[:, A_KV_LORA:A_KV_LORA + A_ROPE]
    kc = kbuf[slot].reshape(PAST_LEN, A_KV_LORA).astype(BF16)
    kr_t = jnp.concatenate([krbuf[slot, r] for r in range(N_PAGES)], axis=1).astype(BF16)
    s_past = _dot(q_lat, kc, NT_DIMS) + _dot(q_rope, kr_t)

    cn = cnew_ref[...].astype(BF16)
    kn = knew_ref[...][:, :A_ROPE].astype(BF16)
    s_new = _dot(q_lat, cn, NT_DIMS) + _dot(q_rope, kn, NT_DIMS)
    t_q = lax.broadcasted_iota(I32, s_new.shape, 0) % DEC_SEQ
    t_k = lax.broadcasted_iota(I32, s_new.shape, 1)
    s_new = jnp.where(t_k <= t_q, s_new, NEG_BIG)

    m = jnp.maximum(jnp.max(s_past, axis=-1, keepdims=True), jnp.max(s_new, axis=-1, keepdims=True))
    p_past = jnp.exp(s_past - m)
    p_new = jnp.exp(s_new - m)
    l = jnp.sum(p_past, axis=-1, keepdims=True) + jnp.sum(p_new, axis=-1, keepdims=True)
    acc = _dot(p_past.astype(BF16), kc) + _dot(p_new.astype(BF16), cn)
    o_lat = (acc / l).astype(BF16)
    for h in range(A_HEADS):
        o_ref[:, h * HEAD_PAD:(h + 1) * HEAD_PAD] = _dot(
            o_lat[h * DEC_SEQ:(h + 1) * DEC_SEQ, :], wuv_ref[:, h * HEAD_PAD:(h + 1) * HEAD_PAD]).astype(o_ref.dtype)


def mla_decode(layer, page_table, qlat, cache_ckv, cache_krope_t, ckv_new, kr_new, wuv_p):
    n_pool = cache_ckv.shape[1]
    hbm = pl.BlockSpec(memory_space=pl.ANY)
    grid_spec = pltpu.PrefetchScalarGridSpec(
        num_scalar_prefetch=1,
        grid=(DEC_BATCH,),
        in_specs=[
            pl.BlockSpec((None, DEC_ROWS, 2 * LANES), lambda b, pt: (b, 0, 0)),
            pl.BlockSpec((None, DEC_SEQ, LANES), lambda b, pt: (b, 0, 0)),
            pl.BlockSpec((None, DEC_SEQ, LANES), lambda b, pt: (b, 0, 0)),
            pl.BlockSpec(wuv_p.shape, lambda b, pt: (0, 0)),
            hbm, hbm,
        ],
        out_specs=pl.BlockSpec((None, DEC_SEQ, A_HEADS * HEAD_PAD), lambda b, pt: (b, 0, 0)),
        scratch_shapes=[pltpu.VMEM((DEC_SLOTS, N_PAGES, PAGE_SIZE, A_KV_LORA), F32),
                        pltpu.VMEM((DEC_SLOTS, N_PAGES, A_ROPE, PAGE_SIZE), F32), pltpu.SemaphoreType.DMA((2, DEC_SLOTS))],
    )
    return pl.pallas_call(
        functools.partial(_decode_kernel, page0=layer * n_pool),
        grid_spec=grid_spec,
        out_shape=jax.ShapeDtypeStruct((DEC_BATCH, DEC_SEQ, A_HEADS * HEAD_PAD), BF16),
        compiler_params=_cparams(("arbitrary",)),
        name="mla_decode",
    )(page_table, qlat, ckv_new, kr_new, wuv_p, cache_ckv.reshape(-1, PAGE_SIZE, A_KV_LORA), cache_krope_t.reshape(-1, A_ROPE, PAGE_SIZE))


CTX_B = 32
CTX_C = 8
SEQS_PER_STEP = 8


def _conv_kernel(*refs, tt, nbb, has_hist):
    if has_hist:
        zb_ref, zc_ref, hb_ref, hc_ref = refs[:4]
        refs = refs[4:]
    else:
        zb_ref, zc_ref = refs[:2]
        refs = refs[2:]
    bw_ref, bb_ref, lg_ref, lb_ref, cw_ref, o_ref, nhb_ref, nhc_ref, xb_ref, xc_ref = refs
    i = pl.program_id(1)
    hb_rows, hc_rows = B_KERNEL - 1, C_KERNEL - 1

    @pl.when(i == 0)
    def _():
        xb_ref[:, 0:CTX_B, :] = jnp.zeros((nbb, CTX_B, BRANCH_WIDTH), F32)
        xc_ref[:, 0:CTX_C, :] = jnp.zeros((nbb, CTX_C, BRANCH_WIDTH), F32)
        if has_hist:
            xb_ref[:, CTX_B - hb_rows:CTX_B, :] = hb_ref[...]
            xc_ref[:, CTX_C - hc_rows:CTX_C, :] = hc_ref[...]

    zb = zb_ref[...]
    zc = zc_ref[...]
    glu = zb[:, :BRANCH_WIDTH] * jax.nn.sigmoid(zb[:, BRANCH_WIDTH:])
    u_c = zc[:, BRANCH_WIDTH:2 * BRANCH_WIDTH] * zc[:, 2 * BRANCH_WIDTH:]
    accs_b, accs_c = [], []
    for s in range(nbb):
        xb_ref[s, CTX_B:CTX_B + tt, :] = glu[s * tt:(s + 1) * tt]
        xc_ref[s, CTX_C:CTX_C + tt, :] = u_c[s * tt:(s + 1) * tt]
        x_all = xb_ref[s]
        acc = jnp.zeros((tt, BRANCH_WIDTH), F32)
        for r in range(SUBLANES):
            x_r = x_all if r == 0 else pltpu.roll(x_all, CTX_B + tt - r, axis=0)
            for k in range(B_KERNEL):
                off = CTX_B - hb_rows + k
                if off % SUBLANES == r:
                    acc = acc + x_r[off - r:off - r + tt] * bw_ref[k:k + 1, :]
        accs_b.append(acc)
        acc = jnp.zeros((tt, BRANCH_WIDTH), F32)
        for k in range(C_KERNEL):
            acc = acc + xc_ref[s, pl.ds(CTX_C - hc_rows + k, tt), :] * cw_ref[k:k + 1, :]
        accs_c.append(acc)
    cat = lambda parts: parts[0] if nbb == 1 else jnp.concatenate(parts, axis=0)
    y = cat(accs_b) + bb_ref[...]
    mu = jnp.mean(y, axis=-1, keepdims=True)
    yc = y - mu
    var = jnp.mean(yc * yc, axis=-1, keepdims=True)
    o_b = jax.nn.silu(yc * lax.rsqrt(var + EPS) * lg_ref[...] + lb_ref[...])
    o_ref[...] = jnp.concatenate([o_b, zc[:, :BRANCH_WIDTH] * cat(accs_c)], axis=1).astype(o_ref.dtype)

    new_b = xb_ref[:, tt:tt + CTX_B, :]
    new_c = xc_ref[:, tt:tt + CTX_C, :]
    xb_ref[:, 0:CTX_B, :] = new_b
    xc_ref[:, 0:CTX_C, :] = new_c

    @pl.when(i == pl.num_programs(1) - 1)
    def _():
        nhb_ref[...] = new_b[:, CTX_B - hb_rows:, :]
        nhc_ref[...] = new_c[:, CTX_C - hc_rows:, :]


def conv_branches(z_all, row0, nb, t, tt, nbb, hist_b, hist_c, b_cw, b_cb, ln_g, ln_b, c_cw):
    assert nbb == 1 or tt == t
    nt = t // tt
    rows = nbb * tt
    r0 = row0 // rows
    has_hist = hist_b is not None
    w = BRANCH_WIDTH
    in_specs = [
        pl.BlockSpec((rows, 2 * w), lambda b, i: (r0 + b * nt + i, ZB // (2 * w))),
        pl.BlockSpec((rows, 3 * w), lambda b, i: (r0 + b * nt + i, ZC // (3 * w))),
    ]
    args = [z_all, z_all]
    if has_hist:
        in_specs += [pl.BlockSpec((nbb, B_KERNEL - 1, w), lambda b, i: (b, 0, 0)),
                     pl.BlockSpec((nbb, C_KERNEL - 1, w), lambda b, i: (b, 0, 0))]
        args += [hist_b, hist_c]
    params = [b_cw, b_cb.reshape(1, w), ln_g.reshape(1, w), ln_b.reshape(1, w), c_cw]
    in_specs += [pl.BlockSpec(p.shape, lambda b, i: (0, 0)) for p in params]
    return pl.pallas_call(
        functools.partial(_conv_kernel, tt=tt, nbb=nbb, has_hist=has_hist),
        grid=(nb // nbb, nt),
        in_specs=in_specs,
        out_specs=[pl.BlockSpec((rows, 2 * w), lambda b, i: (b * nt + i, 0)),
                   pl.BlockSpec((nbb, B_KERNEL - 1, w), lambda b, i: (b, 0, 0)),
                   pl.BlockSpec((nbb, C_KERNEL - 1, w), lambda b, i: (b, 0, 0))],
        out_shape=[jax.ShapeDtypeStruct((nb * t, 2 * w), BF16),
                   jax.ShapeDtypeStruct((nb, B_KERNEL - 1, w), F32),
                   jax.ShapeDtypeStruct((nb, C_KERNEL - 1, w), F32)],
        scratch_shapes=[pltpu.VMEM((nbb, CTX_B + tt, w), F32), pltpu.VMEM((nbb, CTX_C + tt, w), F32)],
        compiler_params=_cparams(("parallel", "arbitrary")),
        name="conv_branches",
    )(*args, *params)


HW = D_HEADS * D_KEY
SUB = 16


def _hgrn_kernel(*refs, tb, seqs, has_state):
    if has_state:
        zd_ref, s0_ref = refs[:2]
        refs = refs[2:]
    else:
        zd_ref = refs[0]
        refs = refs[1:]
    lb_ref, gn_ref, o_ref, sfin_ref, st_ref = refs
    i = pl.program_id(1)
    c = min(SUB, tb // seqs)
    n_sub = tb // c
    assert seqs == 1 or (seqs == n_sub and has_state)

    blocks = [slice(h * D_KEY, (h + 1) * D_KEY) for h in range(D_HEADS)]

    def load_state(j):
        for h, blk in enumerate(blocks):
            st_ref[blk, blk] = s0_ref[j, h]
        return st_ref[...]

    def store_state(j, st):
        for h, blk in enumerate(blocks):
            sfin_ref[j, h] = st[blk, blk]

    if seqs == 1:
        @pl.when(i == 0)
        def _():
            st_ref[...] = jnp.zeros(st_ref.shape, F32)
            if has_state:
                load_state(0)
    else:
        st_ref[...] = jnp.zeros(st_ref.shape, F32)

    zd = zd_ref[...]
    lb = lb_ref[...]
    q = jax.nn.silu(zd[:, 0:HW])
    f = lb + (1.0 - lb) * jax.nn.sigmoid(zd[:, HW:2 * HW])
    logf = jnp.log(f)
    kk = 1.0 - f
    v = zd[:, 2 * HW:3 * HW]
    gate = jax.nn.silu(zd[:, 3 * HW:4 * HW])

    row = lax.broadcasted_iota(I32, (tb, tb), 0)
    col = lax.broadcasted_iota(I32, (tb, tb), 1)
    tril_sub = ((row // c == col // c) & (col <= row)).astype(F32)
    g = _dot(tril_sub, logf, precision=HI)

    hrow = lax.broadcasted_iota(I32, (HW, HW), 0) // D_KEY
    hcol = lax.broadcasted_iota(I32, (HW, HW), 1) // D_KEY
    same_head = hrow == hcol
    head_ones = same_head.astype(BF16)

    def pad_tokens(a):
        return a if tb >= LANES else jnp.concatenate([a, jnp.zeros((LANES - tb, HW), a.dtype)], axis=0)

    v_t = pad_tokens(v).T.astype(BF16)
    tok = lax.broadcasted_iota(I32, (tb, HW), 0)
    s_idx = lax.broadcasted_iota(I32, (c, HW), 0)

    pieces = []
    for j in range(n_sub):
        r0 = j * c
        gj, qj, kj = g[r0:r0 + c], q[r0:r0 + c], kk[r0:r0 + c]
        for t in range(c):
            mask = s_idx <= t
            e = jnp.exp(jnp.where(mask, gj[t:t + 1] - gj, 0.0))
            pieces.append(jnp.where(mask, qj[t:t + 1] * kj * e, 0.0))
    att = _dot(jnp.concatenate(pieces, axis=0).astype(BF16), head_ones)
    o_intra = jnp.sum(att.reshape(tb, c, HW) * v.reshape(n_sub, 1, c, HW).repeat(c, axis=1).reshape(tb, c, HW), axis=1)
    q_dec = (q * jnp.exp(g)).astype(BF16)

    upds, decays = [], []
    for j in range(n_sub):
        r0 = j * c
        g_last = g[r0 + c - 1:r0 + c]
        in_sub = (tok >= r0) & (tok < r0 + c)
        kd = jnp.where(in_sub, kk * jnp.exp(jnp.where(in_sub, g_last - g, 0.0)), 0.0)
        upds.append(jnp.where(same_head, _dot(v_t, pad_tokens(kd).astype(BF16)), 0.0))
        decays.append(jnp.exp(g_last))

    o_inter = []
    if seqs == 1:
        st = st_ref[...]
        for j in range(n_sub):
            o_inter.append(_dot(q_dec[j * c:(j + 1) * c], st.astype(BF16), NT_DIMS))
            st = st * decays[j] + upds[j]
        st_ref[...] = st

        @pl.when(i == pl.num_programs(1) - 1)
        def _():
            store_state(0, st)
    else:
        for j in range(n_sub):
            st = load_state(j)
            o_inter.append(_dot(q_dec[j * c:(j + 1) * c], st.astype(BF16), NT_DIMS))
            store_state(j, st * decays[j] + upds[j])

    o = (jnp.concatenate(o_inter, axis=0) if n_sub > 1 else o_inter[0]) + o_intra
    ms = _dot((o * o).astype(BF16), head_ones) * (1.0 / D_VAL)
    o_ref[...] = (o * lax.rsqrt(ms + EPS) * gn_ref[...] * gate).astype(o_ref.dtype)


def hgrn(z_all, row0, nb, t, tb, seqs, s0_t, lb, gn):
    assert (seqs == 1 and t % tb == 0) or tb == seqs * t
    nt = max(t // tb, 1)
    r0 = row0 // tb
    has_state = s0_t is not None
    in_specs = [pl.BlockSpec((tb, 4 * HW), lambda b, i: (r0 + b * nt + i, ZD // (4 * HW)))]
    args = [z_all]
    if has_state:
        in_specs.append(pl.BlockSpec((seqs, D_HEADS, D_VAL, D_KEY), lambda b, i: (b, 0, 0, 0)))
        args.append(s0_t)
    in_specs += [pl.BlockSpec((1, HW), lambda b, i: (0, 0)), pl.BlockSpec((1, HW), lambda b, i: (0, 0))]
    args += [lb.reshape(1, HW), jnp.tile(gn, D_HEADS).reshape(1, HW)]
    return pl.pallas_call(
        functools.partial(_hgrn_kernel, tb=tb, seqs=seqs, has_state=has_state),
        grid=(nb // seqs, nt),
        in_specs=in_specs,
        out_specs=[pl.BlockSpec((tb, HW), lambda b, i: (b * nt + i, 0)),
                   pl.BlockSpec((seqs, D_HEADS, D_VAL, D_KEY), lambda b, i: (b, 0, 0, 0))],
        out_shape=[jax.ShapeDtypeStruct((nb * t, HW), BF16), jax.ShapeDtypeStruct((nb, D_HEADS, D_VAL, D_KEY), F32)],
        scratch_shapes=[pltpu.VMEM((HW, HW), F32)],
        compiler_params=_cparams(("parallel", "arbitrary")),
        name="hgrn",
    )(*args)


def _merge_kernel(oa_ref, obc_ref, od_ref, ga_ref, gb_ref, gc_ref, gd_ref, wa_ref, wb_ref, wc_ref, wd_ref, wo_ref, x_ref, o_ref):
    obc = obc_ref[...]
    merged = jax.nn.sigmoid(ga_ref[...]) * _dot(oa_ref[...], wa_ref[...])
    merged += jax.nn.sigmoid(gb_ref[...]) * _dot(obc[:, :BRANCH_WIDTH], wb_ref[...])
    merged += jax.nn.sigmoid(gc_ref[...]) * _dot(obc[:, BRANCH_WIDTH:], wc_ref[...])
    merged += jax.nn.sigmoid(gd_ref[...]) * _dot(od_ref[...], wd_ref[...])
    o_ref[...] = x_ref[...] + _dot(merged.astype(BF16), wo_ref[...])


def merge(o_a, o_bc, o_d, z_all, wa_p, wb, wc, wd, w_out, x, tm=TOKEN_TILE):
    n = x.shape[0]
    row = lambda w: pl.BlockSpec((tm, w), lambda i: (i, 0))
    full = lambda a: pl.BlockSpec(a.shape, lambda i: (0, 0))
    gate = lambda k: pl.BlockSpec((tm, D_MODEL), lambda i: (i, ZG // D_MODEL + k))
    return pl.pallas_call(
        _merge_kernel,
        grid=(n // tm,),
        in_specs=[row(o_a.shape[1]), row(o_bc.shape[1]), row(o_d.shape[1]), gate(0), gate(1), gate(2), gate(3),
                  full(wa_p), full(wb), full(wc), full(wd), full(w_out), row(D_MODEL)],
        out_specs=row(D_MODEL),
        out_shape=jax.ShapeDtypeStruct((n, D_MODEL), F32),
        compiler_params=_cparams(("parallel",)),
        name="merge",
    )(o_a, o_bc, o_d, z_all, z_all, z_all, z_all, wa_p, wb, wc, wd, w_out, x)


HP = P_HEADS * 2
SLOTS = P_HEADS * P_TOPK
SUBK = P_DKEY // 2


def _take_max(s, index, n):
    m = jnp.max(s, axis=0, keepdims=True)
    idx = jnp.min(jnp.where(s == m, index, n), axis=0, keepdims=True)
    return m, idx, jnp.where(index == idx, -jnp.inf, s)


_CAND_BLOCKS = [(0, P_TOPK)] + [(a, 8) for a in range(1, 8)]
_CAND_ROWS = sum(nb for _, nb in _CAND_BLOCKS) + 8


def _route_kernel(x_ref, g_ref, wq_ref, keys_ref, h_ref, ii_ref, jj_ref, gw_ref, q_s, sv_s, si_s, oi_s, oj_s, og_s, *, tm):
    x = x_ref[...]
    h = (x * lax.rsqrt(jnp.mean(x * x, axis=-1, keepdims=True) + EPS) * g_ref[...]).astype(BF16)
    h_ref[...] = h
    q_s[...] = _dot(h, wq_ref[...]).astype(BF16)

    half = P_NKEYS // 2
    n_lt = tm // LANES
    iota_lo = lax.broadcasted_iota(I32, (half, LANES), 0).astype(F32)
    iota_hi = iota_lo + float(half)

    def stage1(it, carry):
        hp = it // n_lt
        toks = pl.ds(pl.multiple_of((it % n_lt) * LANES, LANES), LANES)
        q = q_s[toks, pl.ds(pl.multiple_of(hp * SUBK, SUBK), SUBK)]
        s = _dot(keys_ref[hp], q, NT_DIMS)
        a, b = s[:half], s[half:]
        first = a >= b
        cur, cur_i = jnp.where(first, a, b), jnp.where(first, iota_lo, iota_hi)
        nxt, nxt_i = jnp.where(first, b, a), jnp.where(first, iota_hi, iota_lo)
        vals, idxs = [], []
        for _ in range(P_TOPK):
            m = jnp.max(cur, axis=0, keepdims=True)
            idx = jnp.min(jnp.where(cur == m, cur_i, float(P_NKEYS)), axis=0, keepdims=True)
            hit = cur_i == idx
            cur, cur_i, nxt = jnp.where(hit, nxt, cur), jnp.where(hit, nxt_i, cur_i), jnp.where(hit, -jnp.inf, nxt)
            vals.append(m)
            idxs.append(idx)
        sv_s[hp, :, toks] = jnp.concatenate(vals, axis=0)
        si_s[hp, :, toks] = jnp.concatenate(idxs, axis=0)
        return carry

    lax.fori_loop(0, HP * n_lt, stage1, 0, unroll=HP * n_lt)

    r = lax.broadcasted_iota(I32, (_CAND_ROWS, tm), 0)
    mid = r - P_TOPK
    flat = jnp.where(r < P_TOPK, r, jnp.where(r < _CAND_ROWS - 8, (1 + mid // 8) * P_TOPK + mid % 8, (r - (_CAND_ROWS - 16)) * P_TOPK))
    flat = flat.astype(F32)
    iota_t = lax.broadcasted_iota(I32, (P_TOPK, tm), 0).astype(F32)

    def stage2(hd, carry):
        sv1, sv2 = sv_s[2 * hd], sv_s[2 * hd + 1]
        si1, si2 = si_s[2 * hd], si_s[2 * hd + 1]
        cand = jnp.concatenate([sv1[a:a + 1] + sv2[0:nb] for a, nb in _CAND_BLOCKS] + [sv1[8:P_TOPK] + sv2[0:1]], axis=0)
        fv, ei, ej = [], [], []
        for _ in range(P_TOPK):
            m, idx, cand = _take_max(cand, flat, float(P_TOPK * P_TOPK))
            a = jnp.floor(idx * (1.0 / P_TOPK))
            b = idx - a * P_TOPK
            fv.append(m)
            ei.append(jnp.sum(jnp.where(iota_t == a, si1, 0.0), axis=0, keepdims=True))
            ej.append(jnp.sum(jnp.where(iota_t == b, si2, 0.0), axis=0, keepdims=True))
        fv = jnp.concatenate(fv, axis=0)
        e = jnp.exp(fv - fv[0:1])
        rows = pl.ds(pl.multiple_of(hd * P_TOPK, P_TOPK), P_TOPK)
        og_s[rows, :] = e / jnp.sum(e, axis=0, keepdims=True)
        oi_s[rows, :] = jnp.concatenate(ei, axis=0)
        oj_s[rows, :] = jnp.concatenate(ej, axis=0)
        return carry

    lax.fori_loop(0, P_HEADS, stage2, 0, unroll=P_HEADS)
    ii_ref[...] = oi_s[...].T.astype(I32)
    jj_ref[...] = oj_s[...].T.astype(I32)
    gw_ref[...] = og_s[...].T


def peer_route(x, g, wq, keys, tm=ROUTE_TILE):
    n = x.shape[0]
    row = lambda w: pl.BlockSpec((tm, w), lambda i: (i, 0))
    return pl.pallas_call(
        functools.partial(_route_kernel, tm=tm),
        grid=(n // tm,),
        in_specs=[row(D_MODEL), pl.BlockSpec((1, D_MODEL), lambda i: (0, 0)),
                  pl.BlockSpec(wq.shape, lambda i: (0, 0)), pl.BlockSpec(keys.shape, lambda i: (0, 0, 0))],
        out_specs=[row(D_MODEL), row(SLOTS), row(SLOTS), row(SLOTS)],
        out_shape=[jax.ShapeDtypeStruct((n, D_MODEL), BF16), jax.ShapeDtypeStruct((n, SLOTS), I32),
                   jax.ShapeDtypeStruct((n, SLOTS), I32), jax.ShapeDtypeStruct((n, SLOTS), F32)],
        scratch_shapes=[pltpu.VMEM((tm, HP * SUBK), BF16), pltpu.VMEM((HP, P_TOPK, tm), F32), pltpu.VMEM((HP, P_TOPK, tm), F32),
                        pltpu.VMEM((SLOTS, tm), F32), pltpu.VMEM((SLOTS, tm), F32), pltpu.VMEM((SLOTS, tm), F32)],
        compiler_params=_cparams(("parallel",)),
        name="peer_route",
    )(x, g.reshape(1, D_MODEL), wq, keys)


CHUNK_I = 16
CHUNK_E = CHUNK_I * P_NKEYS
N_CHUNKS = P_NKEYS // CHUNK_I


def _peer_kernel(h_ref, ii_ref, jj_ref, gw_ref, u_ref, v_ref, x_ref, o_ref, w_s, acc_s, *, tm):
    c = pl.program_id(1)

    @pl.when(c == 0)
    def _():
        acc_s[...] = jnp.zeros(acc_s.shape, F32)
        iota = lax.broadcasted_iota(I32, (P_NKEYS, SLOTS), 0)

        def build(g, carry):
            t0 = pl.multiple_of(g * PACK, PACK)
            ws = []
            for t in range(PACK):
                irow = ii_ref[pl.ds(t0 + t, 1), :]
                jrow = jj_ref[pl.ds(t0 + t, 1), :]
                grow = gw_ref[pl.ds(t0 + t, 1), :]
                p_t = jnp.where(iota == irow, grow, 0.0).astype(BF16)
                q_t = jnp.where(iota == jrow, 1.0, 0.0).astype(BF16)
                ws.append(_dot(p_t, q_t, NT_DIMS).astype(BF16))
            w_s[:, pl.ds(t0, PACK), :] = pltpu.einshape("tij->itj", jnp.stack(ws, axis=0))
            return carry

        lax.fori_loop(0, tm // PACK, build, 0, unroll=4)

    a = _dot(h_ref[...], u_ref[...], NT_DIMS)
    act = 0.5 * a * (1.0 + lax.erf(a * (1.0 / math.sqrt(2.0))))
    wd = jnp.concatenate([w_s[c * CHUNK_I + r] for r in range(CHUNK_I)], axis=1)
    acc_s[...] += _dot(act.astype(BF16) * wd, v_ref[...])

    @pl.when(c == N_CHUNKS - 1)
    def _():
        o_ref[...] = x_ref[...] + acc_s[...]


def peer_experts(layer, h2, ii, jj, gw, u_tabs, v_tabs, x, tm=EXPERT_TILE):
    n = x.shape[0]
    row = lambda w: pl.BlockSpec((tm, w), lambda i, c: (i, 0))
    return pl.pallas_call(
        functools.partial(_peer_kernel, tm=tm),
        grid=(n // tm, N_CHUNKS),
        in_specs=[row(D_MODEL), row(SLOTS), row(SLOTS), row(SLOTS),
                  pl.BlockSpec((None, CHUNK_E, D_MODEL), lambda i, c: (layer, c, 0)),
                  pl.BlockSpec((None, CHUNK_E, D_MODEL), lambda i, c: (layer, c, 0)),
                  row(D_MODEL)],
        out_specs=row(D_MODEL),
        out_shape=jax.ShapeDtypeStruct((n, D_MODEL), F32),
        scratch_shapes=[pltpu.VMEM((P_NKEYS, tm, P_NKEYS), BF16), pltpu.VMEM((tm, D_MODEL), F32)],
        compiler_params=_cparams(("parallel", "arbitrary")),
        name="peer_experts",
    )(h2, ii, jj, gw, u_tabs, v_tabs, x)


def _rmsnorm_kernel(x_ref, g_ref, o_ref):
    x = x_ref[...]
    o_ref[...] = x * lax.rsqrt(jnp.mean(x * x, axis=-1, keepdims=True) + EPS) * g_ref[...]


def rmsnorm(x, g, row0, n, tm=TOKEN_TILE):
    d = x.shape[1]
    r0 = row0 // tm
    return pl.pallas_call(
        _rmsnorm_kernel,
        grid=(n // tm,),
        in_specs=[pl.BlockSpec((tm, d), lambda i: (r0 + i, 0)), pl.BlockSpec((1, d), lambda i: (0, 0))],
        out_specs=pl.BlockSpec((tm, d), lambda i: (i, 0)),
        out_shape=jax.ShapeDtypeStruct((n, d), F32),
        compiler_params=_cparams(("parallel",)),
        name="rmsnorm",
    )(x, g.reshape(1, d))


def _rope_tables():
    half = A_ROPE // 2
    inv = ROPE_THETA ** (-jnp.arange(half, dtype=F32) / half)
    pos = jnp.concatenate([jnp.arange(SEQ), PAST_LEN + jnp.arange(DEC_SEQ)])
    posf = pos.astype(F32)[:, None]

    def tables(start):
        zeros = jnp.zeros((LANES - start - A_ROPE,), F32)
        freq = jnp.concatenate([jnp.zeros((start,), F32), inv, inv, zeros])
        keep = jnp.concatenate([jnp.ones((start + A_ROPE,), F32), zeros])
        sign = jnp.concatenate([jnp.zeros((start,), F32), -jnp.ones((half,), F32), jnp.ones((half,), F32), zeros])
        ang = posf * freq[None, :]
        per_token = lambda t: jnp.concatenate([jnp.tile(t[:SEQ], (BATCH, 1)), jnp.tile(t[SEQ:], (DEC_BATCH, 1))], axis=0)
        return per_token(jnp.cos(ang) * keep[None, :]), per_token(jnp.sin(ang) * sign[None, :])

    ccq, ssq = tables(A_NOPE)
    cck, ssk = tables(0)
    return ccq, ssq, cck, ssk


def _swap_halves(w):
    half = w.shape[-1] // 2
    return jnp.concatenate([w[..., half:], w[..., :half]], axis=-1)


def _layer_weights(l, w_in, a_w_uq, a_w_uk, a_w_uv, w_branch, w_out, p_w_q, p_sub_keys):
    w = w_in[l]
    o = 0
    parts = []
    for size in (A_Q_LORA, A_KV_LORA, A_ROPE, 2 * BRANCH_WIDTH, 3 * BRANCH_WIDTH, 4 * HW, N_BRANCH * D_MODEL):
        parts.append(w[:, o:o + size])
        o += size
    cq, ckv, kr, b_in, c_in, d_in, gate = parts
    pad = jnp.zeros((D_MODEL, ZB - A_Q_LORA - A_KV_LORA - 2 * A_ROPE), F32)
    w_in_p = jnp.concatenate([cq, ckv, kr, _swap_halves(kr), pad, b_in, d_in, gate, c_in], axis=1).astype(BF16)

    wq = a_w_uq[l].reshape(A_Q_LORA, A_HEADS, A_NOPE + A_ROPE)
    nope, rope = wq[..., :A_NOPE], wq[..., A_NOPE:]
    tail = jnp.zeros((A_Q_LORA, A_HEADS, HEAD_PAD - A_NOPE - A_ROPE), F32)
    wq_main = jnp.concatenate([nope, rope, tail], axis=-1).reshape(A_Q_LORA, -1).astype(BF16)
    wq_swap = jnp.concatenate([jnp.zeros_like(nope), _swap_halves(rope), tail], axis=-1).reshape(A_Q_LORA, -1).astype(BF16)

    w_uk, w_uv = a_w_uk[l], a_w_uv[l]
    head_tail = jnp.zeros((A_KV_LORA, A_HEADS, HEAD_PAD - A_NOPE), F32)
    wuk_p = jnp.concatenate([w_uk, head_tail], axis=-1).reshape(A_KV_LORA, -1).astype(BF16)
    wuv_p = jnp.concatenate([w_uv, head_tail], axis=-1).reshape(A_KV_LORA, -1).astype(BF16)

    r = jnp.arange(LANES)[:, None]
    col = jnp.arange(A_HEADS * HEAD_PAD)[None, :]
    place = ((col % HEAD_PAD == A_NOPE + r) & (r < A_ROPE)).astype(BF16)

    blk = jnp.zeros((A_HEADS, HEAD_PAD, 2 * LANES), F32)
    blk = blk.at[:, :A_NOPE, :A_KV_LORA].set(jnp.transpose(w_uk, (1, 2, 0)))
    blk = blk.at[:, A_NOPE:A_NOPE + A_ROPE, A_KV_LORA:A_KV_LORA + A_ROPE].set(jnp.eye(A_ROPE, dtype=F32))
    eye_h = jnp.eye(A_HEADS, dtype=F32)
    wabs = (blk[:, :, None, :] * eye_h[:, None, :, None]).reshape(A_HEADS * HEAD_PAD, A_HEADS * 2 * LANES).astype(BF16)

    wb = w_branch[l]
    wa_p = jnp.concatenate([wb[0].reshape(A_HEADS, A_V, D_MODEL), jnp.zeros((A_HEADS, HEAD_PAD - A_V, D_MODEL), F32)],
                           axis=1).reshape(A_HEADS * HEAD_PAD, D_MODEL).astype(BF16)
    return dict(w_in_p=w_in_p, wq_main=wq_main, wq_swap=wq_swap, wuk_p=wuk_p, wuv_p=wuv_p, place=place, wabs=wabs,
                wa_p=wa_p, wb=wb[1].astype(BF16), wc=wb[2].astype(BF16), wd=wb[3].astype(BF16), w_out=w_out[l].astype(BF16),
                wq=p_w_q[l].astype(BF16), keys=p_sub_keys[l].reshape(HP, P_NKEYS, SUBK).astype(BF16))


def _swap_state(s):
    return jnp.transpose(s, (0, 1, 3, 2))


def kernel(x_prompt, x_sample, cache_ckv, cache_krope, page_table, state_conv_b, state_conv_c, state_hgrn, norm1_g, w_in, a_q_norm_g, a_w_uq, a_kv_norm_g, a_w_uk, a_w_uv, b_conv_w, b_conv_b, b_ln_g, b_ln_b, c_conv_w, d_lower_bound, d_gnorm_g, w_branch, w_out, norm2_g, p_w_q, p_sub_keys, p_u, p_v, final_norm_g):
    lb_soft = jax.nn.softmax(d_lower_bound.astype(F32), axis=0)
    lower_bounds = jnp.cumsum(lb_soft, axis=0) - lb_soft[0:1]
    ccq, ssq, cck, ssk = _rope_tables()
    x = jnp.concatenate([x_prompt.reshape(NP, D_MODEL), x_sample.reshape(NS, D_MODEL)], axis=0)
    cache_krope_t = jnp.swapaxes(cache_krope, 2, 3)
    u_tabs, v_tabs = p_u.astype(BF16), p_v.astype(BF16)

    states_p, states_s = [], []
    for l in range(DEPTH):
        w = _layer_weights(l, w_in, a_w_uq, a_w_uk, a_w_uv, w_branch, w_out, p_w_q, p_sub_keys)
        z = norm_matmul(x, norm1_g[l], w["w_in_p"], *PROJ_TILE)

        ckv, kr, qp, kp, vp = mla_prep(z, a_q_norm_g[l], a_kv_norm_g[l], w["wq_main"], w["wq_swap"], w["wuk_p"], w["wuv_p"],
                                       w["place"], ccq, ssq, cck, ssk)
        oa_p = flash_prompt(qp, kp, vp)
        qlat = matmul(qp[NP:], w["wabs"], TOKEN_TILE, BF16)
        qlat = jnp.transpose(qlat.reshape(DEC_BATCH, DEC_SEQ, A_HEADS, 2 * LANES), (0, 2, 1, 3)).reshape(DEC_BATCH, DEC_ROWS, 2 * LANES)
        oa_s = mla_decode(l, page_table, qlat, cache_ckv, cache_krope_t, ckv[NP:].reshape(DEC_BATCH, DEC_SEQ, LANES),
                          kr[NP:].reshape(DEC_BATCH, DEC_SEQ, LANES), w["wuv_p"])
        o_a = jnp.concatenate([oa_p, oa_s.reshape(NS, -1)], axis=0)

        conv_w = (b_conv_w[l], b_conv_b[l], b_ln_g[l], b_ln_b[l], c_conv_w[l])
        obc_p, hb_p, hc_p = conv_branches(z, 0, BATCH, SEQ, TOKEN_TILE, 1, None, None, *conv_w)
        obc_s, hb_s, hc_s = conv_branches(z, NP, DEC_BATCH, DEC_SEQ, DEC_SEQ, SEQS_PER_STEP, state_conv_b[l], state_conv_c[l], *conv_w)
        o_bc = jnp.concatenate([obc_p, obc_s], axis=0)

        od_p, st_p = hgrn(z, 0, BATCH, SEQ, LANES, 1, None, lower_bounds[l], d_gnorm_g[l])
        od_s, st_s = hgrn(z, NP, DEC_BATCH, DEC_SEQ, SEQS_PER_STEP * DEC_SEQ, SEQS_PER_STEP, _swap_state(state_hgrn[l]), lower_bounds[l], d_gnorm_g[l])
        o_d = jnp.concatenate([od_p, od_s], axis=0)

        x1 = merge(o_a, o_bc, o_d, z, w["wa_p"], w["wb"], w["wc"], w["wd"], w["w_out"], x)
        h2, ii, jj, gw = peer_route(x1, norm2_g[l], w["wq"], w["keys"])
        x = peer_experts(l, h2, ii, jj, gw, u_tabs, v_tabs, x1)

        states_p.append((ckv[:NP].reshape(BATCH, SEQ, A_KV_LORA), kr[:NP, :A_ROPE].reshape(BATCH, SEQ, A_ROPE),
                         hb_p, hc_p, _swap_state(st_p)))
        states_s.append((ckv[NP:].reshape(DEC_BATCH, DEC_SEQ, A_KV_LORA), kr[NP:, :A_ROPE].reshape(DEC_BATCH, DEC_SEQ, A_ROPE),
                         hb_s, hc_s, _swap_state(st_s)))

    y_p = rmsnorm(x, final_norm_g, 0, NP)
    y_s = rmsnorm(x, final_norm_g, NP, NS)
    stack = lambda states: [jnp.stack([s[i] for s in states], axis=0) for i in range(5)]
    return (y_p.reshape(BATCH, SEQ, D_MODEL), y_s.reshape(DEC_BATCH, DEC_SEQ, D_MODEL), *stack(states_p), *stack(states_s))
```

```python
import functools
import math

import jax
import jax.numpy as jnp
from jax import lax
from jax.experimental import pallas as pl
from jax.experimental.pallas import tpu as pltpu

F32 = jnp.float32
BF16 = jnp.bfloat16
I32 = jnp.int32

D_MODEL = 1024
BATCH = 2
SEQ = 8192
DEPTH = 2
DEC_BATCH = 128
DEC_SEQ = 8
PAST_LEN = 8192
PAGE_SIZE = 128
N_PAGES = PAST_LEN // PAGE_SIZE
BRANCH_WIDTH = 256
N_BRANCH = 4
A_HEADS = 4
A_NOPE = 64
A_ROPE = 32
A_V = 64
A_Q_LORA = 256
A_KV_LORA = 128
ROPE_THETA = 10000.0
B_KERNEL = 31
C_KERNEL = 3
D_HEADS = 4
D_KEY = 64
D_VAL = 64
P_HEADS = 8
P_NKEYS = 128
P_DKEY = 256
P_TOPK = 16
P_EXPERTS = P_NKEYS * P_NKEYS
EPS = 1e-6
NEG_BIG = -1e30

NP = BATCH * SEQ
NS = DEC_BATCH * DEC_SEQ
NT = NP + NS

LANES = 128
SUBLANES = 8
PACK = 16
HEAD_PAD = 128
ZA, ZB, ZD, ZG, ZC = 0, 512, 1024, 2048, 6144
Z_COLS = 6912
VMEM_LIMIT = 56 * 1024 * 1024

TOKEN_TILE = 512
PROJ_TILE = (1024, 2304)
ROUTE_TILE = 256
EXPERT_TILE = 512
FLASH_TILE = (1024, 1024)

HI = lax.Precision.HIGHEST


def _cparams(sem):
    return pltpu.CompilerParams(dimension_semantics=sem, vmem_limit_bytes=VMEM_LIMIT)


def _dot(a, b, dims=(((1,), (0,)), ((), ())), precision=None):
    return lax.dot_general(a, b, dims, precision=precision, preferred_element_type=F32)


NT_DIMS = (((1,), (1,)), ((), ()))


def _norm_matmul_kernel(x_ref, g_ref, w_ref, o_ref, h_ref):
    @pl.when(pl.program_id(1) == 0)
    def _():
        x = x_ref[...]
        y = x * lax.rsqrt(jnp.mean(x * x, axis=-1, keepdims=True) + EPS)
        h_ref[...] = (y * g_ref[...]).astype(BF16)

    o_ref[...] = _dot(h_ref[...], w_ref[...])


def norm_matmul(x, g, w, tm, tn):
    n, d = x.shape
    cols = w.shape[1]
    return pl.pallas_call(
        _norm_matmul_kernel,
        grid=(n // tm, cols // tn),
        in_specs=[
            pl.BlockSpec((tm, d), lambda i, j: (i, 0)),
            pl.BlockSpec((1, d), lambda i, j: (0, 0)),
            pl.BlockSpec((d, tn), lambda i, j: (0, j)),
        ],
        out_specs=pl.BlockSpec((tm, tn), lambda i, j: (i, j)),
        out_shape=jax.ShapeDtypeStruct((n, cols), F32),
        scratch_shapes=[pltpu.VMEM((tm, d), BF16)],
        compiler_params=_cparams(("parallel", "arbitrary")),
        name="norm_matmul",
    )(x, g.reshape(1, d), w)


def _matmul_kernel(x_ref, w_ref, o_ref):
    o_ref[...] = _dot(x_ref[...], w_ref[...]).astype(o_ref.dtype)


def matmul(x, w, tm, out_dtype):
    n, d = x.shape
    cols = w.shape[1]
    return pl.pallas_call(
        _matmul_kernel,
        grid=(n // tm,),
        in_specs=[pl.BlockSpec((tm, d), lambda i: (i, 0)), pl.BlockSpec((d, cols), lambda i: (0, 0))],
        out_specs=pl.BlockSpec((tm, cols), lambda i: (i, 0)),
        out_shape=jax.ShapeDtypeStruct((n, cols), out_dtype),
        compiler_params=_cparams(("parallel",)),
        name="matmul",
    )(x, w)


def _tile4(t):
    return jnp.concatenate([t, t, t, t], axis=1)


def _mla_prep_kernel(z_ref, qn_ref, kvn_ref, wqm_ref, wqs_ref, wuk_ref, wuv_ref, place_ref,
                     ccq_ref, ssq_ref, cck_ref, ssk_ref,
                     ckv_ref, kr_ref, q_ref, k_ref, v_ref):
    z = z_ref[...]
    cq = z[:, 0:A_Q_LORA]
    cqn = cq * lax.rsqrt(jnp.mean(cq * cq, axis=-1, keepdims=True) + EPS) * qn_ref[...]
    cqn = cqn.astype(BF16)
    scale = 1.0 / math.sqrt(A_NOPE + A_ROPE)
    q = _dot(cqn, wqm_ref[...]) * _tile4(ccq_ref[...]) + _dot(cqn, wqs_ref[...]) * _tile4(ssq_ref[...])
    q_ref[...] = (q * scale).astype(BF16)

    c = z[:, A_Q_LORA:A_Q_LORA + A_KV_LORA]
    ckv = c * lax.rsqrt(jnp.mean(c * c, axis=-1, keepdims=True) + EPS) * kvn_ref[...]
    ckv_ref[...] = ckv
    ckv_b = ckv.astype(BF16)

    kc = z[:, A_Q_LORA + A_KV_LORA:]
    kr = kc * cck_ref[...] + pltpu.roll(kc, LANES - A_ROPE, axis=1) * ssk_ref[...]
    kr_ref[...] = kr

    k = _dot(ckv_b, wuk_ref[...]) + _dot(kr.astype(BF16), place_ref[...])
    k_ref[...] = k.astype(BF16)
    lane = lax.broadcasted_iota(I32, (1, A_HEADS * HEAD_PAD), 1) % HEAD_PAD
    v_ref[...] = (_dot(ckv_b, wuv_ref[...]) + (lane == A_V).astype(F32)).astype(BF16)


def mla_prep(z_all, qn_g, kvn_g, wq_main, wq_swap, wuk_p, wuv_p, place, ccq, ssq, cck, ssk, tm=TOKEN_TILE):
    n = z_all.shape[0]
    hp = A_HEADS * HEAD_PAD
    row = lambda w: pl.BlockSpec((tm, w), lambda i: (i, 0))
    full = lambda a: pl.BlockSpec(a.shape, lambda i: (0,) * a.ndim)
    qn_g = qn_g.reshape(1, -1)
    kvn_g = kvn_g.reshape(1, -1)
    return pl.pallas_call(
        _mla_prep_kernel,
        grid=(n // tm,),
        in_specs=[row(512), full(qn_g), full(kvn_g), full(wq_main), full(wq_swap), full(wuk_p), full(wuv_p), full(place),
                  row(LANES), row(LANES), row(LANES), row(LANES)],
        out_specs=[row(LANES), row(LANES), row(hp), row(hp), row(hp)],
        out_shape=[jax.ShapeDtypeStruct((n, LANES), F32), jax.ShapeDtypeStruct((n, LANES), F32),
                   jax.ShapeDtypeStruct((n, hp), BF16), jax.ShapeDtypeStruct((n, hp), BF16),
                   jax.ShapeDtypeStruct((n, hp), BF16)],
        compiler_params=_cparams(("parallel",)),
        name="mla_prep",
    )(z_all, qn_g, kvn_g, wq_main, wq_swap, wuk_p, wuv_p, place, ccq, ssq, cck, ssk)


def _flash_kernel(q_ref, k_ref, v_ref, o_ref, *, tq, tk):
    i = pl.program_id(2)
    q = q_ref[...]

    def step(j, carry, masked):
        m, acc = carry
        kj = k_ref[pl.ds(pl.multiple_of(j * tk, tk), tk), :]
        vj = v_ref[pl.ds(pl.multiple_of(j * tk, tk), tk), :]
        s = _dot(q, kj, NT_DIMS)
        if masked:
            qpos = i * tq + lax.broadcasted_iota(I32, (tq, tk), 0)
            kpos = j * tk + lax.broadcasted_iota(I32, (tq, tk), 1)
            s = jnp.where(kpos <= qpos, s, NEG_BIG)
        m_new = jnp.maximum(m, jnp.max(s, axis=-1, keepdims=True))
        p = jnp.exp(s - m_new)
        acc = jnp.exp(m - m_new) * acc + _dot(p.astype(BF16), vj)
        return m_new, acc

    init = (jnp.full((tq, 1), NEG_BIG, F32), jnp.zeros((tq, HEAD_PAD), F32))
    n_full = (i * tq) // tk
    carry = lax.fori_loop(0, n_full // 2, lambda p, c: step(2 * p + 1, step(2 * p, c, False), False), init)
    carry = lax.fori_loop(0, n_full % 2, lambda _, c: step(n_full - 1, c, False), carry)
    for d in range(tq // tk):
        carry = step(n_full + d, carry, True)
    m, acc = carry
    o_ref[...] = (acc / acc[:, A_V:A_V + 1]).astype(o_ref.dtype)


def flash_prompt(qp, kp, vp, tq=FLASH_TILE[0], tk=FLASH_TILE[1]):
    assert tq % tk == 0
    nq = SEQ // tq
    return pl.pallas_call(
        functools.partial(_flash_kernel, tq=tq, tk=tk),
        grid=(BATCH, A_HEADS, nq),
        in_specs=[
            pl.BlockSpec((tq, HEAD_PAD), lambda b, h, i: (b * nq + i, h)),
            pl.BlockSpec((SEQ, HEAD_PAD), lambda b, h, i: (b, h)),
            pl.BlockSpec((SEQ, HEAD_PAD), lambda b, h, i: (b, h)),
        ],
        out_specs=pl.BlockSpec((tq, HEAD_PAD), lambda b, h, i: (b * nq + i, h)),
        out_shape=jax.ShapeDtypeStruct((NP, A_HEADS * HEAD_PAD), BF16),
        compiler_params=_cparams(("parallel", "parallel", "arbitrary")),
        name="flash_prompt",
    )(qp, kp, vp)


DEC_ROWS = A_HEADS * DEC_SEQ
DEC_SLOTS = 2


def _decode_kernel(pt_ref, q_ref, cnew_ref, knew_ref, wuv_ref, ckv_hbm, kr_hbm, o_ref, kbuf, krbuf, sem, *, page0):
    b = pl.program_id(0)
    slot = b % DEC_SLOTS

    def page_copies(seq, sl, r):
        page = page0 + pt_ref[seq, r]
        return (pltpu.make_async_copy(ckv_hbm.at[page], kbuf.at[sl, r], sem.at[0, sl]),
                pltpu.make_async_copy(kr_hbm.at[page], krbuf.at[sl, r], sem.at[1, sl]))

    def start_fetch(seq, sl):
        def body(r, carry):
            for cp in page_copies(seq, sl, r):
                cp.start()
            return carry
        lax.fori_loop(0, N_PAGES, body, 0)

    @pl.when(b == 0)
    def _():
        for ahead in range(DEC_SLOTS - 1):
            start_fetch(ahead, ahead)

    @pl.when(b + DEC_SLOTS - 1 < pl.num_programs(0))
    def _():
        start_fetch(b + DEC_SLOTS - 1, (b + DEC_SLOTS - 1) % DEC_SLOTS)

    def wait_body(r, carry):
        for cp in page_copies(b, slot, r):
            cp.wait()
        return carry

    lax.fori_loop(0, N_PAGES, wait_body, 0)

    q = q_ref[...]
    q_lat = q[:, :A_KV_LORA]
    q_rope = q[:, A_KV_LORA:A_KV_LORA + A_ROPE]
    kc = kbuf[slot].reshape(PAST_LEN, A_KV_LORA).astype(BF16)
    kr_t = jnp.concatenate([krbuf[slot, r] for r in range(N_PAGES)], axis=1).astype(BF16)
    s_past = _dot(q_lat, kc, NT_DIMS) + _dot(q_rope, kr_t)

    cn = cnew_ref[...].astype(BF16)
    kn = knew_ref[...][:, :A_ROPE].astype(BF16)
    s_new = _dot(q_lat, cn, NT_DIMS) + _dot(q_rope, kn, NT_DIMS)
    t_q = lax.broadcasted_iota(I32, s_new.shape, 0) % DEC_SEQ
    t_k = lax.broadcasted_iota(I32, s_new.shape, 1)
    s_new = jnp.where(t_k <= t_q, s_new, NEG_BIG)

    m = jnp.maximum(jnp.max(s_past, axis=-1, keepdims=True), jnp.max(s_new, axis=-1, keepdims=True))
    p_past = jnp.exp(s_past - m)
    p_new = jnp.exp(s_new - m)
    l = jnp.sum(p_past, axis=-1, keepdims=True) + jnp.sum(p_new, axis=-1, keepdims=True)
    acc = _dot(p_past.astype(BF16), kc) + _dot(p_new.astype(BF16), cn)
    o_lat = (acc / l).astype(BF16)
    for h in range(A_HEADS):
        o_ref[:, h * HEAD_PAD:(h + 1) * HEAD_PAD] = _dot(
            o_lat[h * DEC_SEQ:(h + 1) * DEC_SEQ, :], wuv_ref[:, h * HEAD_PAD:(h + 1) * HEAD_PAD]).astype(o_ref.dtype)


def mla_decode(layer, page_table, qlat, cache_ckv, cache_krope_t, ckv_new, kr_new, wuv_p):
    n_pool = cache_ckv.shape[1]
    hbm = pl.BlockSpec(memory_space=pl.ANY)
    grid_spec = pltpu.PrefetchScalarGridSpec(
        num_scalar_prefetch=1,
        grid=(DEC_BATCH,),
        in_specs=[
            pl.BlockSpec((None, DEC_ROWS, 2 * LANES), lambda b, pt: (b, 0, 0)),
            pl.BlockSpec((None, DEC_SEQ, LANES), lambda b, pt: (b, 0, 0)),
            pl.BlockSpec((None, DEC_SEQ, LANES), lambda b, pt: (b, 0, 0)),
            pl.BlockSpec(wuv_p.shape, lambda b, pt: (0, 0)),
            hbm, hbm,
        ],
        out_specs=pl.BlockSpec((None, DEC_SEQ, A_HEADS * HEAD_PAD), lambda b, pt: (b, 0, 0)),
        scratch_shapes=[pltpu.VMEM((DEC_SLOTS, N_PAGES, PAGE_SIZE, A_KV_LORA), F32),
                        pltpu.VMEM((DEC_SLOTS, N_PAGES, A_ROPE, PAGE_SIZE), F32), pltpu.SemaphoreType.DMA((2, DEC_SLOTS))],
    )
    return pl.pallas_call(
        functools.partial(_decode_kernel, page0=layer * n_pool),
        grid_spec=grid_spec,
        out_shape=jax.ShapeDtypeStruct((DEC_BATCH, DEC_SEQ, A_HEADS * HEAD_PAD), BF16),
        compiler_params=_cparams(("arbitrary",)),
        name="mla_decode",
    )(page_table, qlat, ckv_new, kr_new, wuv_p, cache_ckv.reshape(-1, PAGE_SIZE, A_KV_LORA), cache_krope_t.reshape(-1, A_ROPE, PAGE_SIZE))


CTX_B = 32
CTX_C = 8
SEQS_PER_STEP = 8


def _conv_kernel(*refs, tt, nbb, has_hist):
    if has_hist:
        zb_ref, zc_ref, hb_ref, hc_ref = refs[:4]
        refs = refs[4:]
    else:
        zb_ref, zc_ref = refs[:2]
        refs = refs[2:]
    bw_ref, bb_ref, lg_ref, lb_ref, cw_ref, o_ref, nhb_ref, nhc_ref, xb_ref, xc_ref = refs
    i = pl.program_id(1)
    hb_rows, hc_rows = B_KERNEL - 1, C_KERNEL - 1

    @pl.when(i == 0)
    def _():
        xb_ref[:, 0:CTX_B, :] = jnp.zeros((nbb, CTX_B, BRANCH_WIDTH), F32)
        xc_ref[:, 0:CTX_C, :] = jnp.zeros((nbb, CTX_C, BRANCH_WIDTH), F32)
        if has_hist:
            xb_ref[:, CTX_B - hb_rows:CTX_B, :] = hb_ref[...]
            xc_ref[:, CTX_C - hc_rows:CTX_C, :] = hc_ref[...]

    zb = zb_ref[...]
    zc = zc_ref[...]
    glu = zb[:, :BRANCH_WIDTH] * jax.nn.sigmoid(zb[:, BRANCH_WIDTH:])
    u_c = zc[:, BRANCH_WIDTH:2 * BRANCH_WIDTH] * zc[:, 2 * BRANCH_WIDTH:]
    accs_b, accs_c = [], []
    for s in range(nbb):
        xb_ref[s, CTX_B:CTX_B + tt, :] = glu[s * tt:(s + 1) * tt]
        xc_ref[s, CTX_C:CTX_C + tt, :] = u_c[s * tt:(s + 1) * tt]
        x_all = xb_ref[s]
        acc = jnp.zeros((tt, BRANCH_WIDTH), F32)
        for r in range(SUBLANES):
            x_r = x_all if r == 0 else pltpu.roll(x_all, CTX_B + tt - r, axis=0)
            for k in range(B_KERNEL):
                off = CTX_B - hb_rows + k
                if off % SUBLANES == r:
                    acc = acc + x_r[off - r:off - r + tt] * bw_ref[k:k + 1, :]
        accs_b.append(acc)
        acc = jnp.zeros((tt, BRANCH_WIDTH), F32)
        for k in range(C_KERNEL):
            acc = acc + xc_ref[s, pl.ds(CTX_C - hc_rows + k, tt), :] * cw_ref[k:k + 1, :]
        accs_c.append(acc)
    cat = lambda parts: parts[0] if nbb == 1 else jnp.concatenate(parts, axis=0)
    y = cat(accs_b) + bb_ref[...]
    mu = jnp.mean(y, axis=-1, keepdims=True)
    yc = y - mu
    var = jnp.mean(yc * yc, axis=-1, keepdims=True)
    o_b = jax.nn.silu(yc * lax.rsqrt(var + EPS) * lg_ref[...] + lb_ref[...])
    o_ref[...] = jnp.concatenate([o_b, zc[:, :BRANCH_WIDTH] * cat(accs_c)], axis=1).astype(o_ref.dtype)

    new_b = xb_ref[:, tt:tt + CTX_B, :]
    new_c = xc_ref[:, tt:tt + CTX_C, :]
    xb_ref[:, 0:CTX_B, :] = new_b
    xc_ref[:, 0:CTX_C, :] = new_c

    @pl.when(i == pl.num_programs(1) - 1)
    def _():
        nhb_ref[...] = new_b[:, CTX_B - hb_rows:, :]
        nhc_ref[...] = new_c[:, CTX_C - hc_rows:, :]


def conv_branches(z_all, row0, nb, t, tt, nbb, hist_b, hist_c, b_cw, b_cb, ln_g, ln_b, c_cw):
    assert nbb == 1 or tt == t
    nt = t // tt
    rows = nbb * tt
    r0 = row0 // rows
    has_hist = hist_b is not None
    w = BRANCH_WIDTH
    in_specs = [
        pl.BlockSpec((rows, 2 * w), lambda b, i: (r0 + b * nt + i, ZB // (2 * w))),
        pl.BlockSpec((rows, 3 * w), lambda b, i: (r0 + b * nt + i, ZC // (3 * w))),
    ]
    args = [z_all, z_all]
    if has_hist:
        in_specs += [pl.BlockSpec((nbb, B_KERNEL - 1, w), lambda b, i: (b, 0, 0)),
                     pl.BlockSpec((nbb, C_KERNEL - 1, w), lambda b, i: (b, 0, 0))]
        args += [hist_b, hist_c]
    params = [b_cw, b_cb.reshape(1, w), ln_g.reshape(1, w), ln_b.reshape(1, w), c_cw]
    in_specs += [pl.BlockSpec(p.shape, lambda b, i: (0, 0)) for p in params]
    return pl.pallas_call(
        functools.partial(_conv_kernel, tt=tt, nbb=nbb, has_hist=has_hist),
        grid=(nb // nbb, nt),
        in_specs=in_specs,
        out_specs=[pl.BlockSpec((rows, 2 * w), lambda b, i: (b * nt + i, 0)),
                   pl.BlockSpec((nbb, B_KERNEL - 1, w), lambda b, i: (b, 0, 0)),
                   pl.BlockSpec((nbb, C_KERNEL - 1, w), lambda b, i: (b, 0, 0))],
        out_shape=[jax.ShapeDtypeStruct((nb * t, 2 * w), BF16),
                   jax.ShapeDtypeStruct((nb, B_KERNEL - 1, w), F32),
                   jax.ShapeDtypeStruct((nb, C_KERNEL - 1, w), F32)],
        scratch_shapes=[pltpu.VMEM((nbb, CTX_B + tt, w), F32), pltpu.VMEM((nbb, CTX_C + tt, w), F32)],
        compiler_params=_cparams(("parallel", "arbitrary")),
        name="conv_branches",
    )(*args, *params)


HW = D_HEADS * D_KEY
SUB = 16


def _hgrn_kernel(*refs, tb, seqs, has_state):
    if has_state:
        zd_ref, s0_ref = refs[:2]
        refs = refs[2:]
    else:
        zd_ref = refs[0]
        refs = refs[1:]
    lb_ref, gn_ref, o_ref, sfin_ref, st_ref = refs
    i = pl.program_id(1)
    c = min(SUB, tb // seqs)
    n_sub = tb // c
    assert seqs == 1 or (seqs == n_sub and has_state)

    blocks = [slice(h * D_KEY, (h + 1) * D_KEY) for h in range(D_HEADS)]

    def load_state(j):
        for h, blk in enumerate(blocks):
            st_ref[blk, blk] = s0_ref[j, h]
        return st_ref[...]

    def store_state(j, st):
        for h, blk in enumerate(blocks):
            sfin_ref[j, h] = st[blk, blk]

    if seqs == 1:
        @pl.when(i == 0)
        def _():
            st_ref[...] = jnp.zeros(st_ref.shape, F32)
            if has_state:
                load_state(0)
    else:
        st_ref[...] = jnp.zeros(st_ref.shape, F32)

    zd = zd_ref[...]
    lb = lb_ref[...]
    q = jax.nn.silu(zd[:, 0:HW])
    f = lb + (1.0 - lb) * jax.nn.sigmoid(zd[:, HW:2 * HW])
    logf = jnp.log(f)
    kk = 1.0 - f
    v = zd[:, 2 * HW:3 * HW]
    gate = jax.nn.silu(zd[:, 3 * HW:4 * HW])

    row = lax.broadcasted_iota(I32, (tb, tb), 0)
    col = lax.broadcasted_iota(I32, (tb, tb), 1)
    tril_sub = ((row // c == col // c) & (col <= row)).astype(F32)
    g = _dot(tril_sub, logf, precision=HI)

    hrow = lax.broadcasted_iota(I32, (HW, HW), 0) // D_KEY
    hcol = lax.broadcasted_iota(I32, (HW, HW), 1) // D_KEY
    same_head = hrow == hcol
    head_ones = same_head.astype(BF16)

    def pad_tokens(a):
        return a if tb >= LANES else jnp.concatenate([a, jnp.zeros((LANES - tb, HW), a.dtype)], axis=0)

    v_t = pad_tokens(v).T.astype(BF16)
    tok = lax.broadcasted_iota(I32, (tb, HW), 0)
    s_idx = lax.broadcasted_iota(I32, (c, HW), 0)

    pieces = []
    for j in range(n_sub):
        r0 = j * c
        gj, qj, kj = g[r0:r0 + c], q[r0:r0 + c], kk[r0:r0 + c]
        for t in range(c):
            mask = s_idx <= t
            e = jnp.exp(jnp.where(mask, gj[t:t + 1] - gj, 0.0))
            pieces.append(jnp.where(mask, qj[t:t + 1] * kj * e, 0.0))
    att = _dot(jnp.concatenate(pieces, axis=0).astype(BF16), head_ones)
    o_intra = jnp.sum(att.reshape(tb, c, HW) * v.reshape(n_sub, 1, c, HW).repeat(c, axis=1).reshape(tb, c, HW), axis=1)
    q_dec = (q * jnp.exp(g)).astype(BF16)

    upds, decays = [], []
    for j in range(n_sub):
        r0 = j * c
        g_last = g[r0 + c - 1:r0 + c]
        in_sub = (tok >= r0) & (tok < r0 + c)
        kd = jnp.where(in_sub, kk * jnp.exp(jnp.where(in_sub, g_last - g, 0.0)), 0.0)
        upds.append(jnp.where(same_head, _dot(v_t, pad_tokens(kd).astype(BF16)), 0.0))
        decays.append(jnp.exp(g_last))

    o_inter = []
    if seqs == 1:
        st = st_ref[...]
        for j in range(n_sub):
            o_inter.append(_dot(q_dec[j * c:(j + 1) * c], st.astype(BF16), NT_DIMS))
            st = st * decays[j] + upds[j]
        st_ref[...] = st

        @pl.when(i == pl.num_programs(1) - 1)
        def _():
            store_state(0, st)
    else:
        for j in range(n_sub):
            st = load_state(j)
            o_inter.append(_dot(q_dec[j * c:(j + 1) * c], st.astype(BF16), NT_DIMS))
            store_state(j, st * decays[j] + upds[j])

    o = (jnp.concatenate(o_inter, axis=0) if n_sub > 1 else o_inter[0]) + o_intra
    ms = _dot((o * o).astype(BF16), head_ones) * (1.0 / D_VAL)
    o_ref[...] = (o * lax.rsqrt(ms + EPS) * gn_ref[...] * gate).astype(o_ref.dtype)


def hgrn(z_all, row0, nb, t, tb, seqs, s0_t, lb, gn):
    assert (seqs == 1 and t % tb == 0) or tb == seqs * t
    nt = max(t // tb, 1)
    r0 = row0 // tb
    has_state = s0_t is not None
    in_specs = [pl.BlockSpec((tb, 4 * HW), lambda b, i: (r0 + b * nt + i, ZD // (4 * HW)))]
    args = [z_all]
    if has_state:
        in_specs.append(pl.BlockSpec((seqs, D_HEADS, D_VAL, D_KEY), lambda b, i: (b, 0, 0, 0)))
        args.append(s0_t)
    in_specs += [pl.BlockSpec((1, HW), lambda b, i: (0, 0)), pl.BlockSpec((1, HW), lambda b, i: (0, 0))]
    args += [lb.reshape(1, HW), jnp.tile(gn, D_HEADS).reshape(1, HW)]
    return pl.pallas_call(
        functools.partial(_hgrn_kernel, tb=tb, seqs=seqs, has_state=has_state),
        grid=(nb // seqs, nt),
        in_specs=in_specs,
        out_specs=[pl.BlockSpec((tb, HW), lambda b, i: (b * nt + i, 0)),
                   pl.BlockSpec((seqs, D_HEADS, D_VAL, D_KEY), lambda b, i: (b, 0, 0, 0))],
        out_shape=[jax.ShapeDtypeStruct((nb * t, HW), BF16), jax.ShapeDtypeStruct((nb, D_HEADS, D_VAL, D_KEY), F32)],
        scratch_shapes=[pltpu.VMEM((HW, HW), F32)],
        compiler_params=_cparams(("parallel", "arbitrary")),
        name="hgrn",
    )(*args)


def _merge_kernel(oa_ref, obc_ref, od_ref, ga_ref, gb_ref, gc_ref, gd_ref, wa_ref, wb_ref, wc_ref, wd_ref, wo_ref, x_ref, o_ref):
    obc = obc_ref[...]
    merged = jax.nn.sigmoid(ga_ref[...]) * _dot(oa_ref[...], wa_ref[...])
    merged += jax.nn.sigmoid(gb_ref[...]) * _dot(obc[:, :BRANCH_WIDTH], wb_ref[...])
    merged += jax.nn.sigmoid(gc_ref[...]) * _dot(obc[:, BRANCH_WIDTH:], wc_ref[...])
    merged += jax.nn.sigmoid(gd_ref[...]) * _dot(od_ref[...], wd_ref[...])
    o_ref[...] = x_ref[...] + _dot(merged.astype(BF16), wo_ref[...])


def merge(o_a, o_bc, o_d, z_all, wa_p, wb, wc, wd, w_out, x, tm=TOKEN_TILE):
    n = x.shape[0]
    row = lambda w: pl.BlockSpec((tm, w), lambda i: (i, 0))
    full = lambda a: pl.BlockSpec(a.shape, lambda i: (0, 0))
    gate = lambda k: pl.BlockSpec((tm, D_MODEL), lambda i: (i, ZG // D_MODEL + k))
    return pl.pallas_call(
        _merge_kernel,
        grid=(n // tm,),
        in_specs=[row(o_a.shape[1]), row(o_bc.shape[1]), row(o_d.shape[1]), gate(0), gate(1), gate(2), gate(3),
                  full(wa_p), full(wb), full(wc), full(wd), full(w_out), row(D_MODEL)],
        out_specs=row(D_MODEL),
        out_shape=jax.ShapeDtypeStruct((n, D_MODEL), F32),
        compiler_params=_cparams(("parallel",)),
        name="merge",
    )(o_a, o_bc, o_d, z_all, z_all, z_all, z_all, wa_p, wb, wc, wd, w_out, x)


HP = P_HEADS * 2
SLOTS = P_HEADS * P_TOPK
SUBK = P_DKEY // 2


def _take_max(s, index, n):
    m = jnp.max(s, axis=0, keepdims=True)
    idx = jnp.min(jnp.where(s == m, index, n), axis=0, keepdims=True)
    return m, idx, jnp.where(index == idx, -jnp.inf, s)


_CAND_BLOCKS = [(0, P_TOPK)] + [(a, 8) for a in range(1, 8)]
_CAND_ROWS = sum(nb for _, nb in _CAND_BLOCKS) + 8


def _route_kernel(x_ref, g_ref, wq_ref, keys_ref, h_ref, ii_ref, jj_ref, gw_ref, q_s, sv_s, si_s, oi_s, oj_s, og_s, *, tm):
    x = x_ref[...]
    h = (x * lax.rsqrt(jnp.mean(x * x, axis=-1, keepdims=True) + EPS) * g_ref[...]).astype(BF16)
    h_ref[...] = h
    q_s[...] = _dot(h, wq_ref[...]).astype(BF16)

    half = P_NKEYS // 2
    n_lt = tm // LANES
    iota_lo = lax.broadcasted_iota(I32, (half, LANES), 0).astype(F32)
    iota_hi = iota_lo + float(half)

    def stage1(it, carry):
        hp = it // n_lt
        toks = pl.ds(pl.multiple_of((it % n_lt) * LANES, LANES), LANES)
        q = q_s[toks, pl.ds(pl.multiple_of(hp * SUBK, SUBK), SUBK)]
        s = _dot(keys_ref[hp], q, NT_DIMS)
        a, b = s[:half], s[half:]
        first = a >= b
        cur, cur_i = jnp.where(first, a, b), jnp.where(first, iota_lo, iota_hi)
        nxt, nxt_i = jnp.where(first, b, a), jnp.where(first, iota_hi, iota_lo)
        vals, idxs = [], []
        for _ in range(P_TOPK):
            m = jnp.max(cur, axis=0, keepdims=True)
            idx = jnp.min(jnp.where(cur == m, cur_i, float(P_NKEYS)), axis=0, keepdims=True)
            hit = cur_i == idx
            cur, cur_i, nxt = jnp.where(hit, nxt, cur), jnp.where(hit, nxt_i, cur_i), jnp.where(hit, -jnp.inf, nxt)
            vals.append(m)
            idxs.append(idx)
        sv_s[hp, :, toks] = jnp.concatenate(vals, axis=0)
        si_s[hp, :, toks] = jnp.concatenate(idxs, axis=0)
        return carry

    lax.fori_loop(0, HP * n_lt, stage1, 0, unroll=HP * n_lt)

    r = lax.broadcasted_iota(I32, (_CAND_ROWS, tm), 0)
    mid = r - P_TOPK
    flat = jnp.where(r < P_TOPK, r, jnp.where(r < _CAND_ROWS - 8, (1 + mid // 8) * P_TOPK + mid % 8, (r - (_CAND_ROWS - 16)) * P_TOPK))
    flat = flat.astype(F32)
    iota_t = lax.broadcasted_iota(I32, (P_TOPK, tm), 0).astype(F32)

    def stage2(hd, carry):
        sv1, sv2 = sv_s[2 * hd], sv_s[2 * hd + 1]
        si1, si2 = si_s[2 * hd], si_s[2 * hd + 1]
        cand = jnp.concatenate([sv1[a:a + 1] + sv2[0:nb] for a, nb in _CAND_BLOCKS] + [sv1[8:P_TOPK] + sv2[0:1]], axis=0)
        fv, ei, ej = [], [], []
        for _ in range(P_TOPK):
            m, idx, cand = _take_max(cand, flat, float(P_TOPK * P_TOPK))
            a = jnp.floor(idx * (1.0 / P_TOPK))
            b = idx - a * P_TOPK
            fv.append(m)
            ei.append(jnp.sum(jnp.where(iota_t == a, si1, 0.0), axis=0, keepdims=True))
            ej.append(jnp.sum(jnp.where(iota_t == b, si2, 0.0), axis=0, keepdims=True))
        fv = jnp.concatenate(fv, axis=0)
        e = jnp.exp(fv - fv[0:1])
        rows = pl.ds(pl.multiple_of(hd * P_TOPK, P_TOPK), P_TOPK)
        og_s[rows, :] = e / jnp.sum(e, axis=0, keepdims=True)
        oi_s[rows, :] = jnp.concatenate(ei, axis=0)
        oj_s[rows, :] = jnp.concatenate(ej, axis=0)
        return carry

    lax.fori_loop(0, P_HEADS, stage2, 0, unroll=P_HEADS)
    ii_ref[...] = oi_s[...].T.astype(I32)
    jj_ref[...] = oj_s[...].T.astype(I32)
    gw_ref[...] = og_s[...].T


def peer_route(x, g, wq, keys, tm=ROUTE_TILE):
    n = x.shape[0]
    row = lambda w: pl.BlockSpec((tm, w), lambda i: (i, 0))
    return pl.pallas_call(
        functools.partial(_route_kernel, tm=tm),
        grid=(n // tm,),
        in_specs=[row(D_MODEL), pl.BlockSpec((1, D_MODEL), lambda i: (0, 0)),
                  pl.BlockSpec(wq.shape, lambda i: (0, 0)), pl.BlockSpec(keys.shape, lambda i: (0, 0, 0))],
        out_specs=[row(D_MODEL), row(SLOTS), row(SLOTS), row(SLOTS)],
        out_shape=[jax.ShapeDtypeStruct((n, D_MODEL), BF16), jax.ShapeDtypeStruct((n, SLOTS), I32),
                   jax.ShapeDtypeStruct((n, SLOTS), I32), jax.ShapeDtypeStruct((n, SLOTS), F32)],
        scratch_shapes=[pltpu.VMEM((tm, HP * SUBK), BF16), pltpu.VMEM((HP, P_TOPK, tm), F32), pltpu.VMEM((HP, P_TOPK, tm), F32),
                        pltpu.VMEM((SLOTS, tm), F32), pltpu.VMEM((SLOTS, tm), F32), pltpu.VMEM((SLOTS, tm), F32)],
        compiler_params=_cparams(("parallel",)),
        name="peer_route",
    )(x, g.reshape(1, D_MODEL), wq, keys)


CHUNK_I = 16
CHUNK_E = CHUNK_I * P_NKEYS
N_CHUNKS = P_NKEYS // CHUNK_I


def _peer_kernel(h_ref, ii_ref, jj_ref, gw_ref, u_ref, v_ref, x_ref, o_ref, w_s, acc_s, *, tm):
    c = pl.program_id(1)

    @pl.when(c == 0)
    def _():
        acc_s[...] = jnp.zeros(acc_s.shape, F32)
        iota = lax.broadcasted_iota(I32, (P_NKEYS, SLOTS), 0)

        def build(g, carry):
            t0 = pl.multiple_of(g * PACK, PACK)
            ws = []
            for t in range(PACK):
                irow = ii_ref[pl.ds(t0 + t, 1), :]
                jrow = jj_ref[pl.ds(t0 + t, 1), :]
                grow = gw_ref[pl.ds(t0 + t, 1), :]
                p_t = jnp.where(iota == irow, grow, 0.0).astype(BF16)
                q_t = jnp.where(iota == jrow, 1.0, 0.0).astype(BF16)
                ws.append(_dot(p_t, q_t, NT_DIMS).astype(BF16))
            w_s[:, pl.ds(t0, PACK), :] = pltpu.einshape("tij->itj", jnp.stack(ws, axis=0))
            return carry

        lax.fori_loop(0, tm // PACK, build, 0, unroll=4)

    a = _dot(h_ref[...], u_ref[...], NT_DIMS)
    act = 0.5 * a * (1.0 + lax.erf(a * (1.0 / math.sqrt(2.0))))
    wd = jnp.concatenate([w_s[c * CHUNK_I + r] for r in range(CHUNK_I)], axis=1)
    acc_s[...] += _dot(act.astype(BF16) * wd, v_ref[...])

    @pl.when(c == N_CHUNKS - 1)
    def _():
        o_ref[...] = x_ref[...] + acc_s[...]


def peer_experts(layer, h2, ii, jj, gw, u_tabs, v_tabs, x, tm=EXPERT_TILE):
    n = x.shape[0]
    row = lambda w: pl.BlockSpec((tm, w), lambda i, c: (i, 0))
    return pl.pallas_call(
        functools.partial(_peer_kernel, tm=tm),
        grid=(n // tm, N_CHUNKS),
        in_specs=[row(D_MODEL), row(SLOTS), row(SLOTS), row(SLOTS),
                  pl.BlockSpec((None, CHUNK_E, D_MODEL), lambda i, c: (layer, c, 0)),
                  pl.BlockSpec((None, CHUNK_E, D_MODEL), lambda i, c: (layer, c, 0)),
                  row(D_MODEL)],
        out_specs=row(D_MODEL),
        out_shape=jax.ShapeDtypeStruct((n, D_MODEL), F32),
        scratch_shapes=[pltpu.VMEM((P_NKEYS, tm, P_NKEYS), BF16), pltpu.VMEM((tm, D_MODEL), F32)],
        compiler_params=_cparams(("parallel", "arbitrary")),
        name="peer_experts",
    )(h2, ii, jj, gw, u_tabs, v_tabs, x)


def _rmsnorm_kernel(x_ref, g_ref, o_ref):
    x = x_ref[...]
    o_ref[...] = x * lax.rsqrt(jnp.mean(x * x, axis=-1, keepdims=True) + EPS) * g_ref[...]


def rmsnorm(x, g, row0, n, tm=TOKEN_TILE):
    d = x.shape[1]
    r0 = row0 // tm
    return pl.pallas_call(
        _rmsnorm_kernel,
        grid=(n // tm,),
        in_specs=[pl.BlockSpec((tm, d), lambda i: (r0 + i, 0)), pl.BlockSpec((1, d), lambda i: (0, 0))],
        out_specs=pl.BlockSpec((tm, d), lambda i: (i, 0)),
        out_shape=jax.ShapeDtypeStruct((n, d), F32),
        compiler_params=_cparams(("parallel",)),
        name="rmsnorm",
    )(x, g.reshape(1, d))


def _rope_tables():
    half = A_ROPE // 2
    inv = ROPE_THETA ** (-jnp.arange(half, dtype=F32) / half)
    pos = jnp.concatenate([jnp.arange(SEQ), PAST_LEN + jnp.arange(DEC_SEQ)])
    posf = pos.astype(F32)[:, None]

    def tables(start):
        zeros = jnp.zeros((LANES - start - A_ROPE,), F32)
        freq = jnp.concatenate([jnp.zeros((start,), F32), inv, inv, zeros])
        keep = jnp.concatenate([jnp.ones((start + A_ROPE,), F32), zeros])
        sign = jnp.concatenate([jnp.zeros((start,), F32), -jnp.ones((half,), F32), jnp.ones((half,), F32), zeros])
        ang = posf * freq[None, :]
        per_token = lambda t: jnp.concatenate([jnp.tile(t[:SEQ], (BATCH, 1)), jnp.tile(t[SEQ:], (DEC_BATCH, 1))], axis=0)
        return per_token(jnp.cos(ang) * keep[None, :]), per_token(jnp.sin(ang) * sign[None, :])

    ccq, ssq = tables(A_NOPE)
    cck, ssk = tables(0)
    return ccq, ssq, cck, ssk


def _swap_halves(w):
    half = w.shape[-1] // 2
    return jnp.concatenate([w[..., half:], w[..., :half]], axis=-1)


def _layer_weights(l, w_in, a_w_uq, a_w_uk, a_w_uv, w_branch, w_out, p_w_q, p_sub_keys):
    w = w_in[l]
    o = 0
    parts = []
    for size in (A_Q_LORA, A_KV_LORA, A_ROPE, 2 * BRANCH_WIDTH, 3 * BRANCH_WIDTH, 4 * HW, N_BRANCH * D_MODEL):
        parts.append(w[:, o:o + size])
        o += size
    cq, ckv, kr, b_in, c_in, d_in, gate = parts
    pad = jnp.zeros((D_MODEL, ZB - A_Q_LORA - A_KV_LORA - 2 * A_ROPE), F32)
    w_in_p = jnp.concatenate([cq, ckv, kr, _swap_halves(kr), pad, b_in, d_in, gate, c_in], axis=1).astype(BF16)

    wq = a_w_uq[l].reshape(A_Q_LORA, A_HEADS, A_NOPE + A_ROPE)
    nope, rope = wq[..., :A_NOPE], wq[..., A_NOPE:]
    tail = jnp.zeros((A_Q_LORA, A_HEADS, HEAD_PAD - A_NOPE - A_ROPE), F32)
    wq_main = jnp.concatenate([nope, rope, tail], axis=-1).reshape(A_Q_LORA, -1).astype(BF16)
    wq_swap = jnp.concatenate([jnp.zeros_like(nope), _swap_halves(rope), tail], axis=-1).reshape(A_Q_LORA, -1).astype(BF16)

    w_uk, w_uv = a_w_uk[l], a_w_uv[l]
    head_tail = jnp.zeros((A_KV_LORA, A_HEADS, HEAD_PAD - A_NOPE), F32)
    wuk_p = jnp.concatenate([w_uk, head_tail], axis=-1).reshape(A_KV_LORA, -1).astype(BF16)
    wuv_p = jnp.concatenate([w_uv, head_tail], axis=-1).reshape(A_KV_LORA, -1).astype(BF16)

    r = jnp.arange(LANES)[:, None]
    col = jnp.arange(A_HEADS * HEAD_PAD)[None, :]
    place = ((col % HEAD_PAD == A_NOPE + r) & (r < A_ROPE)).astype(BF16)

    blk = jnp.zeros((A_HEADS, HEAD_PAD, 2 * LANES), F32)
    blk = blk.at[:, :A_NOPE, :A_KV_LORA].set(jnp.transpose(w_uk, (1, 2, 0)))
    blk = blk.at[:, A_NOPE:A_NOPE + A_ROPE, A_KV_LORA:A_KV_LORA + A_ROPE].set(jnp.eye(A_ROPE, dtype=F32))
    eye_h = jnp.eye(A_HEADS, dtype=F32)
    wabs = (blk[:, :, None, :] * eye_h[:, None, :, None]).reshape(A_HEADS * HEAD_PAD, A_HEADS * 2 * LANES).astype(BF16)

    wb = w_branch[l]
    wa_p = jnp.concatenate([wb[0].reshape(A_HEADS, A_V, D_MODEL), jnp.zeros((A_HEADS, HEAD_PAD - A_V, D_MODEL), F32)],
                           axis=1).reshape(A_HEADS * HEAD_PAD, D_MODEL).astype(BF16)
    return dict(w_in_p=w_in_p, wq_main=wq_main, wq_swap=wq_swap, wuk_p=wuk_p, wuv_p=wuv_p, place=place, wabs=wabs,
                wa_p=wa_p, wb=wb[1].astype(BF16), wc=wb[2].astype(BF16), wd=wb[3].astype(BF16), w_out=w_out[l].astype(BF16),
                wq=p_w_q[l].astype(BF16), keys=p_sub_keys[l].reshape(HP, P_NKEYS, SUBK).astype(BF16))


def _swap_state(s):
    return jnp.transpose(s, (0, 1, 3, 2))


def kernel(x_prompt, x_sample, cache_ckv, cache_krope, page_table, state_conv_b, state_conv_c, state_hgrn, norm1_g, w_in, a_q_norm_g, a_w_uq, a_kv_norm_g, a_w_uk, a_w_uv, b_conv_w, b_conv_b, b_ln_g, b_ln_b, c_conv_w, d_lower_bound, d_gnorm_g, w_branch, w_out, norm2_g, p_w_q, p_sub_keys, p_u, p_v, final_norm_g):
    lb_soft = jax.nn.softmax(d_lower_bound.astype(F32), axis=0)
    lower_bounds = jnp.cumsum(lb_soft, axis=0) - lb_soft[0:1]
    ccq, ssq, cck, ssk = _rope_tables()
    x = jnp.concatenate([x_prompt.reshape(NP, D_MODEL), x_sample.reshape(NS, D_MODEL)], axis=0)
    cache_krope_t = jnp.swapaxes(cache_krope, 2, 3)
    u_tabs, v_tabs = p_u.astype(BF16), p_v.astype(BF16)

    states_p, states_s = [], []
    for l in range(DEPTH):
        w = _layer_weights(l, w_in, a_w_uq, a_w_uk, a_w_uv, w_branch, w_out, p_w_q, p_sub_keys)
        z = norm_matmul(x, norm1_g[l], w["w_in_p"], *PROJ_TILE)

        ckv, kr, qp, kp, vp = mla_prep(z, a_q_norm_g[l], a_kv_norm_g[l], w["wq_main"], w["wq_swap"], w["wuk_p"], w["wuv_p"],
                                       w["place"], ccq, ssq, cck, ssk)
        oa_p = flash_prompt(qp, kp, vp)
        qlat = matmul(qp[NP:], w["wabs"], TOKEN_TILE, BF16)
        qlat = jnp.transpose(qlat.reshape(DEC_BATCH, DEC_SEQ, A_HEADS, 2 * LANES), (0, 2, 1, 3)).reshape(DEC_BATCH, DEC_ROWS, 2 * LANES)
        oa_s = mla_decode(l, page_table, qlat, cache_ckv, cache_krope_t, ckv[NP:].reshape(DEC_BATCH, DEC_SEQ, LANES),
                          kr[NP:].reshape(DEC_BATCH, DEC_SEQ, LANES), w["wuv_p"])
        o_a = jnp.concatenate([oa_p, oa_s.reshape(NS, -1)], axis=0)

        conv_w = (b_conv_w[l], b_conv_b[l], b_ln_g[l], b_ln_b[l], c_conv_w[l])
        obc_p, hb_p, hc_p = conv_branches(z, 0, BATCH, SEQ, TOKEN_TILE, 1, None, None, *conv_w)
        obc_s, hb_s, hc_s = conv_branches(z, NP, DEC_BATCH, DEC_SEQ, DEC_SEQ, SEQS_PER_STEP, state_conv_b[l], state_conv_c[l], *conv_w)
        o_bc = jnp.concatenate([obc_p, obc_s], axis=0)

        od_p, st_p = hgrn(z, 0, BATCH, SEQ, LANES, 1, None, lower_bounds[l], d_gnorm_g[l])
        od_s, st_s = hgrn(z, NP, DEC_BATCH, DEC_SEQ, SEQS_PER_STEP * DEC_SEQ, SEQS_PER_STEP, _swap_state(state_hgrn[l]), lower_bounds[l], d_gnorm_g[l])
        o_d = jnp.concatenate([od_p, od_s], axis=0)

        x1 = merge(o_a, o_bc, o_d, z, w["wa_p"], w["wb"], w["wc"], w["wd"], w["w_out"], x)
        h2, ii, jj, gw = peer_route(x1, norm2_g[l], w["wq"], w["keys"])
        x = peer_experts(l, h2, ii, jj, gw, u_tabs, v_tabs, x1)

        states_p.append((ckv[:NP].reshape(BATCH, SEQ, A_KV_LORA), kr[:NP, :A_ROPE].reshape(BATCH, SEQ, A_ROPE),
                         hb_p, hc_p, _swap_state(st_p)))
        states_s.append((ckv[NP:].reshape(DEC_BATCH, DEC_SEQ, A_KV_LORA), kr[NP:, :A_ROPE].reshape(DEC_BATCH, DEC_SEQ, A_ROPE),
                         hb_s, hc_s, _swap_state(st_s)))

    y_p = rmsnorm(x, final_norm_g, 0, NP)
    y_s = rmsnorm(x, final_norm_g, NP, NS)
    stack = lambda states: [jnp.stack([s[i] for s in states], axis=0) for i in range(5)]
    return (y_p.reshape(BATCH, SEQ, D_MODEL), y_s.reshape(DEC_BATCH, DEC_SEQ, D_MODEL), *stack(states_p), *stack(states_s))
```

```python
import functools
import math

import jax
import jax.numpy as jnp
from jax import lax
from jax.experimental import pallas as pl
from jax.experimental.pallas import tpu as pltpu

F32 = jnp.float32
BF16 = jnp.bfloat16
I32 = jnp.int32

D_MODEL = 1024
BATCH = 2
SEQ = 8192
DEPTH = 2
DEC_BATCH = 128
DEC_SEQ = 8
PAST_LEN = 8192
PAGE_SIZE = 128
N_PAGES = PAST_LEN // PAGE_SIZE
BRANCH_WIDTH = 256
N_BRANCH = 4
A_HEADS = 4
A_NOPE = 64
A_ROPE = 32
A_V = 64
A_Q_LORA = 256
A_KV_LORA = 128
ROPE_THETA = 10000.0
B_KERNEL = 31
C_KERNEL = 3
D_HEADS = 4
D_KEY = 64
D_VAL = 64
P_HEADS = 8
P_NKEYS = 128
P_DKEY = 256
P_TOPK = 16
P_EXPERTS = P_NKEYS * P_NKEYS
EPS = 1e-6
NEG_BIG = -1e30

NP = BATCH * SEQ
NS = DEC_BATCH * DEC_SEQ
NT = NP + NS

LANES = 128
SUBLANES = 8
PACK = 16
HEAD_PAD = 128
ZA, ZB, ZD, ZG, ZC = 0, 512, 1024, 2048, 6144
Z_COLS = 6912
VMEM_LIMIT = 56 * 1024 * 1024

TOKEN_TILE = 512
PROJ_TILE = (1024, 2304)
ROUTE_TILE = 256
EXPERT_TILE = 512
FLASH_TILE = (1024, 1024)

HI = lax.Precision.HIGHEST


def _cparams(sem):
    return pltpu.CompilerParams(dimension_semantics=sem, vmem_limit_bytes=VMEM_LIMIT)


def _dot(a, b, dims=(((1,), (0,)), ((), ())), precision=None):
    return lax.dot_general(a, b, dims, precision=precision, preferred_element_type=F32)


NT_DIMS = (((1,), (1,)), ((), ()))


def _norm_matmul_kernel(x_ref, g_ref, w_ref, o_ref, h_ref):
    @pl.when(pl.program_id(1) == 0)
    def _():
        x = x_ref[...]
        y = x * lax.rsqrt(jnp.mean(x * x, axis=-1, keepdims=True) + EPS)
        h_ref[...] = (y * g_ref[...]).astype(BF16)

    o_ref[...] = _dot(h_ref[...], w_ref[...])


def norm_matmul(x, g, w, tm, tn):
    n, d = x.shape
    cols = w.shape[1]
    return pl.pallas_call(
        _norm_matmul_kernel,
        grid=(n // tm, cols // tn),
        in_specs=[
            pl.BlockSpec((tm, d), lambda i, j: (i, 0)),
            pl.BlockSpec((1, d), lambda i, j: (0, 0)),
            pl.BlockSpec((d, tn), lambda i, j: (0, j)),
        ],
        out_specs=pl.BlockSpec((tm, tn), lambda i, j: (i, j)),
        out_shape=jax.ShapeDtypeStruct((n, cols), F32),
        scratch_shapes=[pltpu.VMEM((tm, d), BF16)],
        compiler_params=_cparams(("parallel", "arbitrary")),
        name="norm_matmul",
    )(x, g.reshape(1, d), w)


def _matmul_kernel(x_ref, w_ref, o_ref):
    o_ref[...] = _dot(x_ref[...], w_ref[...]).astype(o_ref.dtype)


def matmul(x, w, tm, out_dtype):
    n, d = x.shape
    cols = w.shape[1]
    return pl.pallas_call(
        _matmul_kernel,
        grid=(n // tm,),
        in_specs=[pl.BlockSpec((tm, d), lambda i: (i, 0)), pl.BlockSpec((d, cols), lambda i: (0, 0))],
        out_specs=pl.BlockSpec((tm, cols), lambda i: (i, 0)),
        out_shape=jax.ShapeDtypeStruct((n, cols), out_dtype),
        compiler_params=_cparams(("parallel",)),
        name="matmul",
    )(x, w)


def _tile4(t):
    return jnp.concatenate([t, t, t, t], axis=1)


def _mla_prep_kernel(z_ref, qn_ref, kvn_ref, wqm_ref, wqs_ref, wuk_ref, wuv_ref, place_ref,
                     ccq_ref, ssq_ref, cck_ref, ssk_ref,
                     ckv_ref, kr_ref, q_ref, k_ref, v_ref):
    z = z_ref[...]
    cq = z[:, 0:A_Q_LORA]
    cqn = cq * lax.rsqrt(jnp.mean(cq * cq, axis=-1, keepdims=True) + EPS) * qn_ref[...]
    cqn = cqn.astype(BF16)
    scale = 1.0 / math.sqrt(A_NOPE + A_ROPE)
    q = _dot(cqn, wqm_ref[...]) * _tile4(ccq_ref[...]) + _dot(cqn, wqs_ref[...]) * _tile4(ssq_ref[...])
    q_ref[...] = (q * scale).astype(BF16)

    c = z[:, A_Q_LORA:A_Q_LORA + A_KV_LORA]
    ckv = c * lax.rsqrt(jnp.mean(c * c, axis=-1, keepdims=True) + EPS) * kvn_ref[...]
    ckv_ref[...] = ckv
    ckv_b = ckv.astype(BF16)

    kc = z[:, A_Q_LORA + A_KV_LORA:]
    kr = kc * cck_ref[...] + pltpu.roll(kc, LANES - A_ROPE, axis=1) * ssk_ref[...]
    kr_ref[...] = kr

    k = _dot(ckv_b, wuk_ref[...]) + _dot(kr.astype(BF16), place_ref[...])
    k_ref[...] = k.astype(BF16)
    lane = lax.broadcasted_iota(I32, (1, A_HEADS * HEAD_PAD), 1) % HEAD_PAD
    v_ref[...] = (_dot(ckv_b, wuv_ref[...]) + (lane == A_V).astype(F32)).astype(BF16)


def mla_prep(z_all, qn_g, kvn_g, wq_main, wq_swap, wuk_p, wuv_p, place, ccq, ssq, cck, ssk, tm=TOKEN_TILE):
    n = z_all.shape[0]
    hp = A_HEADS * HEAD_PAD
    row = lambda w: pl.BlockSpec((tm, w), lambda i: (i, 0))
    full = lambda a: pl.BlockSpec(a.shape, lambda i: (0,) * a.ndim)
    qn_g = qn_g.reshape(1, -1)
    kvn_g = kvn_g.reshape(1, -1)
    return pl.pallas_call(
        _mla_prep_kernel,
        grid=(n // tm,),
        in_specs=[row(512), full(qn_g), full(kvn_g), full(wq_main), full(wq_swap), full(wuk_p), full(wuv_p), full(place),
                  row(LANES), row(LANES), row(LANES), row(LANES)],
        out_specs=[row(LANES), row(LANES), row(hp), row(hp), row(hp)],
        out_shape=[jax.ShapeDtypeStruct((n, LANES), F32), jax.ShapeDtypeStruct((n, LANES), F32),
                   jax.ShapeDtypeStruct((n, hp), BF16), jax.ShapeDtypeStruct((n, hp), BF16),
                   jax.ShapeDtypeStruct((n, hp), BF16)],
        compiler_params=_cparams(("parallel",)),
        name="mla_prep",
    )(z_all, qn_g, kvn_g, wq_main, wq_swap, wuk_p, wuv_p, place, ccq, ssq, cck, ssk)


def _flash_kernel(q_ref, k_ref, v_ref, o_ref, *, tq, tk):
    i = pl.program_id(2)
    q = q_ref[...]

    def step(j, carry, masked):
        m, acc = carry
        kj = k_ref[pl.ds(pl.multiple_of(j * tk, tk), tk), :]
        vj = v_ref[pl.ds(pl.multiple_of(j * tk, tk), tk), :]
        s = _dot(q, kj, NT_DIMS)
        if masked:
            qpos = i * tq + lax.broadcasted_iota(I32, (tq, tk), 0)
            kpos = j * tk + lax.broadcasted_iota(I32, (tq, tk), 1)
            s = jnp.where(kpos <= qpos, s, NEG_BIG)
        m_new = jnp.maximum(m, jnp.max(s, axis=-1, keepdims=True))
        p = jnp.exp(s - m_new)
        acc = jnp.exp(m - m_new) * acc + _dot(p.astype(BF16), vj)
        return m_new, acc

    init = (jnp.full((tq, 1), NEG_BIG, F32), jnp.zeros((tq, HEAD_PAD), F32))
    n_full = (i * tq) // tk
    carry = lax.fori_loop(0, n_full // 2, lambda p, c: step(2 * p + 1, step(2 * p, c, False), False), init)
    carry = lax.fori_loop(0, n_full % 2, lambda _, c: step(n_full - 1, c, False), carry)
    for d in range(tq // tk):
        carry = step(n_full + d, carry, True)
    m, acc = carry
    o_ref[...] = (acc / acc[:, A_V:A_V + 1]).astype(o_ref.dtype)


def flash_prompt(qp, kp, vp, tq=FLASH_TILE[0], tk=FLASH_TILE[1]):
    assert tq % tk == 0
    nq = SEQ // tq
    return pl.pallas_call(
        functools.partial(_flash_kernel, tq=tq, tk=tk),
        grid=(BATCH, A_HEADS, nq),
        in_specs=[
            pl.BlockSpec((tq, HEAD_PAD), lambda b, h, i: (b * nq + i, h)),
            pl.BlockSpec((SEQ, HEAD_PAD), lambda b, h, i: (b, h)),
            pl.BlockSpec((SEQ, HEAD_PAD), lambda b, h, i: (b, h)),
        ],
        out_specs=pl.BlockSpec((tq, HEAD_PAD), lambda b, h, i: (b * nq + i, h)),
        out_shape=jax.ShapeDtypeStruct((NP, A_HEADS * HEAD_PAD), BF16),
        compiler_params=_cparams(("parallel", "parallel", "arbitrary")),
        name="flash_prompt",
    )(qp, kp, vp)


DEC_ROWS = A_HEADS * DEC_SEQ
DEC_SLOTS = 2


def _decode_kernel(pt_ref, q_ref, cnew_ref, knew_ref, wuv_ref, ckv_hbm, kr_hbm, o_ref, kbuf, krbuf, sem, *, page0):
    b = pl.program_id(0)
    slot = b % DEC_SLOTS

    def page_copies(seq, sl, r):
        page = page0 + pt_ref[seq, r]
        return (pltpu.make_async_copy(ckv_hbm.at[page], kbuf.at[sl, r], sem.at[0, sl]),
                pltpu.make_async_copy(kr_hbm.at[page], krbuf.at[sl, r], sem.at[1, sl]))

    def start_fetch(seq, sl):
        def body(r, carry):
            for cp in page_copies(seq, sl, r):
                cp.start()
            return carry
        lax.fori_loop(0, N_PAGES, body, 0)

    @pl.when(b == 0)
    def _():
        for ahead in range(DEC_SLOTS - 1):
            start_fetch(ahead, ahead)

    @pl.when(b + DEC_SLOTS - 1 < pl.num_programs(0))
    def _():
        start_fetch(b + DEC_SLOTS - 1, (b + DEC_SLOTS - 1) % DEC_SLOTS)

    def wait_body(r, carry):
        for cp in page_copies(b, slot, r):
            cp.wait()
        return carry

    lax.fori_loop(0, N_PAGES, wait_body, 0)

    q = q_ref[...]
    q_lat = q[:, :A_KV_LORA]
    q_rope = q[:, A_KV_LORA:A_KV_LORA + A_ROPE]
    kc = kbuf[slot].reshape(PAST_LEN, A_KV_LORA).astype(BF16)
    kr_t = jnp.concatenate([krbuf[slot, r] for r in range(N_PAGES)], axis=1).astype(BF16)
    s_past = _dot(q_lat, kc, NT_DIMS) + _dot(q_rope, kr_t)

    cn = cnew_ref[...].astype(BF16)
    kn = knew_ref[...][:, :A_ROPE].astype(BF16)
    s_new = _dot(q_lat, cn, NT_DIMS) + _dot(q_rope, kn, NT_DIMS)
    t_q = lax.broadcasted_iota(I32, s_new.shape, 0) % DEC_SEQ
    t_k = lax.broadcasted_iota(I32, s_new.shape, 1)
    s_new = jnp.where(t_k <= t_q, s_new, NEG_BIG)

    m = jnp.maximum(jnp.max(s_past, axis=-1, keepdims=True), jnp.max(s_new, axis=-1, keepdims=True))
    p_past = jnp.exp(s_past - m)
    p_new = jnp.exp(s_new - m)
    l = jnp.sum(p_past, axis=-1, keepdims=True) + jnp.sum(p_new, axis=-1, keepdims=True)
    acc = _dot(p_past.astype(BF16), kc) + _dot(p_new.astype(BF16), cn)
    o_lat = (acc / l).astype(BF16)
    for h in range(A_HEADS):
        o_ref[:, h * HEAD_PAD:(h + 1) * HEAD_PAD] = _dot(
            o_lat[h * DEC_SEQ:(h + 1) * DEC_SEQ, :], wuv_ref[:, h * HEAD_PAD:(h + 1) * HEAD_PAD]).astype(o_ref.dtype)


def mla_decode(layer, page_table, qlat, cache_ckv, cache_krope_t, ckv_new, kr_new, wuv_p):
    n_pool = cache_ckv.shape[1]
    hbm = pl.BlockSpec(memory_space=pl.ANY)
    grid_spec = pltpu.PrefetchScalarGridSpec(
        num_scalar_prefetch=1,
        grid=(DEC_BATCH,),
        in_specs=[
            pl.BlockSpec((None, DEC_ROWS, 2 * LANES), lambda b, pt: (b, 0, 0)),
            pl.BlockSpec((None, DEC_SEQ, LANES), lambda b, pt: (b, 0, 0)),
            pl.BlockSpec((None, DEC_SEQ, LANES), lambda b, pt: (b, 0, 0)),
            pl.BlockSpec(wuv_p.shape, lambda b, pt: (0, 0)),
            hbm, hbm,
        ],
        out_specs=pl.BlockSpec((None, DEC_SEQ, A_HEADS * HEAD_PAD), lambda b, pt: (b, 0, 0)),
        scratch_shapes=[pltpu.VMEM((DEC_SLOTS, N_PAGES, PAGE_SIZE, A_KV_LORA), F32),
                        pltpu.VMEM((DEC_SLOTS, N_PAGES, A_ROPE, PAGE_SIZE), F32), pltpu.SemaphoreType.DMA((2, DEC_SLOTS))],
    )
    return pl.pallas_call(
        functools.partial(_decode_kernel, page0=layer * n_pool),
        grid_spec=grid_spec,
        out_shape=jax.ShapeDtypeStruct((DEC_BATCH, DEC_SEQ, A_HEADS * HEAD_PAD), BF16),
        compiler_params=_cparams(("arbitrary",)),
        name="mla_decode",
    )(page_table, qlat, ckv_new, kr_new, wuv_p, cache_ckv.reshape(-1, PAGE_SIZE, A_KV_LORA), cache_krope_t.reshape(-1, A_ROPE, PAGE_SIZE))


CTX_B = 32
CTX_C = 8
SEQS_PER_STEP = 8


def _conv_kernel(*refs, tt, nbb, has_hist):
    if has_hist:
        zb_ref, zc_ref, hb_ref, hc_ref = refs[:4]
        refs = refs[4:]
    else:
        zb_ref, zc_ref = refs[:2]
        refs = refs[2:]
    bw_ref, bb_ref, lg_ref, lb_ref, cw_ref, o_ref, nhb_ref, nhc_ref, xb_ref, xc_ref = refs
    i = pl.program_id(1)
    hb_rows, hc_rows = B_KERNEL - 1, C_KERNEL - 1

    @pl.when(i == 0)
    def _():
        xb_ref[:, 0:CTX_B, :] = jnp.zeros((nbb, CTX_B, BRANCH_WIDTH), F32)
        xc_ref[:, 0:CTX_C, :] = jnp.zeros((nbb, CTX_C, BRANCH_WIDTH), F32)
        if has_hist:
            xb_ref[:, CTX_B - hb_rows:CTX_B, :] = hb_ref[...]
            xc_ref[:, CTX_C - hc_rows:CTX_C, :] = hc_ref[...]

    zb = zb_ref[...]
    zc = zc_ref[...]
    glu = zb[:, :BRANCH_WIDTH] * jax.nn.sigmoid(zb[:, BRANCH_WIDTH:])
    u_c = zc[:, BRANCH_WIDTH:2 * BRANCH_WIDTH] * zc[:, 2 * BRANCH_WIDTH:]
    accs_b, accs_c = [], []
    for s in range(nbb):
        xb_ref[s, CTX_B:CTX_B + tt, :] = glu[s * tt:(s + 1) * tt]
        xc_ref[s, CTX_C:CTX_C + tt, :] = u_c[s * tt:(s + 1) * tt]
        x_all = xb_ref[s]
        acc = jnp.zeros((tt, BRANCH_WIDTH), F32)
        for r in range(SUBLANES):
            x_r = x_all if r == 0 else pltpu.roll(x_all, CTX_B + tt - r, axis=0)
            for k in range(B_KERNEL):
                off = CTX_B - hb_rows + k
                if off % SUBLANES == r:
                    acc = acc + x_r[off - r:off - r + tt] * bw_ref[k:k + 1, :]
        accs_b.append(acc)
        acc = jnp.zeros((tt, BRANCH_WIDTH), F32)
        for k in range(C_KERNEL):
            acc = acc + xc_ref[s, pl.ds(CTX_C - hc_rows + k, tt), :] * cw_ref[k:k + 1, :]
        accs_c.append(acc)
    cat = lambda parts: parts[0] if nbb == 1 else jnp.concatenate(parts, axis=0)
    y = cat(accs_b) + bb_ref[...]
    mu = jnp.mean(y, axis=-1, keepdims=True)
    yc = y - mu
    var = jnp.mean(yc * yc, axis=-1, keepdims=True)
    o_b = jax.nn.silu(yc * lax.rsqrt(var + EPS) * lg_ref[...] + lb_ref[...])
    o_ref[...] = jnp.concatenate([o_b, zc[:, :BRANCH_WIDTH] * cat(accs_c)], axis=1).astype(o_ref.dtype)

    new_b = xb_ref[:, tt:tt + CTX_B, :]
    new_c = xc_ref[:, tt:tt + CTX_C, :]
    xb_ref[:, 0:CTX_B, :] = new_b
    xc_ref[:, 0:CTX_C, :] = new_c

    @pl.when(i == pl.num_programs(1) - 1)
    def _():
        nhb_ref[...] = new_b[:, CTX_B - hb_rows:, :]
        nhc_ref[...] = new_c[:, CTX_C - hc_rows:, :]


def conv_branches(z_all, row0, nb, t, tt, nbb, hist_b, hist_c, b_cw, b_cb, ln_g, ln_b, c_cw):
    assert nbb == 1 or tt == t
    nt = t // tt
    rows = nbb * tt
    r0 = row0 // rows
    has_hist = hist_b is not None
    w = BRANCH_WIDTH
    in_specs = [
        pl.BlockSpec((rows, 2 * w), lambda b, i: (r0 + b * nt + i, ZB // (2 * w))),
        pl.BlockSpec((rows, 3 * w), lambda b, i: (r0 + b * nt + i, ZC // (3 * w))),
    ]
    args = [z_all, z_all]
    if has_hist:
        in_specs += [pl.BlockSpec((nbb, B_KERNEL - 1, w), lambda b, i: (b, 0, 0)),
                     pl.BlockSpec((nbb, C_KERNEL - 1, w), lambda b, i: (b, 0, 0))]
        args += [hist_b, hist_c]
    params = [b_cw, b_cb.reshape(1, w), ln_g.reshape(1, w), ln_b.reshape(1, w), c_cw]
    in_specs += [pl.BlockSpec(p.shape, lambda b, i: (0, 0)) for p in params]
    return pl.pallas_call(
        functools.partial(_conv_kernel, tt=tt, nbb=nbb, has_hist=has_hist),
        grid=(nb // nbb, nt),
        in_specs=in_specs,
        out_specs=[pl.BlockSpec((rows, 2 * w), lambda b, i: (b * nt + i, 0)),
                   pl.BlockSpec((nbb, B_KERNEL - 1, w), lambda b, i: (b, 0, 0)),
                   pl.BlockSpec((nbb, C_KERNEL - 1, w), lambda b, i: (b, 0, 0))],
        out_shape=[jax.ShapeDtypeStruct((nb * t, 2 * w), BF16),
                   jax.ShapeDtypeStruct((nb, B_KERNEL - 1, w), F32),
                   jax.ShapeDtypeStruct((nb, C_KERNEL - 1, w), F32)],
        scratch_shapes=[pltpu.VMEM((nbb, CTX_B + tt, w), F32), pltpu.VMEM((nbb, CTX_C + tt, w), F32)],
        compiler_params=_cparams(("parallel", "arbitrary")),
        name="conv_branches",
    )(*args, *params)


HW = D_HEADS * D_KEY
SUB = 16


def _hgrn_kernel(*refs, tb, seqs, has_state):
    if has_state:
        zd_ref, s0_ref = refs[:2]
        refs = refs[2:]
    else:
        zd_ref = refs[0]
        refs = refs[1:]
    lb_ref, gn_ref, o_ref, sfin_ref, st_ref = refs
    i = pl.program_id(1)
    c = min(SUB, tb // seqs)
    n_sub = tb // c
    assert seqs == 1 or (seqs == n_sub and has_state)

    blocks = [slice(h * D_KEY, (h + 1) * D_KEY) for h in range(D_HEADS)]

    def load_state(j):
        for h, blk in enumerate(blocks):
            st_ref[blk, blk] = s0_ref[j, h]
        return st_ref[...]

    def store_state(j, st):
        for h, blk in enumerate(blocks):
            sfin_ref[j, h] = st[blk, blk]

    if seqs == 1:
        @pl.when(i == 0)
        def _():
            st_ref[...] = jnp.zeros(st_ref.shape, F32)
            if has_state:
                load_state(0)
    else:
        st_ref[...] = jnp.zeros(st_ref.shape, F32)

    zd = zd_ref[...]
    lb = lb_ref[...]
    q = jax.nn.silu(zd[:, 0:HW])
    f = lb + (1.0 - lb) * jax.nn.sigmoid(zd[:, HW:2 * HW])
    logf = jnp.log(f)
    kk = 1.0 - f
    v = zd[:, 2 * HW:3 * HW]
    gate = jax.nn.silu(zd[:, 3 * HW:4 * HW])

    row = lax.broadcasted_iota(I32, (tb, tb), 0)
    col = lax.broadcasted_iota(I32, (tb, tb), 1)
    tril_sub = ((row // c == col // c) & (col <= row)).astype(F32)
    g = _dot(tril_sub, logf, precision=HI)

    hrow = lax.broadcasted_iota(I32, (HW, HW), 0) // D_KEY
    hcol = lax.broadcasted_iota(I32, (HW, HW), 1) // D_KEY
    same_head = hrow == hcol
    head_ones = same_head.astype(BF16)

    def pad_tokens(a):
        return a if tb >= LANES else jnp.concatenate([a, jnp.zeros((LANES - tb, HW), a.dtype)], axis=0)

    v_t = pad_tokens(v).T.astype(BF16)
    tok = lax.broadcasted_iota(I32, (tb, HW), 0)
    s_idx = lax.broadcasted_iota(I32, (c, HW), 0)

    pieces = []
    for j in range(n_sub):
        r0 = j * c
        gj, qj, kj = g[r0:r0 + c], q[r0:r0 + c], kk[r0:r0 + c]
        for t in range(c):
            mask = s_idx <= t
            e = jnp.exp(jnp.where(mask, gj[t:t + 1] - gj, 0.0))
            pieces.append(jnp.where(mask, qj[t:t + 1] * kj * e, 0.0))
    att = _dot(jnp.concatenate(pieces, axis=0).astype(BF16), head_ones)
    o_intra = jnp.sum(att.reshape(tb, c, HW) * v.reshape(n_sub, 1, c, HW).repeat(c, axis=1).reshape(tb, c, HW), axis=1)
    q_dec = (q * jnp.exp(g)).astype(BF16)

    upds, decays = [], []
    for j in range(n_sub):
        r0 = j * c
        g_last = g[r0 + c - 1:r0 + c]
        in_sub = (tok >= r0) & (tok < r0 + c)
        kd = jnp.where(in_sub, kk * jnp.exp(jnp.where(in_sub, g_last - g, 0.0)), 0.0)
        upds.append(jnp.where(same_head, _dot(v_t, pad_tokens(kd).astype(BF16)), 0.0))
        decays.append(jnp.exp(g_last))

    o_inter = []
    if seqs == 1:
        st = st_ref[...]
        for j in range(n_sub):
            o_inter.append(_dot(q_dec[j * c:(j + 1) * c], st.astype(BF16), NT_DIMS))
            st = st * decays[j] + upds[j]
        st_ref[...] = st

        @pl.when(i == pl.num_programs(1) - 1)
        def _():
            store_state(0, st)
    else:
        for j in range(n_sub):
            st = load_state(j)
            o_inter.append(_dot(q_dec[j * c:(j + 1) * c], st.astype(BF16), NT_DIMS))
            store_state(j, st * decays[j] + upds[j])

    o = (jnp.concatenate(o_inter, axis=0) if n_sub > 1 else o_inter[0]) + o_intra
    ms = _dot((o * o).astype(BF16), head_ones) * (1.0 / D_VAL)
    o_ref[...] = (o * lax.rsqrt(ms + EPS) * gn_ref[...] * gate).astype(o_ref.dtype)


def hgrn(z_all, row0, nb, t, tb, seqs, s0_t, lb, gn):
    assert (seqs == 1 and t % tb == 0) or tb == seqs * t
    nt = max(t // tb, 1)
    r0 = row0 // tb
    has_state = s0_t is not None
    in_specs = [pl.BlockSpec((tb, 4 * HW), lambda b, i: (r0 + b * nt + i, ZD // (4 * HW)))]
    args = [z_all]
    if has_state:
        in_specs.append(pl.BlockSpec((seqs, D_HEADS, D_VAL, D_KEY), lambda b, i: (b, 0, 0, 0)))
        args.append(s0_t)
    in_specs += [pl.BlockSpec((1, HW), lambda b, i: (0, 0)), pl.BlockSpec((1, HW), lambda b, i: (0, 0))]
    args += [lb.reshape(1, HW), jnp.tile(gn, D_HEADS).reshape(1, HW)]
    return pl.pallas_call(
        functools.partial(_hgrn_kernel, tb=tb, seqs=seqs, has_state=has_state),
        grid=(nb // seqs, nt),
        in_specs=in_specs,
        out_specs=[pl.BlockSpec((tb, HW), lambda b, i: (b * nt + i, 0)),
                   pl.BlockSpec((seqs, D_HEADS, D_VAL, D_KEY), lambda b, i: (b, 0, 0, 0))],
        out_shape=[jax.ShapeDtypeStruct((nb * t, HW), BF16), jax.ShapeDtypeStruct((nb, D_HEADS, D_VAL, D_KEY), F32)],
        scratch_shapes=[pltpu.VMEM((HW, HW), F32)],
        compiler_params=_cparams(("parallel", "arbitrary")),
        name="hgrn",
    )(*args)


def _merge_kernel(oa_ref, obc_ref, od_ref, ga_ref, gb_ref, gc_ref, gd_ref, wa_ref, wb_ref, wc_ref, wd_ref, wo_ref, x_ref, o_ref):
    obc = obc_ref[...]
    merged = jax.nn.sigmoid(ga_ref[...]) * _dot(oa_ref[...], wa_ref[...])
    merged += jax.nn.sigmoid(gb_ref[...]) * _dot(obc[:, :BRANCH_WIDTH], wb_ref[...])
    merged += jax.nn.sigmoid(gc_ref[...]) * _dot(obc[:, BRANCH_WIDTH:], wc_ref[...])
    merged += jax.nn.sigmoid(gd_ref[...]) * _dot(od_ref[...], wd_ref[...])
    o_ref[...] = x_ref[...] + _dot(merged.astype(BF16), wo_ref[...])


def merge(o_a, o_bc, o_d, z_all, wa_p, wb, wc, wd, w_out, x, tm=TOKEN_TILE):
    n = x.shape[0]
    row = lambda w: pl.BlockSpec((tm, w), lambda i: (i, 0))
    full = lambda a: pl.BlockSpec(a.shape, lambda i: (0, 0))
    gate = lambda k: pl.BlockSpec((tm, D_MODEL), lambda i: (i, ZG // D_MODEL + k))
    return pl.pallas_call(
        _merge_kernel,
        grid=(n // tm,),
        in_specs=[row(o_a.shape[1]), row(o_bc.shape[1]), row(o_d.shape[1]), gate(0), gate(1), gate(2), gate(3),
                  full(wa_p), full(wb), full(wc), full(wd), full(w_out), row(D_MODEL)],
        out_specs=row(D_MODEL),
        out_shape=jax.ShapeDtypeStruct((n, D_MODEL), F32),
        compiler_params=_cparams(("parallel",)),
        name="merge",
    )(o_a, o_bc, o_d, z_all, z_all, z_all, z_all, wa_p, wb, wc, wd, w_out, x)


HP = P_HEADS * 2
SLOTS = P_HEADS * P_TOPK
SUBK = P_DKEY // 2


def _take_max(s, index, n):
    m = jnp.max(s, axis=0, keepdims=True)
    idx = jnp.min(jnp.where(s == m, index, n), axis=0, keepdims=True)
    return m, idx, jnp.where(index == idx, -jnp.inf, s)


_CAND_BLOCKS = [(0, P_TOPK)] + [(a, 8) for a in range(1, 8)]
_CAND_ROWS = sum(nb for _, nb in _CAND_BLOCKS) + 8


def _route_kernel(x_ref, g_ref, wq_ref, keys_ref, h_ref, ii_ref, jj_ref, gw_ref, q_s, sv_s, si_s, oi_s, oj_s, og_s, *, tm):
    x = x_ref[...]
    h = (x * lax.rsqrt(jnp.mean(x * x, axis=-1, keepdims=True) + EPS) * g_ref[...]).astype(BF16)
    h_ref[...] = h
    q_s[...] = _dot(h, wq_ref[...]).astype(BF16)

    half = P_NKEYS // 2
    n_lt = tm // LANES
    iota_lo = lax.broadcasted_iota(I32, (half, LANES), 0).astype(F32)
    iota_hi = iota_lo + float(half)

    def stage1(it, carry):
        hp = it // n_lt
        toks = pl.ds(pl.multiple_of((it % n_lt) * LANES, LANES), LANES)
        q = q_s[toks, pl.ds(pl.multiple_of(hp * SUBK, SUBK), SUBK)]
        s = _dot(keys_ref[hp], q, NT_DIMS)
        a, b = s[:half], s[half:]
        first = a >= b
        cur, cur_i = jnp.where(first, a, b), jnp.where(first, iota_lo, iota_hi)
        nxt, nxt_i = jnp.where(first, b, a), jnp.where(first, iota_hi, iota_lo)
        vals, idxs = [], []
        for _ in range(P_TOPK):
            m = jnp.max(cur, axis=0, keepdims=True)
            idx = jnp.min(jnp.where(cur == m, cur_i, float(P_NKEYS)), axis=0, keepdims=True)
            hit = cur_i == idx
            cur, cur_i, nxt = jnp.where(hit, nxt, cur), jnp.where(hit, nxt_i, cur_i), jnp.where(hit, -jnp.inf, nxt)
            vals.append(m)
            idxs.append(idx)
        sv_s[hp, :, toks] = jnp.concatenate(vals, axis=0)
        si_s[hp, :, toks] = jnp.concatenate(idxs, axis=0)
        return carry

    lax.fori_loop(0, HP * n_lt, stage1, 0, unroll=HP * n_lt)

    r = lax.broadcasted_iota(I32, (_CAND_ROWS, tm), 0)
    mid = r - P_TOPK
    flat = jnp.where(r < P_TOPK, r, jnp.where(r < _CAND_ROWS - 8, (1 + mid // 8) * P_TOPK + mid % 8, (r - (_CAND_ROWS - 16)) * P_TOPK))
    flat = flat.astype(F32)
    iota_t = lax.broadcasted_iota(I32, (P_TOPK, tm), 0).astype(F32)

    def stage2(hd, carry):
        sv1, sv2 = sv_s[2 * hd], sv_s[2 * hd + 1]
        si1, si2 = si_s[2 * hd], si_s[2 * hd + 1]
        cand = jnp.concatenate([sv1[a:a + 1] + sv2[0:nb] for a, nb in _CAND_BLOCKS] + [sv1[8:P_TOPK] + sv2[0:1]], axis=0)
        fv, ei, ej = [], [], []
        for _ in range(P_TOPK):
            m, idx, cand = _take_max(cand, flat, float(P_TOPK * P_TOPK))
            a = jnp.floor(idx * (1.0 / P_TOPK))
            b = idx - a * P_TOPK
            fv.append(m)
            ei.append(jnp.sum(jnp.where(iota_t == a, si1, 0.0), axis=0, keepdims=True))
            ej.append(jnp.sum(jnp.where(iota_t == b, si2, 0.0), axis=0, keepdims=True))
        fv = jnp.concatenate(fv, axis=0)
        e = jnp.exp(fv - fv[0:1])
        rows = pl.ds(pl.multiple_of(hd * P_TOPK, P_TOPK), P_TOPK)
        og_s[rows, :] = e / jnp.sum(e, axis=0, keepdims=True)
        oi_s[rows, :] = jnp.concatenate(ei, axis=0)
        oj_s[rows, :] = jnp.concatenate(ej, axis=0)
        return carry

    lax.fori_loop(0, P_HEADS, stage2, 0, unroll=P_HEADS)
    ii_ref[...] = oi_s[...].T.astype(I32)
    jj_ref[...] = oj_s[...].T.astype(I32)
    gw_ref[...] = og_s[...].T


def peer_route(x, g, wq, keys, tm=ROUTE_TILE):
    n = x.shape[0]
    row = lambda w: pl.BlockSpec((tm, w), lambda i: (i, 0))
    return pl.pallas_call(
        functools.partial(_route_kernel, tm=tm),
        grid=(n // tm,),
        in_specs=[row(D_MODEL), pl.BlockSpec((1, D_MODEL), lambda i: (0, 0)),
                  pl.BlockSpec(wq.shape, lambda i: (0, 0)), pl.BlockSpec(keys.shape, lambda i: (0, 0, 0))],
        out_specs=[row(D_MODEL), row(SLOTS), row(SLOTS), row(SLOTS)],
        out_shape=[jax.ShapeDtypeStruct((n, D_MODEL), BF16), jax.ShapeDtypeStruct((n, SLOTS), I32),
                   jax.ShapeDtypeStruct((n, SLOTS), I32), jax.ShapeDtypeStruct((n, SLOTS), F32)],
        scratch_shapes=[pltpu.VMEM((tm, HP * SUBK), BF16), pltpu.VMEM((HP, P_TOPK, tm), F32), pltpu.VMEM((HP, P_TOPK, tm), F32),
                        pltpu.VMEM((SLOTS, tm), F32), pltpu.VMEM((SLOTS, tm), F32), pltpu.VMEM((SLOTS, tm), F32)],
        compiler_params=_cparams(("parallel",)),
        name="peer_route",
    )(x, g.reshape(1, D_MODEL), wq, keys)


CHUNK_I = 16
CHUNK_E = CHUNK_I * P_NKEYS
N_CHUNKS = P_NKEYS // CHUNK_I


def _peer_kernel(h_ref, ii_ref, jj_ref, gw_ref, u_ref, v_ref, x_ref, o_ref, w_s, acc_s, *, tm):
    c = pl.program_id(1)

    @pl.when(c == 0)
    def _():
        acc_s[...] = jnp.zeros(acc_s.shape, F32)
        iota = lax.broadcasted_iota(I32, (P_NKEYS, SLOTS), 0)

        def build(g, carry):
            t0 = pl.multiple_of(g * PACK, PACK)
            ws = []
            for t in range(PACK):
                irow = ii_ref[pl.ds(t0 + t, 1), :]
                jrow = jj_ref[pl.ds(t0 + t, 1), :]
                grow = gw_ref[pl.ds(t0 + t, 1), :]
                p_t = jnp.where(iota == irow, grow, 0.0).astype(BF16)
                q_t = jnp.where(iota == jrow, 1.0, 0.0).astype(BF16)
                ws.append(_dot(p_t, q_t, NT_DIMS).astype(BF16))
            w_s[:, pl.ds(t0, PACK), :] = pltpu.einshape("tij->itj", jnp.stack(ws, axis=0))
            return carry

        lax.fori_loop(0, tm // PACK, build, 0, unroll=4)

    a = _dot(h_ref[...], u_ref[...], NT_DIMS)
    act = 0.5 * a * (1.0 + lax.erf(a * (1.0 / math.sqrt(2.0))))
    wd = jnp.concatenate([w_s[c * CHUNK_I + r] for r in range(CHUNK_I)], axis=1)
    acc_s[...] += _dot(act.astype(BF16) * wd, v_ref[...])

    @pl.when(c == N_CHUNKS - 1)
    def _():
        o_ref[...] = x_ref[...] + acc_s[...]


def peer_experts(layer, h2, ii, jj, gw, u_tabs, v_tabs, x, tm=EXPERT_TILE):
    n = x.shape[0]
    row = lambda w: pl.BlockSpec((tm, w), lambda i, c: (i, 0))
    return pl.pallas_call(
        functools.partial(_peer_kernel, tm=tm),
        grid=(n // tm, N_CHUNKS),
        in_specs=[row(D_MODEL), row(SLOTS), row(SLOTS), row(SLOTS),
                  pl.BlockSpec((None, CHUNK_E, D_MODEL), lambda i, c: (layer, c, 0)),
                  pl.BlockSpec((None, CHUNK_E, D_MODEL), lambda i, c: (layer, c, 0)),
                  row(D_MODEL)],
        out_specs=row(D_MODEL),
        out_shape=jax.ShapeDtypeStruct((n, D_MODEL), F32),
        scratch_shapes=[pltpu.VMEM((P_NKEYS, tm, P_NKEYS), BF16), pltpu.VMEM((tm, D_MODEL), F32)],
        compiler_params=_cparams(("parallel", "arbitrary")),
        name="peer_experts",
    )(h2, ii, jj, gw, u_tabs, v_tabs, x)


def _rmsnorm_kernel(x_ref, g_ref, o_ref):
    x = x_ref[...]
    o_ref[...] = x * lax.rsqrt(jnp.mean(x * x, axis=-1, keepdims=True) + EPS) * g_ref[...]


def rmsnorm(x, g, row0, n, tm=TOKEN_TILE):
    d = x.shape[1]
    r0 = row0 // tm
    return pl.pallas_call(
        _rmsnorm_kernel,
        grid=(n // tm,),
        in_specs=[pl.BlockSpec((tm, d), lambda i: (r0 + i, 0)), pl.BlockSpec((1, d), lambda i: (0, 0))],
        out_specs=pl.BlockSpec((tm, d), lambda i: (i, 0)),
        out_shape=jax.ShapeDtypeStruct((n, d), F32),
        compiler_params=_cparams(("parallel",)),
        name="rmsnorm",
    )(x, g.reshape(1, d))


def _rope_tables():
    half = A_ROPE // 2
    inv = ROPE_THETA ** (-jnp.arange(half, dtype=F32) / half)
    pos = jnp.concatenate([jnp.arange(SEQ), PAST_LEN + jnp.arange(DEC_SEQ)])
    posf = pos.astype(F32)[:, None]

    def tables(start):
        zeros = jnp.zeros((LANES - start - A_ROPE,), F32)
        freq = jnp.concatenate([jnp.zeros((start,), F32), inv, inv, zeros])
        keep = jnp.concatenate([jnp.ones((start + A_ROPE,), F32), zeros])
        sign = jnp.concatenate([jnp.zeros((start,), F32), -jnp.ones((half,), F32), jnp.ones((half,), F32), zeros])
        ang = posf * freq[None, :]
        per_token = lambda t: jnp.concatenate([jnp.tile(t[:SEQ], (BATCH, 1)), jnp.tile(t[SEQ:], (DEC_BATCH, 1))], axis=0)
        return per_token(jnp.cos(ang) * keep[None, :]), per_token(jnp.sin(ang) * sign[None, :])

    ccq, ssq = tables(A_NOPE)
    cck, ssk = tables(0)
    return ccq, ssq, cck, ssk


def _swap_halves(w):
    half = w.shape[-1] // 2
    return jnp.concatenate([w[..., half:], w[..., :half]], axis=-1)


def _layer_weights(l, w_in, a_w_uq, a_w_uk, a_w_uv, w_branch, w_out, p_w_q, p_sub_keys):
    w = w_in[l]
    o = 0
    parts = []
    for size in (A_Q_LORA, A_KV_LORA, A_ROPE, 2 * BRANCH_WIDTH, 3 * BRANCH_WIDTH, 4 * HW, N_BRANCH * D_MODEL):
        parts.append(w[:, o:o + size])
        o += size
    cq, ckv, kr, b_in, c_in, d_in, gate = parts
    pad = jnp.zeros((D_MODEL, ZB - A_Q_LORA - A_KV_LORA - 2 * A_ROPE), F32)
    w_in_p = jnp.concatenate([cq, ckv, kr, _swap_halves(kr), pad, b_in, d_in, gate, c_in], axis=1).astype(BF16)

    wq = a_w_uq[l].reshape(A_Q_LORA, A_HEADS, A_NOPE + A_ROPE)
    nope, rope = wq[..., :A_NOPE], wq[..., A_NOPE:]
    tail = jnp.zeros((A_Q_LORA, A_HEADS, HEAD_PAD - A_NOPE - A_ROPE), F32)
    wq_main = jnp.concatenate([nope, rope, tail], axis=-1).reshape(A_Q_LORA, -1).astype(BF16)
    wq_swap = jnp.concatenate([jnp.zeros_like(nope), _swap_halves(rope), tail], axis=-1).reshape(A_Q_LORA, -1).astype(BF16)

    w_uk, w_uv = a_w_uk[l], a_w_uv[l]
    head_tail = jnp.zeros((A_KV_LORA, A_HEADS, HEAD_PAD - A_NOPE), F32)
    wuk_p = jnp.concatenate([w_uk, head_tail], axis=-1).reshape(A_KV_LORA, -1).astype(BF16)
    wuv_p = jnp.concatenate([w_uv, head_tail], axis=-1).reshape(A_KV_LORA, -1).astype(BF16)

    r = jnp.arange(LANES)[:, None]
    col = jnp.arange(A_HEADS * HEAD_PAD)[None, :]
    place = ((col % HEAD_PAD == A_NOPE + r) & (r < A_ROPE)).astype(BF16)

    blk = jnp.zeros((A_HEADS, HEAD_PAD, 2 * LANES), F32)
    blk = blk.at[:, :A_NOPE, :A_KV_LORA].set(jnp.transpose(w_uk, (1, 2, 0)))
    blk = blk.at[:, A_NOPE:A_NOPE + A_ROPE, A_KV_LORA:A_KV_LORA + A_ROPE].set(jnp.eye(A_ROPE, dtype=F32))
    eye_h = jnp.eye(A_HEADS, dtype=F32)
    wabs = (blk[:, :, None, :] * eye_h[:, None, :, None]).reshape(A_HEADS * HEAD_PAD, A_HEADS * 2 * LANES).astype(BF16)

    wb = w_branch[l]
    wa_p = jnp.concatenate([wb[0].reshape(A_HEADS, A_V, D_MODEL), jnp.zeros((A_HEADS, HEAD_PAD - A_V, D_MODEL), F32)],
                           axis=1).reshape(A_HEADS * HEAD_PAD, D_MODEL).astype(BF16)
    return dict(w_in_p=w_in_p, wq_main=wq_main, wq_swap=wq_swap, wuk_p=wuk_p, wuv_p=wuv_p, place=place, wabs=wabs,
                wa_p=wa_p, wb=wb[1].astype(BF16), wc=wb[2].astype(BF16), wd=wb[3].astype(BF16), w_out=w_out[l].astype(BF16),
                wq=p_w_q[l].astype(BF16), keys=p_sub_keys[l].reshape(HP, P_NKEYS, SUBK).astype(BF16))


def _swap_state(s):
    return jnp.transpose(s, (0, 1, 3, 2))


def kernel(x_prompt, x_sample, cache_ckv, cache_krope, page_table, state_conv_b, state_conv_c, state_hgrn, norm1_g, w_in, a_q_norm_g, a_w_uq, a_kv_norm_g, a_w_uk, a_w_uv, b_conv_w, b_conv_b, b_ln_g, b_ln_b, c_conv_w, d_lower_bound, d_gnorm_g, w_branch, w_out, norm2_g, p_w_q, p_sub_keys, p_u, p_v, final_norm_g):
    lb_soft = jax.nn.softmax(d_lower_bound.astype(F32), axis=0)
    lower_bounds = jnp.cumsum(lb_soft, axis=0) - lb_soft[0:1]
    ccq, ssq, cck, ssk = _rope_tables()
    x = jnp.concatenate([x_prompt.reshape(NP, D_MODEL), x_sample.reshape(NS, D_MODEL)], axis=0)
    cache_krope_t = jnp.swapaxes(cache_krope, 2, 3)
    u_tabs, v_tabs = p_u.astype(BF16), p_v.astype(BF16)

    states_p, states_s = [], []
    for l in range(DEPTH):
        w = _layer_weights(l, w_in, a_w_uq, a_w_uk, a_w_uv, w_branch, w_out, p_w_q, p_sub_keys)
        z = norm_matmul(x, norm1_g[l], w["w_in_p"], *PROJ_TILE)

        ckv, kr, qp, kp, vp = mla_prep(z, a_q_norm_g[l], a_kv_norm_g[l], w["wq_main"], w["wq_swap"], w["wuk_p"], w["wuv_p"],
                                       w["place"], ccq, ssq, cck, ssk)
        oa_p = flash_prompt(qp, kp, vp)
        qlat = matmul(qp[NP:], w["wabs"], TOKEN_TILE, BF16)
        qlat = jnp.transpose(qlat.reshape(DEC_BATCH, DEC_SEQ, A_HEADS, 2 * LANES), (0, 2, 1, 3)).reshape(DEC_BATCH, DEC_ROWS, 2 * LANES)
        oa_s = mla_decode(l, page_table, qlat, cache_ckv, cache_krope_t, ckv[NP:].reshape(DEC_BATCH, DEC_SEQ, LANES),
                          kr[NP:].reshape(DEC_BATCH, DEC_SEQ, LANES), w["wuv_p"])
        o_a = jnp.concatenate([oa_p, oa_s.reshape(NS, -1)], axis=0)

        conv_w = (b_conv_w[l], b_conv_b[l], b_ln_g[l], b_ln_b[l], c_conv_w[l])
        obc_p, hb_p, hc_p = conv_branches(z, 0, BATCH, SEQ, TOKEN_TILE, 1, None, None, *conv_w)
        obc_s, hb_s, hc_s = conv_branches(z, NP, DEC_BATCH, DEC_SEQ, DEC_SEQ, SEQS_PER_STEP, state_conv_b[l], state_conv_c[l], *conv_w)
        o_bc = jnp.concatenate([obc_p, obc_s], axis=0)

        od_p, st_p = hgrn(z, 0, BATCH, SEQ, 2 * LANES, 1, None, lower_bounds[l], d_gnorm_g[l])
        od_s, st_s = hgrn(z, NP, DEC_BATCH, DEC_SEQ, SEQS_PER_STEP * DEC_SEQ, SEQS_PER_STEP, _swap_state(state_hgrn[l]), lower_bounds[l], d_gnorm_g[l])
        o_d = jnp.concatenate([od_p, od_s], axis=0)

        x1 = merge(o_a, o_bc, o_d, z, w["wa_p"], w["wb"], w["wc"], w["wd"], w["w_out"], x)
        h2, ii, jj, gw = peer_route(x1, norm2_g[l], w["wq"], w["keys"])
        x = peer_experts(l, h2, ii, jj, gw, u_tabs, v_tabs, x1)

        states_p.append((ckv[:NP].reshape(BATCH, SEQ, A_KV_LORA), kr[:NP, :A_ROPE].reshape(BATCH, SEQ, A_ROPE),
                         hb_p, hc_p, _swap_state(st_p)))
        states_s.append((ckv[NP:].reshape(DEC_BATCH, DEC_SEQ, A_KV_LORA), kr[NP:, :A_ROPE].reshape(DEC_BATCH, DEC_SEQ, A_ROPE),
                         hb_s, hc_s, _swap_state(st_s)))

    y_p = rmsnorm(x, final_norm_g, 0, NP)
    y_s = rmsnorm(x, final_norm_g, NP, NS)
    stack = lambda states: [jnp.stack([s[i] for s in states], axis=0) for i in range(5)]
    return (y_p.reshape(BATCH, SEQ, D_MODEL), y_s.reshape(DEC_BATCH, DEC_SEQ, D_MODEL), *stack(states_p), *stack(states_s))
```
